```python
import jax, jax.numpy as jnp
from jax import lax
import numpy as np

D_MODEL = 1024
BATCH = 16
SEQ = 4096
DEPTH = 4

GRID_W = 64
CTX_LEN = 256
HEAD_DIM = 64
N_Q_HEADS = 8
N_KV_HEADS = 2
Q_PER_KV = N_Q_HEADS // N_KV_HEADS
ATTN_WIDTH = N_Q_HEADS * HEAD_DIM
KV_WIDTH = N_KV_HEADS * HEAD_DIM
CONV_WIDTH = D_MODEL - ATTN_WIDTH
WINDOW = 128
BLOCK = 128
ROPE_THETA = 10000.0
N_FOURIER_GROUPS = 4
FOURIER_GROUP = D_MODEL // N_FOURIER_GROUPS
D_FF = -(-(-(-8 * D_MODEL // 3)) // 256) * 256
N_EVEN = (DEPTH + 1) // 2
N_ODD = DEPTH // 2
B_START = 0
C_START = CONV_WIDTH
H_START = 2 * CONV_WIDTH
Q_START = 3 * CONV_WIDTH
K_START = Q_START + ATTN_WIDTH
V_START = K_START + KV_WIDTH
IN_WIDTH = V_START + KV_WIDTH
EPS = 1e-6

kernel_name = "hybrid_conv_swa_fourier_dit_block"


def rmsnorm(x, g):
    xf = x.astype(jnp.float32)
    y = xf * lax.rsqrt(jnp.mean(xf * xf, axis=-1, keepdims=True) + EPS)
    return (y * g.astype(jnp.float32)).astype(x.dtype)


def modulate(h, shift, scale):
    return h * (1 + scale) + shift


def adaln(cond, w, bias):
    return jax.nn.silu(cond) @ w + bias


def swiglu(h, wg, wu, wd):
    return (jax.nn.silu(h @ wg) * (h @ wu)) @ wd


def axial_rope_tables(n, dtype):
    rows = n // GRID_W
    r = jnp.repeat(jnp.arange(rows), GRID_W).astype(jnp.float32)
    col = jnp.tile(jnp.arange(GRID_W), rows).astype(jnp.float32)
    quarter = HEAD_DIM // 4
    inv = ROPE_THETA ** (-jnp.arange(quarter, dtype=jnp.float32) / quarter)
    ang_r = (r[:, None] * inv)[:, None, :]
    ang_c = (col[:, None] * inv)[:, None, :]
    return (jnp.cos(ang_r).astype(dtype), jnp.sin(ang_r).astype(dtype),
            jnp.cos(ang_c).astype(dtype), jnp.sin(ang_c).astype(dtype))


def _rope_half(x, cos, sin):
    h = x.shape[-1] // 2
    x1, x2 = x[..., :h], x[..., h:]
    return jnp.concatenate([x1 * cos - x2 * sin, x2 * cos + x1 * sin], axis=-1)


def apply_axial_rope(x, tabs):
    cr, sr, cc, sc = tabs
    half = HEAD_DIM // 2
    return jnp.concatenate([_rope_half(x[..., :half], cr, sr),
                            _rope_half(x[..., half:], cc, sc)], axis=-1)


def short_conv(u, w):
    n = u.shape[1]
    up = jnp.pad(u, ((0, 0), (1, 1), (0, 0)))
    return up[:, :n] * w[0] + up[:, 1:n + 1] * w[1] + up[:, 2:] * w[2]


def gated_conv(p, conv_w):
    bg = p[..., B_START:C_START]
    cg = p[..., C_START:H_START]
    hv = p[..., H_START:Q_START]
    return bg * short_conv(cg * hv, conv_w)


def window_attention(q, k, v, kc, vc, sink):
    b, n = q.shape[:2]
    nb = n // BLOCK
    scale = HEAD_DIM ** -0.5
    qb = q.reshape(b, nb, BLOCK, N_KV_HEADS, Q_PER_KV, HEAD_DIM)

    def band(t):
        tp = jnp.pad(t, ((0, 0), (BLOCK, BLOCK), (0, 0), (0, 0)))
        tp = tp.reshape(b, nb + 2, BLOCK, N_KV_HEADS, HEAD_DIM)
        return jnp.concatenate([tp[:, :-2], tp[:, 1:-1], tp[:, 2:]], axis=2)

    kb, vb = band(k), band(v)
    qpos = jnp.arange(n).reshape(nb, BLOCK)
    kpos = jnp.arange(nb)[:, None] * BLOCK - BLOCK + jnp.arange(3 * BLOCK)[None, :]
    dist = qpos[:, :, None] - kpos[:, None, :]
    valid = (jnp.abs(dist) <= WINDOW) & (kpos[:, None, :] >= 0) & (kpos[:, None, :] < n)
    s_loc = jnp.einsum('bnqhgd,bnkhd->bnhgqk', qb, kb).astype(jnp.float32) * scale
    s_loc = jnp.where(valid[None, :, None, None], s_loc, -jnp.inf)
    s_ctx = jnp.einsum('bnqhgd,bchd->bnhgqc', qb, kc).astype(jnp.float32) * scale
    s_sink = jnp.broadcast_to(sink.astype(jnp.float32).reshape(N_KV_HEADS, Q_PER_KV)[None, None, :, :, None, None],
                              s_loc.shape[:-1] + (1,))
    probs = jax.nn.softmax(jnp.concatenate([s_loc, s_ctx, s_sink], axis=-1), axis=-1).astype(v.dtype)
    n_loc = 3 * BLOCK
    n_ctx = kc.shape[1]
    out = (jnp.einsum('bnhgqk,bnkhd->bnqhgd', probs[..., :n_loc], vb)
           + jnp.einsum('bnhgqc,bchd->bnqhgd', probs[..., n_loc:n_loc + n_ctx], vc))
    return out.reshape(b, n, ATTN_WIDTH)


def context_attention(q, k, v, sink):
    b, L = q.shape[:2]
    scale = HEAD_DIM ** -0.5
    qg = q.reshape(b, L, N_KV_HEADS, Q_PER_KV, HEAD_DIM)
    s = jnp.einsum('blhgd,bchd->bhglc', qg, k).astype(jnp.float32) * scale
    s_sink = jnp.broadcast_to(sink.astype(jnp.float32).reshape(N_KV_HEADS, Q_PER_KV)[None, :, :, None, None],
                              s.shape[:-1] + (1,))
    probs = jax.nn.softmax(jnp.concatenate([s, s_sink], axis=-1), axis=-1).astype(v.dtype)
    out = jnp.einsum('bhglc,bchd->blhgd', probs[..., :-1], v)
    return out.reshape(b, L, ATTN_WIDTH)


def fourier_mix(h):
    b, n, _ = h.shape
    hg = h.astype(jnp.float32).reshape(b, n, N_FOURIER_GROUPS, FOURIER_GROUP)
    f = jnp.fft.fft2(hg, axes=(1, 3), norm='ortho').real
    return f.reshape(b, n, D_MODEL).astype(h.dtype)


def setup_inputs(seed: int = 0) -> dict:
    key = jax.random.key(seed)
    ks = jax.random.split(key, 20)
    f32 = jnp.float32
    d = D_MODEL
    return {
        'x': jax.random.normal(ks[0], (BATCH, SEQ, d), f32),
        'c': jax.random.normal(ks[1], (BATCH, d), f32),
        'ctx': jax.random.normal(ks[2], (BATCH, CTX_LEN, d), f32),
        'c_ctx': jax.random.normal(ks[3], (d,), f32),
        'w_ada': jax.random.normal(ks[4], (DEPTH, d, 6 * d), f32) * (0.5 * d ** -0.5),
        'b_ada': jax.random.normal(ks[5], (DEPTH, 6 * d), f32) * 0.01,
        'norm1_g': 1.0 + 0.01 * jax.random.normal(ks[6], (DEPTH, d), f32),
        'norm2_g': 1.0 + 0.01 * jax.random.normal(ks[7], (DEPTH, d), f32),
        'w_in': jax.random.normal(ks[8], (N_EVEN, d, IN_WIDTH), f32) * d ** -0.5,
        'conv_w': jax.random.normal(ks[9], (N_EVEN, 3, CONV_WIDTH), f32) * 3 ** -0.5,
        'sink': jax.random.normal(ks[10], (N_EVEN, N_Q_HEADS), f32),
        'w_mix_out': jax.random.normal(ks[11], (N_EVEN, CONV_WIDTH + ATTN_WIDTH, d), f32) * (CONV_WIDTH + ATTN_WIDTH) ** -0.5,
        'w_fourier_out': jax.random.normal(ks[12], (N_ODD, d, d), f32) * d ** -0.5,
        'w_ffn_gate': jax.random.normal(ks[13], (DEPTH, d, D_FF), f32) * d ** -0.5,
        'w_ffn_up': jax.random.normal(ks[14], (DEPTH, d, D_FF), f32) * d ** -0.5,
        'w_ffn_down': jax.random.normal(ks[15], (DEPTH, D_FF, d), f32) * D_FF ** -0.5,
        'final_g': 1.0 + 0.01 * jax.random.normal(ks[16], (d,), f32),
    }


def reference(x, c, ctx, c_ctx, w_ada, b_ada, norm1_g, norm2_g, w_in, conv_w, sink,
              w_mix_out, w_fourier_out, w_ffn_gate, w_ffn_up, w_ffn_down, final_g):
    b, n, _ = x.shape
    L = ctx.shape[1]
    rope_tabs = axial_rope_tables(n, x.dtype)
    xc = ctx
    for l in range(DEPTH):
        ctx_after = any(j % 2 == 0 for j in range(l + 1, DEPTH))
        ctx_here = (l % 2 == 0) or ctx_after
        sh1, sc1, g1, sh2, sc2, g2 = jnp.split(adaln(c, w_ada[l], b_ada[l])[:, None, :], 6, axis=-1)
        h = modulate(rmsnorm(x, norm1_g[l]), sh1, sc1)
        if ctx_here:
            csh1, csc1, cg1, csh2, csc2, cg2 = jnp.split(adaln(c_ctx, w_ada[l], b_ada[l]), 6, axis=-1)
            hc = modulate(rmsnorm(xc, norm1_g[l]), csh1, csc1)
        if l % 2 == 0:
            e = l // 2
            if ctx_after:
                pc = hc @ w_in[e]
                kvc = pc[..., K_START:]
            else:
                kvc = hc @ w_in[e][:, K_START:]
            kc = kvc[..., :KV_WIDTH].reshape(b, L, N_KV_HEADS, HEAD_DIM)
            vc = kvc[..., KV_WIDTH:].reshape(b, L, N_KV_HEADS, HEAD_DIM)
            p = h @ w_in[e]
            a_out = gated_conv(p, conv_w[e])
            q = apply_axial_rope(p[..., Q_START:K_START].reshape(b, n, N_Q_HEADS, HEAD_DIM), rope_tabs)
            k = apply_axial_rope(p[..., K_START:V_START].reshape(b, n, N_KV_HEADS, HEAD_DIM), rope_tabs)
            v = p[..., V_START:].reshape(b, n, N_KV_HEADS, HEAD_DIM)
            b_out = window_attention(q, k, v, kc, vc, sink[e])
            x = x + g1 * (jnp.concatenate([a_out, b_out], axis=-1) @ w_mix_out[e])
            if ctx_after:
                a_c = gated_conv(pc, conv_w[e])
                qc = pc[..., Q_START:K_START].reshape(b, L, N_Q_HEADS, HEAD_DIM)
                b_c = context_attention(qc, kc, vc, sink[e])
                xc = xc + cg1 * (jnp.concatenate([a_c, b_c], axis=-1) @ w_mix_out[e])
        else:
            o = l // 2
            x = x + g1 * (fourier_mix(h) @ w_fourier_out[o])
            if ctx_after:
                xc = xc + cg1 * (fourier_mix(hc) @ w_fourier_out[o])
        h2 = modulate(rmsnorm(x, norm2_g[l]), sh2, sc2)
        x = x + g2 * swiglu(h2, w_ffn_gate[l], w_ffn_up[l], w_ffn_down[l])
        if ctx_after:
            hc2 = modulate(rmsnorm(xc, norm2_g[l]), csh2, csc2)
            xc = xc + cg2 * swiglu(hc2, w_ffn_gate[l], w_ffn_up[l], w_ffn_down[l])
    return rmsnorm(x, final_g)
```

```python
import functools

import numpy as np
import jax
import jax.numpy as jnp
from jax import lax
from jax.experimental import pallas as pl
from jax.experimental.pallas import tpu as pltpu

F32 = jnp.float32
BF16 = jnp.bfloat16

D_MODEL = 1024
DEPTH = 4
GRID_W = 64
HEAD_DIM = 64
N_Q_HEADS = 8
N_KV_HEADS = 2
ATTN_WIDTH = N_Q_HEADS * HEAD_DIM
KV_WIDTH = N_KV_HEADS * HEAD_DIM
CONV_WIDTH = D_MODEL - ATTN_WIDTH
WINDOW = 128
BLOCK = 128
ROPE_THETA = 10000.0
N_FOURIER_GROUPS = 4
FOURIER_GROUP = D_MODEL // N_FOURIER_GROUPS
D_FF = 2816
EPS = 1e-6
Q_START = 3 * CONV_WIDTH
K_START = Q_START + ATTN_WIDTH
V_START = K_START + KV_WIDTH

LANES = 128
SUBLANES = 8
KV_REP = 2 * KV_WIDTH
NEG_BIG = -1e30
COND_ROWS = 24
VMEM_LIMIT = 56 * 1024 * 1024


def _params(n_axes):
    return pltpu.CompilerParams(dimension_semantics=("arbitrary",) * n_axes,
                                vmem_limit_bytes=VMEM_LIMIT)


def _const_spec(shape):
    nd = len(shape)
    return pl.BlockSpec(shape, lambda *_: (0,) * nd, pipeline_mode=pl.Buffered(1))


def _norm_mod(x, g, shift, scale):
    ms = jnp.mean(x * x, axis=-1, keepdims=True)
    y = x * lax.rsqrt(ms + EPS)
    return (y * g) * (1.0 + scale) + shift


def _silu(x):
    return x * (1.0 / (1.0 + jnp.exp(-x)))


def _adaln_kernel(c_ref, w_ref, b_ref, o_ref):
    a = _silu(c_ref[...])
    w = w_ref[0]
    a_hi = a.astype(BF16)
    a_lo = (a - a_hi.astype(F32)).astype(BF16)
    w_hi = w.astype(BF16)
    w_lo = (w - w_hi.astype(F32)).astype(BF16)
    acc = jnp.dot(a_hi, w_hi, preferred_element_type=F32)
    acc += jnp.dot(a_lo, w_hi, preferred_element_type=F32)
    acc += jnp.dot(a_hi, w_lo, preferred_element_type=F32)
    o_ref[0] = acc + b_ref[0]


def _adaln(cond, w_ada, b_ada):
    tn = 1024
    nt = (6 * D_MODEL) // tn
    return pl.pallas_call(
        _adaln_kernel,
        grid=(DEPTH, nt),
        in_specs=[
            pl.BlockSpec((COND_ROWS, D_MODEL), lambda l, j: (0, 0)),
            pl.BlockSpec((1, D_MODEL, tn), lambda l, j: (l, 0, j)),
            pl.BlockSpec((1, 1, tn), lambda l, j: (l, 0, j)),
        ],
        out_specs=pl.BlockSpec((1, COND_ROWS, tn), lambda l, j: (l, 0, j)),
        out_shape=jax.ShapeDtypeStruct((DEPTH, COND_ROWS, 6 * D_MODEL), F32),
        compiler_params=_params(2),
        name="adaln",
    )(cond, w_ada, b_ada.reshape(DEPTH, 1, 6 * D_MODEL))


IN_EXT = Q_START + ATTN_WIDTH + 2 * KV_REP


def _inproj_kernel(x_ref, sh_ref, sc_ref, g_ref, w_ref, cos_ref, sa_ref, sb_ref,
                   bgu_ref, q_ref, kk_ref, vv_ref):
    h = _norm_mod(x_ref[0], g_ref[...], sh_ref[0], sc_ref[0]).astype(BF16)
    p = jnp.dot(h, w_ref[...], preferred_element_type=F32)
    bgu_ref[0, :, :CONV_WIDTH] = p[:, :CONV_WIDTH].astype(BF16)
    bgu_ref[0, :, CONV_WIDTH:] = (p[:, CONV_WIDTH:2 * CONV_WIDTH]
                                  * p[:, 2 * CONV_WIDTH:Q_START]).astype(BF16)
    cos = cos_ref[...]
    sa = sa_ref[...]
    sb = sb_ref[...]

    def rope(t):
        return t * cos + pltpu.roll(t, LANES - 16, 1) * sa + pltpu.roll(t, 16, 1) * sb

    for j in range(ATTN_WIDTH // LANES):
        lo = Q_START + j * LANES
        q_ref[0, :, j * LANES:(j + 1) * LANES] = (
            rope(p[:, lo:lo + LANES]) * (HEAD_DIM ** -0.5)).astype(BF16)
    k0 = Q_START + ATTN_WIDTH
    for j in range(KV_REP // LANES):
        lo = k0 + j * LANES
        kk_ref[0, :, j * LANES:(j + 1) * LANES] = rope(p[:, lo:lo + LANES]).astype(BF16)
    vv_ref[0] = p[:, k0 + KV_REP:].astype(BF16)


def _inproj(x, shift, scale, g, w_ext, cos, sa, sb, tm):
    b, n, _ = x.shape
    nt = n // tm
    tok = lambda w: pl.BlockSpec((1, tm, w), lambda i, bb: (bb, i, 0))
    mod = pl.BlockSpec((1, 1, D_MODEL), lambda i, bb: (bb, 0, 0))
    tab = pl.BlockSpec((tm, LANES), lambda i, bb: (i, 0))
    return pl.pallas_call(
        _inproj_kernel,
        grid=(nt, b),
        in_specs=[tok(D_MODEL), mod, mod, _const_spec((1, D_MODEL)),
                  _const_spec((D_MODEL, IN_EXT)), tab, tab, tab],
        out_specs=[tok(D_MODEL), tok(ATTN_WIDTH), tok(KV_REP), tok(KV_REP)],
        out_shape=[jax.ShapeDtypeStruct((b, n, D_MODEL), BF16),
                   jax.ShapeDtypeStruct((b, n, ATTN_WIDTH), BF16),
                   jax.ShapeDtypeStruct((b, n, KV_REP), BF16),
                   jax.ShapeDtypeStruct((b, n, KV_REP), BF16)],
        compiler_params=_params(2),
        name="inproj",
    )(x, shift, scale, g, w_ext, cos, sa, sb)


def _softmax_parts(parts, sink_col):
    m = sink_col
    for s in parts:
        m = jnp.maximum(m, jnp.max(s, axis=-1, keepdims=True))
    es = [jnp.exp(s - m) for s in parts]
    den = jnp.exp(sink_col - m)
    for e in es:
        den = den + jnp.sum(e, axis=-1, keepdims=True)
    return es, 1.0 / den


def _mixout_kernel(*refs, tile_rows, local):
    if local:
        (sink_ref, x_ref, gate_ref, bgu_ref, bgu_p_ref, bgu_n_ref, q_ref,
         kk_ref, kk_p_ref, kk_n_ref, vv_ref, vv_p_ref, vv_n_ref, kkc_ref, vvc_ref,
         bias_mid_ref, bias_a_ref, bias_b_ref, cw_ref, wm_ref, o_ref) = refs
    else:
        (sink_ref, x_ref, gate_ref, bgu_ref, bgu_p_ref, bgu_n_ref, q_ref,
         kkc_ref, vvc_ref, cw_ref, wm_ref, o_ref) = refs
    i = pl.program_id(0)
    n_tiles = pl.num_programs(0)
    r = tile_rows
    nb = r // BLOCK

    bg = bgu_ref[0, :, :CONV_WIDTH].astype(F32)
    u = bgu_ref[0, :, CONV_WIDTH:].astype(F32)
    has_prev = (i > 0).astype(F32)
    has_next = (i < n_tiles - 1).astype(F32)
    u_prev = bgu_p_ref[0, SUBLANES - 1:SUBLANES, CONV_WIDTH:].astype(F32) * has_prev
    u_next = bgu_n_ref[0, 0:1, CONV_WIDTH:].astype(F32) * has_next
    row = lax.broadcasted_iota(jnp.int32, (r, CONV_WIDTH), 0)
    u_up = jnp.where(row == 0, u_prev, pltpu.roll(u, 1, 0))
    u_dn = jnp.where(row == r - 1, u_next, pltpu.roll(u, r - 1, 0))
    cw = cw_ref[...]
    a_out = (bg * (u_up * cw[0:1] + u * cw[1:2] + u_dn * cw[2:3])).astype(BF16)

    lane = lax.broadcasted_iota(jnp.int32, (1, LANES), 1)
    first_half = lane < HEAD_DIM
    rows2 = lax.broadcasted_iota(jnp.int32, (2 * BLOCK, 1), 0)
    top = rows2 < BLOCK
    zero = jnp.zeros((), BF16)

    def halves(t):
        return jnp.concatenate([jnp.where(first_half, t, zero), jnp.where(first_half, zero, t)], axis=0)

    b_cols = []
    for h in range(N_KV_HEADS):
        hs = slice(h * LANES, (h + 1) * LANES)
        kc_bd = halves(kkc_ref[0, :, hs])
        vc_bd = halves(vvc_ref[0, :, hs])
        n_ctx = kkc_ref.shape[1]
        if local:
            k_all = jnp.concatenate([kk_p_ref[0, :, hs], kk_ref[0, :, hs], kk_n_ref[0, :, hs]], axis=0)
            v_all = jnp.concatenate([vv_p_ref[0, :, hs], vv_ref[0, :, hs], vv_n_ref[0, :, hs]], axis=0)
        sink_a = jnp.where(top, sink_ref[4 * h + 0], sink_ref[4 * h + 2])
        sink_b = jnp.where(top, sink_ref[4 * h + 1], sink_ref[4 * h + 3])
        outs = []
        for j in range(nb):
            rs = slice(j * BLOCK, (j + 1) * BLOCK)
            q2 = jnp.concatenate([q_ref[0, rs, 2 * h * LANES:(2 * h + 1) * LANES],
                                  q_ref[0, rs, (2 * h + 1) * LANES:(2 * h + 2) * LANES]], axis=0)
            s_ctx = lax.dot_general(q2, kc_bd, (((1,), (1,)), ((), ())), preferred_element_type=F32)
            parts_a = [s_ctx[:, :n_ctx]]
            parts_b = [s_ctx[:, n_ctx:]]
            if local:
                nl = 3 * BLOCK
                kl_bd = halves(k_all[j * BLOCK:j * BLOCK + nl])
                vl_bd = halves(v_all[j * BLOCK:j * BLOCK + nl])
                s_loc = lax.dot_general(q2, kl_bd, (((1,), (1,)), ((), ())), preferred_element_type=F32)
                if nb == 1:
                    bias = bias_a_ref[0] + bias_b_ref[0]
                elif j == 0:
                    bias = bias_a_ref[0]
                elif j == nb - 1:
                    bias = bias_b_ref[0]
                else:
                    bias = bias_mid_ref[0]
                s_loc = s_loc + bias
                parts_a.append(s_loc[:, :nl])
                parts_b.append(s_loc[:, nl:])
            es_a, inv_a = _softmax_parts(parts_a, sink_a)
            es_b, inv_b = _softmax_parts(parts_b, sink_b)
            e_ctx = jnp.concatenate([es_a[0], es_b[0]], axis=1).astype(BF16)
            o = jnp.dot(e_ctx, vc_bd, preferred_element_type=F32)
            if local:
                e_loc = jnp.concatenate([es_a[1], es_b[1]], axis=1).astype(BF16)
                o = o + jnp.dot(e_loc, vl_bd, preferred_element_type=F32)
            o = o * jnp.where(first_half, inv_a, inv_b)
            outs.append(jnp.concatenate([o[:BLOCK], o[BLOCK:]], axis=1))
        b_cols.append(jnp.concatenate(outs, axis=0) if nb > 1 else outs[0])
    b_out = jnp.concatenate(b_cols, axis=1).astype(BF16)

    mix = jnp.dot(a_out, wm_ref[:CONV_WIDTH, :], preferred_element_type=F32)
    mix = mix + jnp.dot(b_out, wm_ref[CONV_WIDTH:, :], preferred_element_type=F32)
    o_ref[0] = x_ref[0] + gate_ref[0] * mix


def _band_bias():
    rr = np.arange(2 * BLOCK)[:, None] % BLOCK
    kidx = np.arange(6 * BLOCK)[None, :] % (3 * BLOCK)
    band = (kidx >= rr) & (kidx <= rr + 2 * WINDOW)
    mid = band
    first = band & (kidx >= BLOCK)
    last = band & (kidx < 2 * BLOCK)
    return jnp.asarray(np.where(np.stack([mid, first, last]), 0.0, NEG_BIG), dtype=F32)


def _mixout(x, gate, bgu, q, kk, vv, kkc, vvc, sink, conv_w, wm, tile_rows, local):
    b, n, _ = x.shape
    r = tile_rows
    nt = n // r
    nctx = kkc.shape[1]
    tok = lambda w: pl.BlockSpec((1, r, w), lambda i, bb: (bb, i, 0))
    mod = pl.BlockSpec((1, 1, D_MODEL), lambda i, bb: (bb, 0, 0))
    r8 = r // SUBLANES
    halo8_p = pl.BlockSpec((1, SUBLANES, D_MODEL), lambda i, bb: (bb, jnp.maximum(i * r8 - 1, 0), 0))
    halo8_n = pl.BlockSpec((1, SUBLANES, D_MODEL),
                           lambda i, bb: (bb, jnp.minimum((i + 1) * r8, n // SUBLANES - 1), 0))
    ctx = pl.BlockSpec((1, nctx, KV_REP), lambda i, bb: (bb, 0, 0))
    smem = pl.BlockSpec(memory_space=pltpu.SMEM)
    in_specs = [smem, tok(D_MODEL), mod, tok(D_MODEL), halo8_p, halo8_n, tok(ATTN_WIDTH)]
    args = [sink, x, gate, bgu, bgu, bgu, q]
    if local:
        rb = r // BLOCK
        halo_p = pl.BlockSpec((1, BLOCK, KV_REP), lambda i, bb: (bb, jnp.maximum(i * rb - 1, 0), 0))
        halo_n = pl.BlockSpec((1, BLOCK, KV_REP),
                              lambda i, bb: (bb, jnp.minimum((i + 1) * rb, n // BLOCK - 1), 0))
        bias = _band_bias()
        bshape = (1, 2 * BLOCK, 6 * BLOCK)
        in_specs += [tok(KV_REP), halo_p, halo_n, tok(KV_REP), halo_p, halo_n, ctx, ctx,
                     pl.BlockSpec(bshape, lambda i, bb: (0, 0, 0)),
                     pl.BlockSpec(bshape, lambda i, bb: (jnp.where(i == 0, 1, 0), 0, 0)),
                     pl.BlockSpec(bshape, lambda i, bb: (jnp.where(i == nt - 1, 2, 0), 0, 0))]
        args += [kk, kk, kk, vv, vv, vv, kkc, vvc, bias, bias, bias]
    else:
        in_specs += [ctx, ctx]
        args += [kkc, vvc]
    in_specs += [_const_spec((3, CONV_WIDTH)), _const_spec((D_MODEL, D_MODEL))]
    args += [conv_w, wm]
    return pl.pallas_call(
        functools.partial(_mixout_kernel, tile_rows=r, local=local),
        grid=(nt, b),
        in_specs=in_specs,
        out_specs=tok(D_MODEL),
        out_shape=jax.ShapeDtypeStruct((b, n, D_MODEL), F32),
        compiler_params=_params(2),
        name="mixout_local" if local else "mixout_ctx",
    )(*args)


def _chan_dft_kernel(x_ref, sh_ref, sc_ref, g_ref, cs_ref, y1_ref, y2_ref):
    h = _norm_mod(x_ref[0], g_ref[...], sh_ref[0], sc_ref[0]).astype(BF16)
    fg = FOURIER_GROUP
    for gi in range(N_FOURIER_GROUPS):
        yc = jnp.dot(h[:, gi * fg:(gi + 1) * fg], cs_ref[...], preferred_element_type=F32)
        y1_ref[0, :, gi * fg:(gi + 1) * fg] = yc[:, :fg].astype(BF16)
        y2_ref[0, :, gi * fg:(gi + 1) * fg] = yc[:, fg:].astype(BF16)


def _chan_dft(x, shift, scale, g, cs, tm):
    b, n, _ = x.shape
    tok = pl.BlockSpec((1, tm, D_MODEL), lambda i, bb: (bb, i, 0))
    mod = pl.BlockSpec((1, 1, D_MODEL), lambda i, bb: (bb, 0, 0))
    return pl.pallas_call(
        _chan_dft_kernel,
        grid=(n // tm, b),
        in_specs=[tok, mod, mod, _const_spec((1, D_MODEL)), _const_spec((FOURIER_GROUP, 2 * FOURIER_GROUP))],
        out_specs=[tok, tok],
        out_shape=[jax.ShapeDtypeStruct((b, n, D_MODEL), BF16)] * 2,
        compiler_params=_params(2),
        name="chan_dft",
    )(x, shift, scale, g, cs)


def _seq_dft_kernel(x_ref, gate_ref, cn_ref, sn_ref, y1_ref, y2_ref, w_ref, o_ref):
    z = jnp.dot(cn_ref[...], y1_ref[0], preferred_element_type=F32)
    z = z + jnp.dot(sn_ref[...], y2_ref[0], preferred_element_type=F32)
    mix = jnp.dot(z.astype(BF16), w_ref[...], preferred_element_type=F32)
    o_ref[0] = x_ref[0] + gate_ref[0] * mix


def _seq_dft(x, gate, cn, nsn, y1, y2, w, tm):
    b, n, _ = x.shape
    tok = pl.BlockSpec((1, tm, D_MODEL), lambda bb, i: (bb, i, 0))
    mod = pl.BlockSpec((1, 1, D_MODEL), lambda bb, i: (bb, 0, 0))
    mat = pl.BlockSpec((tm, n), lambda bb, i: (i, 0))
    full = pl.BlockSpec((1, n, D_MODEL), lambda bb, i: (bb, 0, 0), pipeline_mode=pl.Buffered(1))
    return pl.pallas_call(
        _seq_dft_kernel,
        grid=(b, n // tm),
        in_specs=[tok, mod, mat, mat, full, full, _const_spec((D_MODEL, D_MODEL))],
        out_specs=tok,
        out_shape=jax.ShapeDtypeStruct((b, n, D_MODEL), F32),
        compiler_params=_params(2),
        name="seq_dft",
    )(x, gate, cn, nsn, y1, y2, w)


def _dft_mats(n, scale):
    j = jnp.arange(n, dtype=jnp.int32)
    m = (j[:, None] * j[None, :]) % n
    ang = m.astype(F32) * np.float32(2.0 * np.pi / n)
    return jnp.cos(ang) * scale, jnp.sin(ang) * scale


FF_CHUNK = 256


def _ffn_kernel(*refs, final):
    if final:
        x_ref, sh_ref, sc_ref, gate_ref, g_ref, wg_ref, wu_ref, wd_ref, fg_ref, o_ref, acc_ref = refs
    else:
        x_ref, sh_ref, sc_ref, gate_ref, g_ref, wg_ref, wu_ref, wd_ref, o_ref, acc_ref = refs
    x = x_ref[0]
    h = _norm_mod(x, g_ref[...], sh_ref[0], sc_ref[0]).astype(BF16)
    for c in range(D_FF // FF_CHUNK):
        cs = slice(c * FF_CHUNK, (c + 1) * FF_CHUNK)
        gt = jnp.dot(h, wg_ref[:, cs], preferred_element_type=F32)
        up = jnp.dot(h, wu_ref[:, cs], preferred_element_type=F32)
        a = (_silu(gt) * up).astype(BF16)
        part = jnp.dot(a, wd_ref[cs, :], preferred_element_type=F32)
        if c == 0:
            acc_ref[...] = part
        else:
            acc_ref[...] += part
    y = x + gate_ref[0] * acc_ref[...]
    if final:
        ms = jnp.mean(y * y, axis=-1, keepdims=True)
        y = (y * lax.rsqrt(ms + EPS)) * fg_ref[...]
    o_ref[0] = y


def _ffn(x, shift, scale, gate, g, wg, wu, wd, tm, final_g=None):
    b, n, _ = x.shape
    tok = pl.BlockSpec((1, tm, D_MODEL), lambda i, bb: (bb, i, 0))
    mod = pl.BlockSpec((1, 1, D_MODEL), lambda i, bb: (bb, 0, 0))
    in_specs = [tok, mod, mod, mod, _const_spec((1, D_MODEL)), _const_spec((D_MODEL, D_FF)),
                _const_spec((D_MODEL, D_FF)), _const_spec((D_FF, D_MODEL))]
    args = [x, shift, scale, gate, g, wg, wu, wd]
    if final_g is not None:
        in_specs.append(_const_spec((1, D_MODEL)))
        args.append(final_g)
    return pl.pallas_call(
        functools.partial(_ffn_kernel, final=final_g is not None),
        grid=(n // tm, b),
        in_specs=in_specs,
        out_specs=tok,
        out_shape=jax.ShapeDtypeStruct((b, n, D_MODEL), F32),
        scratch_shapes=[pltpu.VMEM((tm, D_MODEL), F32)],
        compiler_params=_params(2),
        name="ffn",
    )(*args)


def _rope_lane_tables(n):
    rows = n // GRID_W
    r = jnp.repeat(jnp.arange(rows), GRID_W).astype(F32)
    col = jnp.tile(jnp.arange(GRID_W), rows).astype(F32)
    quarter = HEAD_DIM // 4
    inv = ROPE_THETA ** (-jnp.arange(quarter, dtype=F32) / quarter)
    ang_r = r[:, None] * inv
    ang_c = col[:, None] * inv
    cr, sr, cc, sc = jnp.cos(ang_r), jnp.sin(ang_r), jnp.cos(ang_c), jnp.sin(ang_c)
    z = jnp.zeros_like(sr)
    cos = jnp.concatenate([cr, cr, cc, cc], axis=1)
    sa = jnp.concatenate([-sr, z, -sc, z], axis=1)
    sb = jnp.concatenate([z, sr, z, sc], axis=1)
    rep = lambda t: jnp.concatenate([t, t], axis=1)
    return rep(cos), rep(sa), rep(sb)


def _identity_lane_tables(n):
    return jnp.ones((n, LANES), F32), jnp.zeros((n, LANES), F32), jnp.zeros((n, LANES), F32)


def _extend_w_in(w):
    k = w[:, K_START:V_START]
    v = w[:, V_START:]
    dup = lambda t: jnp.concatenate([t[:, :HEAD_DIM], t[:, :HEAD_DIM], t[:, HEAD_DIM:], t[:, HEAD_DIM:]], axis=1)
    return jnp.concatenate([w[:, :K_START], dup(k), dup(v)], axis=1).astype(BF16)


def kernel(x, c, ctx, c_ctx, w_ada, b_ada, norm1_g, norm2_g, w_in, conv_w, sink,
           w_mix_out, w_fourier_out, w_ffn_gate, w_ffn_up, w_ffn_down, final_g):
    b, n, d = x.shape
    L = ctx.shape[1]
    tm = 512

    cond = jnp.concatenate([c, c_ctx[None, :], jnp.zeros((COND_ROWS - b - 1, d), F32)], axis=0)
    mods = _adaln(cond, w_ada, b_ada)

    def split(l, rows):
        m = mods[l, rows]
        return [m[:, None, k * d:(k + 1) * d] for k in range(6)]

    lat_rows = jnp.arange(b)
    ctx_rows = jnp.full((b,), b, jnp.int32)

    rope_lat = _rope_lane_tables(n)
    rope_ctx = _identity_lane_tables(L)
    cc, sc_ = _dft_mats(FOURIER_GROUP, FOURIER_GROUP ** -0.5)
    cs_chan = jnp.concatenate([cc, sc_], axis=1).astype(BF16)
    seq_mats = {}
    for m in {n, L}:
        cm, sm = _dft_mats(m, m ** -0.5)
        seq_mats[m] = (cm.astype(BF16), (-sm).astype(BF16))

    xc = ctx
    for l in range(DEPTH):
        ctx_after = any(j % 2 == 0 for j in range(l + 1, DEPTH))
        ctx_here = (l % 2 == 0) or ctx_after
        sh1, sc1, g1, sh2, sc2, g2 = split(l, lat_rows)
        if ctx_here:
            csh1, csc1, cg1, csh2, csc2, cg2 = split(l, ctx_rows)
        n1 = norm1_g[l][None, :]
        n2 = norm2_g[l][None, :]
        if l % 2 == 0:
            e = l // 2
            w_ext = _extend_w_in(w_in[e])
            wm = w_mix_out[e].astype(BF16)
            sk = sink[e]
            bgu_c, q_c, kk_c, vv_c = _inproj(xc, csh1, csc1, n1, w_ext, *rope_ctx, tm=L)
            bgu, q, kk, vv = _inproj(x, sh1, sc1, n1, w_ext, *rope_lat, tm=tm)
            x = _mixout(x, g1, bgu, q, kk, vv, kk_c, vv_c, sk, conv_w[e], wm, tile_rows=tm, local=True)
            if ctx_after:
                xc = _mixout(xc, cg1, bgu_c, q_c, None, None, kk_c, vv_c, sk, conv_w[e], wm,
                             tile_rows=L, local=False)
        else:
            o = l // 2
            wf = w_fourier_out[o].astype(BF16)
            y1, y2 = _chan_dft(x, sh1, sc1, n1, cs_chan, tm=tm)
            x = _seq_dft(x, g1, *seq_mats[n], y1, y2, wf, tm=256)
            if ctx_after:
                y1c, y2c = _chan_dft(xc, csh1, csc1, n1, cs_chan, tm=L)
                xc = _seq_dft(xc, cg1, *seq_mats[L], y1c, y2c, wf, tm=L)
        wg = w_ffn_gate[l].astype(BF16)
        wu = w_ffn_up[l].astype(BF16)
        wd = w_ffn_down[l].astype(BF16)
        x = _ffn(x, sh2, sc2, g2, n2, wg, wu, wd, tm=tm,
                 final_g=final_g[None, :] if l == DEPTH - 1 else None)
        if ctx_after:
            xc = _ffn(xc, csh2, csc2, cg2, n2, wg, wu, wd, tm=L)
    return x
```

```python
import functools

import numpy as np
import jax
import jax.numpy as jnp
from jax import lax
from jax.experimental import pallas as pl
from jax.experimental.pallas import tpu as pltpu

F32 = jnp.float32
BF16 = jnp.bfloat16

D_MODEL = 1024
DEPTH = 4
GRID_W = 64
HEAD_DIM = 64
N_Q_HEADS = 8
N_KV_HEADS = 2
ATTN_WIDTH = N_Q_HEADS * HEAD_DIM
KV_WIDTH = N_KV_HEADS * HEAD_DIM
CONV_WIDTH = D_MODEL - ATTN_WIDTH
WINDOW = 128
BLOCK = 128
ROPE_THETA = 10000.0
N_FOURIER_GROUPS = 4
FOURIER_GROUP = D_MODEL // N_FOURIER_GROUPS
D_FF = 2816
EPS = 1e-6
Q_START = 3 * CONV_WIDTH
K_START = Q_START + ATTN_WIDTH
V_START = K_START + KV_WIDTH

LANES = 128
SUBLANES = 8
KV_REP = 2 * KV_WIDTH
NEG_BIG = -1e30
COND_ROWS = 24
VMEM_LIMIT = 56 * 1024 * 1024


def _params(n_axes):
    return pltpu.CompilerParams(dimension_semantics=("arbitrary",) * n_axes,
                                vmem_limit_bytes=VMEM_LIMIT)


def _const_spec(shape):
    nd = len(shape)
    return pl.BlockSpec(shape, lambda *_: (0,) * nd, pipeline_mode=pl.Buffered(1))


def _norm_mod(x, g, shift, scale):
    ms = jnp.mean(x * x, axis=-1, keepdims=True)
    y = x * lax.rsqrt(ms + EPS)
    return (y * g) * (1.0 + scale) + shift


def _silu(x):
    return x * (1.0 / (1.0 + jnp.exp(-x)))


def _adaln_kernel(c_ref, w_ref, b_ref, o_ref):
    a = _silu(c_ref[...])
    w = w_ref[0]
    a_hi = a.astype(BF16)
    a_lo = (a - a_hi.astype(F32)).astype(BF16)
    w_hi = w.astype(BF16)
    w_lo = (w - w_hi.astype(F32)).astype(BF16)
    acc = jnp.dot(a_hi, w_hi, preferred_element_type=F32)
    acc += jnp.dot(a_lo, w_hi, preferred_element_type=F32)
    acc += jnp.dot(a_hi, w_lo, preferred_element_type=F32)
    o_ref[0] = acc + b_ref[0]


def _adaln(cond, w_ada, b_ada):
    tn = 1024
    nt = (6 * D_MODEL) // tn
    return pl.pallas_call(
        _adaln_kernel,
        grid=(DEPTH, nt),
        in_specs=[
            pl.BlockSpec((COND_ROWS, D_MODEL), lambda l, j: (0, 0)),
            pl.BlockSpec((1, D_MODEL, tn), lambda l, j: (l, 0, j)),
            pl.BlockSpec((1, 1, tn), lambda l, j: (l, 0, j)),
        ],
        out_specs=pl.BlockSpec((1, COND_ROWS, tn), lambda l, j: (l, 0, j)),
        out_shape=jax.ShapeDtypeStruct((DEPTH, COND_ROWS, 6 * D_MODEL), F32),
        compiler_params=_params(2),
        name="adaln",
    )(cond, w_ada, b_ada.reshape(DEPTH, 1, 6 * D_MODEL))


IN_EXT = Q_START + ATTN_WIDTH + 2 * KV_REP


def _inproj_kernel(x_ref, sh_ref, sc_ref, g_ref, w_ref, cos_ref, sa_ref, sb_ref,
                   bgu_ref, q_ref, kk_ref, vv_ref):
    h = _norm_mod(x_ref[0], g_ref[...], sh_ref[0], sc_ref[0]).astype(BF16)
    p = jnp.dot(h, w_ref[...], preferred_element_type=F32)
    bgu_ref[0, :, :CONV_WIDTH] = p[:, :CONV_WIDTH].astype(BF16)
    bgu_ref[0, :, CONV_WIDTH:] = (p[:, CONV_WIDTH:2 * CONV_WIDTH]
                                  * p[:, 2 * CONV_WIDTH:Q_START]).astype(BF16)
    cos = cos_ref[...]
    sa = sa_ref[...]
    sb = sb_ref[...]

    def rope(t):
        return t * cos + pltpu.roll(t, LANES - 16, 1) * sa + pltpu.roll(t, 16, 1) * sb

    for j in range(ATTN_WIDTH // LANES):
        lo = Q_START + j * LANES
        q_ref[0, :, j * LANES:(j + 1) * LANES] = (
            rope(p[:, lo:lo + LANES]) * (HEAD_DIM ** -0.5)).astype(BF16)
    k0 = Q_START + ATTN_WIDTH
    for j in range(KV_REP // LANES):
        lo = k0 + j * LANES
        kk_ref[0, :, j * LANES:(j + 1) * LANES] = rope(p[:, lo:lo + LANES]).astype(BF16)
    vv_ref[0] = p[:, k0 + KV_REP:].astype(BF16)


def _inproj(x, shift, scale, g, w_ext, cos, sa, sb, tm):
    b, n, _ = x.shape
    nt = n // tm
    tok = lambda w: pl.BlockSpec((1, tm, w), lambda i, bb: (bb, i, 0))
    mod = pl.BlockSpec((1, 1, D_MODEL), lambda i, bb: (bb, 0, 0))
    tab = pl.BlockSpec((tm, LANES), lambda i, bb: (i, 0))
    return pl.pallas_call(
        _inproj_kernel,
        grid=(nt, b),
        in_specs=[tok(D_MODEL), mod, mod, _const_spec((1, D_MODEL)),
                  _const_spec((D_MODEL, IN_EXT)), tab, tab, tab],
        out_specs=[tok(D_MODEL), tok(ATTN_WIDTH), tok(KV_REP), tok(KV_REP)],
        out_shape=[jax.ShapeDtypeStruct((b, n, D_MODEL), BF16),
                   jax.ShapeDtypeStruct((b, n, ATTN_WIDTH), BF16),
                   jax.ShapeDtypeStruct((b, n, KV_REP), BF16),
                   jax.ShapeDtypeStruct((b, n, KV_REP), BF16)],
        compiler_params=_params(2),
        name="inproj",
    )(x, shift, scale, g, w_ext, cos, sa, sb)


def _softmax_parts(parts, sink_col):
    m = sink_col
    for s in parts:
        m = jnp.maximum(m, jnp.max(s, axis=-1, keepdims=True))
    es = [jnp.exp(s - m) for s in parts]
    den = jnp.exp(sink_col - m)
    for e in es:
        den = den + jnp.sum(e, axis=-1, keepdims=True)
    return es, 1.0 / den


def _mixout_kernel(*refs, tile_rows, local):
    if local:
        (sink_ref, x_ref, gate_ref, bgu_ref, bgu_p_ref, bgu_n_ref, q_ref,
         kk_ref, kk_p_ref, kk_n_ref, vv_ref, vv_p_ref, vv_n_ref, kkc_ref, vvc_ref,
         bias_mid_ref, bias_a_ref, bias_b_ref, cw_ref, wm_ref, o_ref) = refs
    else:
        (sink_ref, x_ref, gate_ref, bgu_ref, bgu_p_ref, bgu_n_ref, q_ref,
         kkc_ref, vvc_ref, cw_ref, wm_ref, o_ref) = refs
    i = pl.program_id(0)
    n_tiles = pl.num_programs(0)
    r = tile_rows
    nb = r // BLOCK

    bg = bgu_ref[0, :, :CONV_WIDTH].astype(F32)
    u = bgu_ref[0, :, CONV_WIDTH:].astype(F32)
    has_prev = (i > 0).astype(F32)
    has_next = (i < n_tiles - 1).astype(F32)
    u_prev = bgu_p_ref[0, SUBLANES - 1:SUBLANES, CONV_WIDTH:].astype(F32) * has_prev
    u_next = bgu_n_ref[0, 0:1, CONV_WIDTH:].astype(F32) * has_next
    row = lax.broadcasted_iota(jnp.int32, (r, CONV_WIDTH), 0)
    u_up = jnp.where(row == 0, u_prev, pltpu.roll(u, 1, 0))
    u_dn = jnp.where(row == r - 1, u_next, pltpu.roll(u, r - 1, 0))
    cw = cw_ref[...]
    a_out = (bg * (u_up * cw[0:1] + u * cw[1:2] + u_dn * cw[2:3])).astype(BF16)

    lane = lax.broadcasted_iota(jnp.int32, (1, LANES), 1)
    first_half = lane < HEAD_DIM
    rows2 = lax.broadcasted_iota(jnp.int32, (2 * BLOCK, 1), 0)
    top = rows2 < BLOCK
    zero = jnp.zeros((), BF16)

    def halves(t):
        return jnp.concatenate([jnp.where(first_half, t, zero), jnp.where(first_half, zero, t)], axis=0)

    b_cols = []
    for h in range(N_KV_HEADS):
        hs = slice(h * LANES, (h + 1) * LANES)
        kc_bd = halves(kkc_ref[0, :, hs])
        vc_bd = halves(vvc_ref[0, :, hs])
        n_ctx = kkc_ref.shape[1]
        if local:
            k_all = jnp.concatenate([kk_p_ref[0, :, hs], kk_ref[0, :, hs], kk_n_ref[0, :, hs]], axis=0)
            v_all = jnp.concatenate([vv_p_ref[0, :, hs], vv_ref[0, :, hs], vv_n_ref[0, :, hs]], axis=0)
        sink_a = jnp.where(top, sink_ref[4 * h + 0], sink_ref[4 * h + 2])
        sink_b = jnp.where(top, sink_ref[4 * h + 1], sink_ref[4 * h + 3])
        outs = []
        for j in range(nb):
            rs = slice(j * BLOCK, (j + 1) * BLOCK)
            q2 = jnp.concatenate([q_ref[0, rs, 2 * h * LANES:(2 * h + 1) * LANES],
                                  q_ref[0, rs, (2 * h + 1) * LANES:(2 * h + 2) * LANES]], axis=0)
            s_ctx = lax.dot_general(q2, kc_bd, (((1,), (1,)), ((), ())), preferred_element_type=F32)
            parts_a = [s_ctx[:, :n_ctx]]
            parts_b = [s_ctx[:, n_ctx:]]
            if local:
                nl = 3 * BLOCK
                kl_bd = halves(k_all[j * BLOCK:j * BLOCK + nl])
                vl_bd = halves(v_all[j * BLOCK:j * BLOCK + nl])
                s_loc = lax.dot_general(q2, kl_bd, (((1,), (1,)), ((), ())), preferred_element_type=F32)
                if nb == 1:
                    bias = bias_a_ref[0] + bias_b_ref[0]
                elif j == 0:
                    bias = bias_a_ref[0]
                elif j == nb - 1:
                    bias = bias_b_ref[0]
                else:
                    bias = bias_mid_ref[0]
                s_loc = s_loc + bias
                parts_a.append(s_loc[:, :nl])
                parts_b.append(s_loc[:, nl:])
            es_a, inv_a = _softmax_parts(parts_a, sink_a)
            es_b, inv_b = _softmax_parts(parts_b, sink_b)
            e_ctx = jnp.concatenate([es_a[0], es_b[0]], axis=1).astype(BF16)
            o = jnp.dot(e_ctx, vc_bd, preferred_element_type=F32)
            if local:
                e_loc = jnp.concatenate([es_a[1], es_b[1]], axis=1).astype(BF16)
                o = o + jnp.dot(e_loc, vl_bd, preferred_element_type=F32)
            o = o * jnp.where(first_half, inv_a, inv_b)
            outs.append(jnp.concatenate([o[:BLOCK], o[BLOCK:]], axis=1))
        b_cols.append(jnp.concatenate(outs, axis=0) if nb > 1 else outs[0])
    b_out = jnp.concatenate(b_cols, axis=1).astype(BF16)

    mix = jnp.dot(a_out, wm_ref[:CONV_WIDTH, :], preferred_element_type=F32)
    mix = mix + jnp.dot(b_out, wm_ref[CONV_WIDTH:, :], preferred_element_type=F32)
    o_ref[0] = x_ref[0] + gate_ref[0] * mix


def _band_bias():
    rr = np.arange(2 * BLOCK)[:, None] % BLOCK
    kidx = np.arange(6 * BLOCK)[None, :] % (3 * BLOCK)
    band = (kidx >= rr) & (kidx <= rr + 2 * WINDOW)
    mid = band
    first = band & (kidx >= BLOCK)
    last = band & (kidx < 2 * BLOCK)
    return jnp.asarray(np.where(np.stack([mid, first, last]), 0.0, NEG_BIG), dtype=F32)


def _mixout(x, gate, bgu, q, kk, vv, kkc, vvc, sink, conv_w, wm, tile_rows, local):
    b, n, _ = x.shape
    r = tile_rows
    nt = n // r
    nctx = kkc.shape[1]
    tok = lambda w: pl.BlockSpec((1, r, w), lambda i, bb: (bb, i, 0))
    mod = pl.BlockSpec((1, 1, D_MODEL), lambda i, bb: (bb, 0, 0))
    r8 = r // SUBLANES
    halo8_p = pl.BlockSpec((1, SUBLANES, D_MODEL), lambda i, bb: (bb, jnp.maximum(i * r8 - 1, 0), 0))
    halo8_n = pl.BlockSpec((1, SUBLANES, D_MODEL),
                           lambda i, bb: (bb, jnp.minimum((i + 1) * r8, n // SUBLANES - 1), 0))
    ctx = pl.BlockSpec((1, nctx, KV_REP), lambda i, bb: (bb, 0, 0))
    smem = pl.BlockSpec(memory_space=pltpu.SMEM)
    in_specs = [smem, tok(D_MODEL), mod, tok(D_MODEL), halo8_p, halo8_n, tok(ATTN_WIDTH)]
    args = [sink, x, gate, bgu, bgu, bgu, q]
    if local:
        rb = r // BLOCK
        halo_p = pl.BlockSpec((1, BLOCK, KV_REP), lambda i, bb: (bb, jnp.maximum(i * rb - 1, 0), 0))
        halo_n = pl.BlockSpec((1, BLOCK, KV_REP),
                              lambda i, bb: (bb, jnp.minimum((i + 1) * rb, n // BLOCK - 1), 0))
        bias = _band_bias()
        bshape = (1, 2 * BLOCK, 6 * BLOCK)
        in_specs += [tok(KV_REP), halo_p, halo_n, tok(KV_REP), halo_p, halo_n, ctx, ctx,
                     pl.BlockSpec(bshape, lambda i, bb: (0, 0, 0)),
                     pl.BlockSpec(bshape, lambda i, bb: (jnp.where(i == 0, 1, 0), 0, 0)),
                     pl.BlockSpec(bshape, lambda i, bb: (jnp.where(i == nt - 1, 2, 0), 0, 0))]
        args += [kk, kk, kk, vv, vv, vv, kkc, vvc, bias, bias, bias]
    else:
        in_specs += [ctx, ctx]
        args += [kkc, vvc]
    in_specs += [_const_spec((3, CONV_WIDTH)), _const_spec((D_MODEL, D_MODEL))]
    args += [conv_w, wm]
    return pl.pallas_call(
        functools.partial(_mixout_kernel, tile_rows=r, local=local),
        grid=(nt, b),
        in_specs=in_specs,
        out_specs=tok(D_MODEL),
        out_shape=jax.ShapeDtypeStruct((b, n, D_MODEL), F32),
        compiler_params=_params(2),
        name="mixout_local" if local else "mixout_ctx",
    )(*args)


def _chan_dft_kernel(x_ref, sh_ref, sc_ref, g_ref, cs_ref, y1_ref, y2_ref):
    h = _norm_mod(x_ref[0], g_ref[...], sh_ref[0], sc_ref[0]).astype(BF16)
    fg = FOURIER_GROUP
    for gi in range(N_FOURIER_GROUPS):
        yc = jnp.dot(h[:, gi * fg:(gi + 1) * fg], cs_ref[...], preferred_element_type=F32)
        y1_ref[0, :, gi * fg:(gi + 1) * fg] = yc[:, :fg].astype(BF16)
        y2_ref[0, :, gi * fg:(gi + 1) * fg] = yc[:, fg:].astype(BF16)


def _chan_dft(x, shift, scale, g, cs, tm):
    b, n, _ = x.shape
    tok = pl.BlockSpec((1, tm, D_MODEL), lambda i, bb: (bb, i, 0))
    mod = pl.BlockSpec((1, 1, D_MODEL), lambda i, bb: (bb, 0, 0))
    return pl.pallas_call(
        _chan_dft_kernel,
        grid=(n // tm, b),
        in_specs=[tok, mod, mod, _const_spec((1, D_MODEL)), _const_spec((FOURIER_GROUP, 2 * FOURIER_GROUP))],
        out_specs=[tok, tok],
        out_shape=[jax.ShapeDtypeStruct((b, n, D_MODEL), BF16)] * 2,
        compiler_params=_params(2),
        name="chan_dft",
    )(x, shift, scale, g, cs)


def _seq_dft_kernel(x_ref, gate_ref, cn_ref, sn_ref, y1_ref, y2_ref, w_ref, o_ref):
    z = jnp.dot(cn_ref[...], y1_ref[0], preferred_element_type=F32)
    z = z + jnp.dot(sn_ref[...], y2_ref[0], preferred_element_type=F32)
    mix = jnp.dot(z.astype(BF16), w_ref[...], preferred_element_type=F32)
    o_ref[0] = x_ref[0] + gate_ref[0] * mix


def _seq_dft(x, gate, cn, nsn, y1, y2, w, tm):
    b, n, _ = x.shape
    tok = pl.BlockSpec((1, tm, D_MODEL), lambda bb, i: (bb, i, 0))
    mod = pl.BlockSpec((1, 1, D_MODEL), lambda bb, i: (bb, 0, 0))
    mat = pl.BlockSpec((tm, n), lambda bb, i: (i, 0))
    full = pl.BlockSpec((1, n, D_MODEL), lambda bb, i: (bb, 0, 0), pipeline_mode=pl.Buffered(1))
    return pl.pallas_call(
        _seq_dft_kernel,
        grid=(b, n // tm),
        in_specs=[tok, mod, mat, mat, full, full, _const_spec((D_MODEL, D_MODEL))],
        out_specs=tok,
        out_shape=jax.ShapeDtypeStruct((b, n, D_MODEL), F32),
        compiler_params=_params(2),
        name="seq_dft",
    )(x, gate, cn, nsn, y1, y2, w)


def _dft_mats(n, scale):
    j = jnp.arange(n, dtype=jnp.int32)
    m = (j[:, None] * j[None, :]) % n
    ang = m.astype(F32) * np.float32(2.0 * np.pi / n)
    return jnp.cos(ang) * scale, jnp.sin(ang) * scale


FFT_RADIX = 16
FFT_INNER = 256
FFT_N = FFT_RADIX * FFT_INNER
A_PER_STEP = 4
MIX_TILE = 512


def _chan_dft_perm_kernel(x_ref, sh_ref, sc_ref, g_ref, cs_ref, y_ref):
    fg = FOURIER_GROUP
    for k in range(A_PER_STEP):
        h = _norm_mod(x_ref[0, :, k * D_MODEL:(k + 1) * D_MODEL], g_ref[...], sh_ref[0], sc_ref[0]).astype(BF16)
        for gi in range(N_FOURIER_GROUPS):
            yc = jnp.dot(h[:, gi * fg:(gi + 1) * fg], cs_ref[...], preferred_element_type=F32)
            y_ref[0, k, :FFT_INNER, gi * fg:(gi + 1) * fg] = yc[:, :fg].astype(BF16)
            y_ref[0, k, FFT_INNER:, gi * fg:(gi + 1) * fg] = yc[:, fg:].astype(BF16)


def _chan_dft_perm(x, shift, scale, g, cs):
    b, n, _ = x.shape
    xv = x.reshape(b, FFT_INNER, FFT_RADIX * D_MODEL)
    mod = pl.BlockSpec((1, 1, D_MODEL), lambda j, bb: (bb, 0, 0))
    return pl.pallas_call(
        _chan_dft_perm_kernel,
        grid=(FFT_RADIX // A_PER_STEP, b),
        in_specs=[pl.BlockSpec((1, FFT_INNER, A_PER_STEP * D_MODEL), lambda j, bb: (bb, 0, j)),
                  mod, mod, _const_spec((1, D_MODEL)), _const_spec((FOURIER_GROUP, 2 * FOURIER_GROUP))],
        out_specs=pl.BlockSpec((1, A_PER_STEP, 2 * FFT_INNER, D_MODEL), lambda j, bb: (bb, j, 0, 0)),
        out_shape=jax.ShapeDtypeStruct((b, FFT_RADIX, 2 * FFT_INNER, D_MODEL), BF16),
        compiler_params=_params(2),
        name="chan_dft_perm",
    )(xv, shift, scale, g, cs)


_C1 = float(np.cos(np.pi / 8))
_C2 = float(np.cos(np.pi / 4))
_C3 = float(np.cos(3 * np.pi / 8))


def _dft16_real(ur, ui):
    p = [ur[0]] + [ur[a] + ur[16 - a] for a in range(1, 8)] + [ur[8]]
    q = [p[a] + p[8 - a] for a in range(4)] + [p[4]]
    r = [p[a] - p[8 - a] for a in range(4)]
    s0, s1, s2 = q[0] + q[4], q[1] + q[3], q[2]
    t0, t1 = q[0] - q[4], q[1] - q[3]
    ca = [None] * 9
    ca[0] = s0 + s1 + s2
    ca[4] = s0 - s2
    ca[8] = s0 - s1 + s2
    ca[2] = t0 + _C2 * t1
    ca[6] = t0 - _C2 * t1
    e0, e1 = r[0] + _C2 * r[2], r[0] - _C2 * r[2]
    f, g = _C1 * r[1] + _C3 * r[3], _C3 * r[1] - _C1 * r[3]
    ca[1], ca[7], ca[3], ca[5] = e0 + f, e0 - f, e1 + g, e1 - g
    pp = [None] + [ui[a] - ui[16 - a] for a in range(1, 8)]
    qq = [None] + [pp[a] + pp[8 - a] for a in range(1, 4)] + [pp[4]]
    rr = [None] + [pp[a] - pp[8 - a] for a in range(1, 4)]
    sb = [None] * 8
    e0, e1 = _C2 * qq[2] + qq[4], _C2 * qq[2] - qq[4]
    f, g = _C3 * qq[1] + _C1 * qq[3], _C1 * qq[1] - _C3 * qq[3]
    sb[1], sb[7], sb[3], sb[5] = f + e0, f - e0, g + e1, g - e1
    w = _C2 * (rr[1] + rr[3])
    sb[2], sb[6], sb[4] = w + rr[2], w - rr[2], rr[1] - rr[3]
    out = [None] * 16
    out[0], out[8] = ca[0], ca[8]
    for c in range(1, 8):
        out[c] = ca[c] + sb[c]
        out[16 - c] = ca[c] - sb[c]
    return out


def _seq_fft_kernel(x_ref, gate_ref, y_ref, l_ref, w_ref, o_ref, u_ref, f_ref):
    s = pl.program_id(1)
    ng = N_FOURIER_GROUPS
    fg = FOURIER_GROUP

    @pl.when(s < ng)
    def _():
        for a in range(FFT_RADIX):
            u_ref[a] = jnp.dot(l_ref[a], y_ref[0, a], preferred_element_type=F32)

        rt = 2 * SUBLANES

        def body(i, carry):
            r0 = pl.multiple_of(i * rt, rt)
            for lh in range(fg // LANES):
                ls = slice(lh * LANES, (lh + 1) * LANES)
                ur = [u_ref[a, pl.ds(r0, rt), ls] for a in range(FFT_RADIX)]
                ui = [u_ref[a, pl.ds(FFT_INNER + r0, rt), ls] for a in range(FFT_RADIX)]
                out = _dft16_real(ur, ui)
                for c in range(FFT_RADIX):
                    f_ref[s, pl.ds(c * FFT_INNER + r0, rt), ls] = out[c].astype(BF16)
            return carry

        lax.fori_loop(0, FFT_INNER // rt, body, 0)

    @pl.when(s >= ng)
    def _():
        r0 = pl.multiple_of((s - ng) * MIX_TILE, MIX_TILE)
        mix = jnp.dot(f_ref[0, pl.ds(r0, MIX_TILE), :], w_ref[:fg, :], preferred_element_type=F32)
        for gi in range(1, ng):
            mix = mix + jnp.dot(f_ref[gi, pl.ds(r0, MIX_TILE), :], w_ref[gi * fg:(gi + 1) * fg, :],
                                preferred_element_type=F32)
        o_ref[0] = x_ref[0] + gate_ref[0] * mix


def _seq_fft(x, gate, ycat, lmat, w):
    b, n, _ = x.shape
    ng = N_FOURIER_GROUPS
    nt = n // MIX_TILE
    tok = pl.BlockSpec((1, MIX_TILE, D_MODEL), lambda bb, s: (bb, jnp.maximum(s - ng, 0), 0))
    mod = pl.BlockSpec((1, 1, D_MODEL), lambda bb, s: (bb, 0, 0))
    ysp = pl.BlockSpec((1, FFT_RADIX, 2 * FFT_INNER, FOURIER_GROUP),
                       lambda bb, s: (bb, 0, 0, jnp.minimum(s, ng - 1)))
    return pl.pallas_call(
        _seq_fft_kernel,
        grid=(b, ng + nt),
        in_specs=[tok, mod, ysp, _const_spec((FFT_RADIX, 2 * FFT_INNER, 2 * FFT_INNER)),
                  _const_spec((D_MODEL, D_MODEL))],
        out_specs=tok,
        out_shape=jax.ShapeDtypeStruct((b, n, D_MODEL), F32),
        scratch_shapes=[pltpu.VMEM((FFT_RADIX, 2 * FFT_INNER, FOURIER_GROUP), F32),
                        pltpu.VMEM((ng, n, FOURIER_GROUP), BF16)],
        compiler_params=_params(2),
        name="seq_fft",
    )(x, gate, ycat, lmat, w)


def _fft_stage_mats():
    a = jnp.arange(FFT_RADIX, dtype=jnp.int32)[:, None, None]
    d = jnp.arange(FFT_INNER, dtype=jnp.int32)[None, :, None]
    bb = jnp.arange(FFT_INNER, dtype=jnp.int32)[None, None, :]
    m = (d * (a + FFT_RADIX * bb)) % FFT_N
    ang = m.astype(F32) * np.float32(2.0 * np.pi / FFT_N)
    c = jnp.cos(ang) * (FFT_N ** -0.5)
    s = jnp.sin(ang) * (FFT_N ** -0.5)
    top = jnp.concatenate([c, -s], axis=2)
    bot = jnp.concatenate([-s, -c], axis=2)
    return jnp.concatenate([top, bot], axis=1).astype(BF16)


FF_CHUNK = 256


def _ffn_kernel(*refs, final):
    if final:
        x_ref, sh_ref, sc_ref, gate_ref, g_ref, wg_ref, wu_ref, wd_ref, fg_ref, o_ref, acc_ref = refs
    else:
        x_ref, sh_ref, sc_ref, gate_ref, g_ref, wg_ref, wu_ref, wd_ref, o_ref, acc_ref = refs
    x = x_ref[0]
    h = _norm_mod(x, g_ref[...], sh_ref[0], sc_ref[0]).astype(BF16)
    for c in range(D_FF // FF_CHUNK):
        cs = slice(c * FF_CHUNK, (c + 1) * FF_CHUNK)
        gt = jnp.dot(h, wg_ref[:, cs], preferred_element_type=F32)
        up = jnp.dot(h, wu_ref[:, cs], preferred_element_type=F32)
        a = (_silu(gt) * up).astype(BF16)
        part = jnp.dot(a, wd_ref[cs, :], preferred_element_type=F32)
        if c == 0:
            acc_ref[...] = part
        else:
            acc_ref[...] += part
    y = x + gate_ref[0] * acc_ref[...]
    if final:
        ms = jnp.mean(y * y, axis=-1, keepdims=True)
        y = (y * lax.rsqrt(ms + EPS)) * fg_ref[...]
    o_ref[0] = y


def _ffn(x, shift, scale, gate, g, wg, wu, wd, tm, final_g=None):
    b, n, _ = x.shape
    tok = pl.BlockSpec((1, tm, D_MODEL), lambda i, bb: (bb, i, 0))
    mod = pl.BlockSpec((1, 1, D_MODEL), lambda i, bb: (bb, 0, 0))
    in_specs = [tok, mod, mod, mod, _const_spec((1, D_MODEL)), _const_spec((D_MODEL, D_FF)),
                _const_spec((D_MODEL, D_FF)), _const_spec((D_FF, D_MODEL))]
    args = [x, shift, scale, gate, g, wg, wu, wd]
    if final_g is not None:
        in_specs.append(_const_spec((1, D_MODEL)))
        args.append(final_g)
    return pl.pallas_call(
        functools.partial(_ffn_kernel, final=final_g is not None),
        grid=(n // tm, b),
        in_specs=in_specs,
        out_specs=tok,
        out_shape=jax.ShapeDtypeStruct((b, n, D_MODEL), F32),
        scratch_shapes=[pltpu.VMEM((tm, D_MODEL), F32)],
        compiler_params=_params(2),
        name="ffn",
    )(*args)


def _rope_lane_tables(n):
    rows = n // GRID_W
    r = jnp.repeat(jnp.arange(rows), GRID_W).astype(F32)
    col = jnp.tile(jnp.arange(GRID_W), rows).astype(F32)
    quarter = HEAD_DIM // 4
    inv = ROPE_THETA ** (-jnp.arange(quarter, dtype=F32) / quarter)
    ang_r = r[:, None] * inv
    ang_c = col[:, None] * inv
    cr, sr, cc, sc = jnp.cos(ang_r), jnp.sin(ang_r), jnp.cos(ang_c), jnp.sin(ang_c)
    z = jnp.zeros_like(sr)
    cos = jnp.concatenate([cr, cr, cc, cc], axis=1)
    sa = jnp.concatenate([-sr, z, -sc, z], axis=1)
    sb = jnp.concatenate([z, sr, z, sc], axis=1)
    rep = lambda t: jnp.concatenate([t, t], axis=1)
    return rep(cos), rep(sa), rep(sb)


def _identity_lane_tables(n):
    return jnp.ones((n, LANES), F32), jnp.zeros((n, LANES), F32), jnp.zeros((n, LANES), F32)


def _extend_w_in(w):
    k = w[:, K_START:V_START]
    v = w[:, V_START:]
    dup = lambda t: jnp.concatenate([t[:, :HEAD_DIM], t[:, :HEAD_DIM], t[:, HEAD_DIM:], t[:, HEAD_DIM:]], axis=1)
    return jnp.concatenate([w[:, :K_START], dup(k), dup(v)], axis=1).astype(BF16)


def kernel(x, c, ctx, c_ctx, w_ada, b_ada, norm1_g, norm2_g, w_in, conv_w, sink,
           w_mix_out, w_fourier_out, w_ffn_gate, w_ffn_up, w_ffn_down, final_g):
    b, n, d = x.shape
    L = ctx.shape[1]
    tm = 512

    cond = jnp.concatenate([c, c_ctx[None, :], jnp.zeros((COND_ROWS - b - 1, d), F32)], axis=0)
    mods = _adaln(cond, w_ada, b_ada)

    def split(l, rows):
        m = mods[l, rows]
        return [m[:, None, k * d:(k + 1) * d] for k in range(6)]

    lat_rows = jnp.arange(b)
    ctx_rows = jnp.full((b,), b, jnp.int32)

    rope_lat = _rope_lane_tables(n)
    rope_ctx = _identity_lane_tables(L)
    cc, sc_ = _dft_mats(FOURIER_GROUP, FOURIER_GROUP ** -0.5)
    cs_chan = jnp.concatenate([cc, sc_], axis=1).astype(BF16)
    seq_mats = {}
    for m in {n, L} - {FFT_N}:
        cm, sm = _dft_mats(m, m ** -0.5)
        seq_mats[m] = (cm.astype(BF16), (-sm).astype(BF16))
    fft_l = _fft_stage_mats() if n == FFT_N else None

    xc = ctx
    for l in range(DEPTH):
        ctx_after = any(j % 2 == 0 for j in range(l + 1, DEPTH))
        ctx_here = (l % 2 == 0) or ctx_after
        sh1, sc1, g1, sh2, sc2, g2 = split(l, lat_rows)
        if ctx_here:
            csh1, csc1, cg1, csh2, csc2, cg2 = split(l, ctx_rows)
        n1 = norm1_g[l][None, :]
        n2 = norm2_g[l][None, :]
        if l % 2 == 0:
            e = l // 2
            w_ext = _extend_w_in(w_in[e])
            wm = w_mix_out[e].astype(BF16)
            sk = sink[e]
            bgu_c, q_c, kk_c, vv_c = _inproj(xc, csh1, csc1, n1, w_ext, *rope_ctx, tm=L)
            bgu, q, kk, vv = _inproj(x, sh1, sc1, n1, w_ext, *rope_lat, tm=tm)
            x = _mixout(x, g1, bgu, q, kk, vv, kk_c, vv_c, sk, conv_w[e], wm, tile_rows=tm, local=True)
            if ctx_after:
                xc = _mixout(xc, cg1, bgu_c, q_c, None, None, kk_c, vv_c, sk, conv_w[e], wm,
                             tile_rows=L, local=False)
        else:
            o = l // 2
            wf = w_fourier_out[o].astype(BF16)
            if n == FFT_N:
                ycat = _chan_dft_perm(x, sh1, sc1, n1, cs_chan)
                x = _seq_fft(x, g1, ycat, fft_l, wf)
            else:
                y1, y2 = _chan_dft(x, sh1, sc1, n1, cs_chan, tm=tm)
                x = _seq_dft(x, g1, *seq_mats[n], y1, y2, wf, tm=256)
            if ctx_after:
                y1c, y2c = _chan_dft(xc, csh1, csc1, n1, cs_chan, tm=L)
                xc = _seq_dft(xc, cg1, *seq_mats[L], y1c, y2c, wf, tm=L)
        wg = w_ffn_gate[l].astype(BF16)
        wu = w_ffn_up[l].astype(BF16)
        wd = w_ffn_down[l].astype(BF16)
        x = _ffn(x, sh2, sc2, g2, n2, wg, wu, wd, tm=tm,
                 final_g=final_g[None, :] if l == DEPTH - 1 else None)
        if ctx_after:
            xc = _ffn(xc, csh2, csc2, cg2, n2, wg, wu, wd, tm=L)
    return x
```

```python
import functools

import numpy as np
import jax
import jax.numpy as jnp
from jax import lax
from jax.experimental import pallas as pl
from jax.experimental.pallas import tpu as pltpu

F32 = jnp.float32
BF16 = jnp.bfloat16

D_MODEL = 1024
DEPTH = 4
GRID_W = 64
HEAD_DIM = 64
N_Q_HEADS = 8
N_KV_HEADS = 2
ATTN_WIDTH = N_Q_HEADS * HEAD_DIM
KV_WIDTH = N_KV_HEADS * HEAD_DIM
CONV_WIDTH = D_MODEL - ATTN_WIDTH
WINDOW = 128
BLOCK = 128
ROPE_THETA = 10000.0
N_FOURIER_GROUPS = 4
FOURIER_GROUP = D_MODEL // N_FOURIER_GROUPS
D_FF = 2816
EPS = 1e-6
Q_START = 3 * CONV_WIDTH
K_START = Q_START + ATTN_WIDTH
V_START = K_START + KV_WIDTH

LANES = 128
SUBLANES = 8
KV_REP = 2 * KV_WIDTH
NEG_BIG = -1e30
COND_ROWS = 24
VMEM_LIMIT = 56 * 1024 * 1024


def _params(n_axes):
    return pltpu.CompilerParams(dimension_semantics=("arbitrary",) * n_axes,
                                vmem_limit_bytes=VMEM_LIMIT)


def _const_spec(shape):
    nd = len(shape)
    return pl.BlockSpec(shape, lambda *_: (0,) * nd, pipeline_mode=pl.Buffered(1))


def _norm_mod(x, g, shift, scale):
    ms = jnp.mean(x * x, axis=-1, keepdims=True)
    y = x * lax.rsqrt(ms + EPS)
    return (y * g) * (1.0 + scale) + shift


def _silu(x):
    return x * (1.0 / (1.0 + jnp.exp(-x)))


def _adaln_kernel(c_ref, w_ref, b_ref, o_ref):
    a = _silu(c_ref[...])
    w = w_ref[0]
    a_hi = a.astype(BF16)
    a_lo = (a - a_hi.astype(F32)).astype(BF16)
    w_hi = w.astype(BF16)
    w_lo = (w - w_hi.astype(F32)).astype(BF16)
    acc = jnp.dot(a_hi, w_hi, preferred_element_type=F32)
    acc += jnp.dot(a_lo, w_hi, preferred_element_type=F32)
    acc += jnp.dot(a_hi, w_lo, preferred_element_type=F32)
    o_ref[0] = acc + b_ref[0]


def _adaln(cond, w_ada, b_ada):
    tn = 1024
    nt = (6 * D_MODEL) // tn
    return pl.pallas_call(
        _adaln_kernel,
        grid=(DEPTH, nt),
        in_specs=[
            pl.BlockSpec((COND_ROWS, D_MODEL), lambda l, j: (0, 0)),
            pl.BlockSpec((1, D_MODEL, tn), lambda l, j: (l, 0, j)),
            pl.BlockSpec((1, 1, tn), lambda l, j: (l, 0, j)),
        ],
        out_specs=pl.BlockSpec((1, COND_ROWS, tn), lambda l, j: (l, 0, j)),
        out_shape=jax.ShapeDtypeStruct((DEPTH, COND_ROWS, 6 * D_MODEL), F32),
        compiler_params=_params(2),
        name="adaln",
    )(cond, w_ada, b_ada.reshape(DEPTH, 1, 6 * D_MODEL))


IN_EXT = Q_START + ATTN_WIDTH + 2 * KV_REP


def _inproj_kernel(x_ref, sh_ref, sc_ref, g_ref, w_ref, cos_ref, sa_ref, sb_ref,
                   bgu_ref, q_ref, kk_ref, vv_ref):
    h = _norm_mod(x_ref[0], g_ref[...], sh_ref[0], sc_ref[0]).astype(BF16)
    p = jnp.dot(h, w_ref[...], preferred_element_type=F32)
    bgu_ref[0, :, :CONV_WIDTH] = p[:, :CONV_WIDTH].astype(BF16)
    bgu_ref[0, :, CONV_WIDTH:] = (p[:, CONV_WIDTH:2 * CONV_WIDTH]
                                  * p[:, 2 * CONV_WIDTH:Q_START]).astype(BF16)
    cos = cos_ref[...]
    sa = sa_ref[...]
    sb = sb_ref[...]

    def rope(t):
        return t * cos + pltpu.roll(t, LANES - 16, 1) * sa + pltpu.roll(t, 16, 1) * sb

    for j in range(ATTN_WIDTH // LANES):
        lo = Q_START + j * LANES
        q_ref[0, :, j * LANES:(j + 1) * LANES] = (
            rope(p[:, lo:lo + LANES]) * (HEAD_DIM ** -0.5)).astype(BF16)
    k0 = Q_START + ATTN_WIDTH
    for j in range(KV_REP // LANES):
        lo = k0 + j * LANES
        kk_ref[0, :, j * LANES:(j + 1) * LANES] = rope(p[:, lo:lo + LANES]).astype(BF16)
    vv_ref[0] = p[:, k0 + KV_REP:].astype(BF16)


def _inproj(x, shift, scale, g, w_ext, cos, sa, sb, tm):
    b, n, _ = x.shape
    nt = n // tm
    tok = lambda w: pl.BlockSpec((1, tm, w), lambda i, bb: (bb, i, 0))
    mod = pl.BlockSpec((1, 1, D_MODEL), lambda i, bb: (bb, 0, 0))
    tab = pl.BlockSpec((tm, LANES), lambda i, bb: (i, 0))
    return pl.pallas_call(
        _inproj_kernel,
        grid=(nt, b),
        in_specs=[tok(D_MODEL), mod, mod, _const_spec((1, D_MODEL)),
                  _const_spec((D_MODEL, IN_EXT)), tab, tab, tab],
        out_specs=[tok(D_MODEL), tok(ATTN_WIDTH), tok(KV_REP), tok(KV_REP)],
        out_shape=[jax.ShapeDtypeStruct((b, n, D_MODEL), BF16),
                   jax.ShapeDtypeStruct((b, n, ATTN_WIDTH), BF16),
                   jax.ShapeDtypeStruct((b, n, KV_REP), BF16),
                   jax.ShapeDtypeStruct((b, n, KV_REP), BF16)],
        compiler_params=_params(2),
        name="inproj",
    )(x, shift, scale, g, w_ext, cos, sa, sb)


def _softmax_parts(parts, sink_col):
    m = sink_col
    for s in parts:
        m = jnp.maximum(m, jnp.max(s, axis=-1, keepdims=True))
    es = [jnp.exp(s - m) for s in parts]
    den = jnp.exp(sink_col - m)
    for e in es:
        den = den + jnp.sum(e, axis=-1, keepdims=True)
    return es, 1.0 / den


def _mixout_kernel(*refs, tile_rows, local):
    if local:
        (sink_ref, bgu_ref, bgu_p_ref, bgu_n_ref, q_ref,
         kk_ref, kk_p_ref, kk_n_ref, vv_ref, vv_p_ref, vv_n_ref, kkc_ref, vvc_ref,
         bias_mid_ref, bias_a_ref, bias_b_ref, cw_ref, o_ref) = refs
    else:
        (sink_ref, bgu_ref, bgu_p_ref, bgu_n_ref, q_ref,
         kkc_ref, vvc_ref, cw_ref, o_ref) = refs
    i = pl.program_id(0)
    n_tiles = pl.num_programs(0)
    r = tile_rows
    nb = r // BLOCK

    bg = bgu_ref[0, :, :CONV_WIDTH].astype(F32)
    u = bgu_ref[0, :, CONV_WIDTH:].astype(F32)
    has_prev = (i > 0).astype(F32)
    has_next = (i < n_tiles - 1).astype(F32)
    u_prev = bgu_p_ref[0, SUBLANES - 1:SUBLANES, CONV_WIDTH:].astype(F32) * has_prev
    u_next = bgu_n_ref[0, 0:1, CONV_WIDTH:].astype(F32) * has_next
    row = lax.broadcasted_iota(jnp.int32, (r, CONV_WIDTH), 0)
    u_up = jnp.where(row == 0, u_prev, pltpu.roll(u, 1, 0))
    u_dn = jnp.where(row == r - 1, u_next, pltpu.roll(u, r - 1, 0))
    cw = cw_ref[...]
    a_out = (bg * (u_up * cw[0:1] + u * cw[1:2] + u_dn * cw[2:3])).astype(BF16)

    lane = lax.broadcasted_iota(jnp.int32, (1, LANES), 1)
    first_half = lane < HEAD_DIM
    rows2 = lax.broadcasted_iota(jnp.int32, (2 * BLOCK, 1), 0)
    top = rows2 < BLOCK
    zero = jnp.zeros((), BF16)

    def halves(t):
        return jnp.concatenate([jnp.where(first_half, t, zero), jnp.where(first_half, zero, t)], axis=0)

    b_cols = []
    for h in range(N_KV_HEADS):
        hs = slice(h * LANES, (h + 1) * LANES)
        kc_bd = halves(kkc_ref[0, :, hs])
        vc_bd = halves(vvc_ref[0, :, hs])
        n_ctx = kkc_ref.shape[1]
        if local:
            k_all = jnp.concatenate([kk_p_ref[0, :, hs], kk_ref[0, :, hs], kk_n_ref[0, :, hs]], axis=0)
            v_all = jnp.concatenate([vv_p_ref[0, :, hs], vv_ref[0, :, hs], vv_n_ref[0, :, hs]], axis=0)
        sink_a = jnp.where(top, sink_ref[4 * h + 0], sink_ref[4 * h + 2])
        sink_b = jnp.where(top, sink_ref[4 * h + 1], sink_ref[4 * h + 3])
        outs = []
        for j in range(nb):
            rs = slice(j * BLOCK, (j + 1) * BLOCK)
            q2 = jnp.concatenate([q_ref[0, rs, 2 * h * LANES:(2 * h + 1) * LANES],
                                  q_ref[0, rs, (2 * h + 1) * LANES:(2 * h + 2) * LANES]], axis=0)
            s_ctx = lax.dot_general(q2, kc_bd, (((1,), (1,)), ((), ())), preferred_element_type=F32)
            parts_a = [s_ctx[:, :n_ctx]]
            parts_b = [s_ctx[:, n_ctx:]]
            if local:
                nl = 3 * BLOCK
                kl_bd = halves(k_all[j * BLOCK:j * BLOCK + nl])
                vl_bd = halves(v_all[j * BLOCK:j * BLOCK + nl])
                s_loc = lax.dot_general(q2, kl_bd, (((1,), (1,)), ((), ())), preferred_element_type=F32)
                if nb == 1:
                    bias = bias_a_ref[0] + bias_b_ref[0]
                elif j == 0:
                    bias = bias_a_ref[0]
                elif j == nb - 1:
                    bias = bias_b_ref[0]
                else:
                    bias = bias_mid_ref[0]
                s_loc = s_loc + bias
                parts_a.append(s_loc[:, :nl])
                parts_b.append(s_loc[:, nl:])
            es_a, inv_a = _softmax_parts(parts_a, sink_a)
            es_b, inv_b = _softmax_parts(parts_b, sink_b)
            e_ctx = jnp.concatenate([es_a[0], es_b[0]], axis=1).astype(BF16)
            o = jnp.dot(e_ctx, vc_bd, preferred_element_type=F32)
            if local:
                e_loc = jnp.concatenate([es_a[1], es_b[1]], axis=1).astype(BF16)
                o = o + jnp.dot(e_loc, vl_bd, preferred_element_type=F32)
            o = o * jnp.where(first_half, inv_a, inv_b)
            outs.append(jnp.concatenate([o[:BLOCK], o[BLOCK:]], axis=1))
        b_cols.append(jnp.concatenate(outs, axis=0) if nb > 1 else outs[0])
    o_ref[0, :, :CONV_WIDTH] = a_out
    o_ref[0, :, CONV_WIDTH:] = jnp.concatenate(b_cols, axis=1).astype(BF16)


def _band_bias():
    rr = np.arange(2 * BLOCK)[:, None] % BLOCK
    kidx = np.arange(6 * BLOCK)[None, :] % (3 * BLOCK)
    band = (kidx >= rr) & (kidx <= rr + 2 * WINDOW)
    mid = band
    first = band & (kidx >= BLOCK)
    last = band & (kidx < 2 * BLOCK)
    return jnp.asarray(np.where(np.stack([mid, first, last]), 0.0, NEG_BIG), dtype=F32)


def _mixout(bgu, q, kk, vv, kkc, vvc, sink, conv_w, tile_rows, local):
    b, n, _ = bgu.shape
    r = tile_rows
    nt = n // r
    nctx = kkc.shape[1]
    tok = lambda w: pl.BlockSpec((1, r, w), lambda i, bb: (bb, i, 0))
    r8 = r // SUBLANES
    halo8_p = pl.BlockSpec((1, SUBLANES, D_MODEL), lambda i, bb: (bb, jnp.maximum(i * r8 - 1, 0), 0))
    halo8_n = pl.BlockSpec((1, SUBLANES, D_MODEL),
                           lambda i, bb: (bb, jnp.minimum((i + 1) * r8, n // SUBLANES - 1), 0))
    ctx = pl.BlockSpec((1, nctx, KV_REP), lambda i, bb: (bb, 0, 0))
    smem = pl.BlockSpec(memory_space=pltpu.SMEM)
    in_specs = [smem, tok(D_MODEL), halo8_p, halo8_n, tok(ATTN_WIDTH)]
    args = [sink, bgu, bgu, bgu, q]
    if local:
        rb = r // BLOCK
        halo_p = pl.BlockSpec((1, BLOCK, KV_REP), lambda i, bb: (bb, jnp.maximum(i * rb - 1, 0), 0))
        halo_n = pl.BlockSpec((1, BLOCK, KV_REP),
                              lambda i, bb: (bb, jnp.minimum((i + 1) * rb, n // BLOCK - 1), 0))
        bias = _band_bias()
        bshape = (1, 2 * BLOCK, 6 * BLOCK)
        in_specs += [tok(KV_REP), halo_p, halo_n, tok(KV_REP), halo_p, halo_n, ctx, ctx,
                     pl.BlockSpec(bshape, lambda i, bb: (0, 0, 0)),
                     pl.BlockSpec(bshape, lambda i, bb: (jnp.where(i == 0, 1, 0), 0, 0)),
                     pl.BlockSpec(bshape, lambda i, bb: (jnp.where(i == nt - 1, 2, 0), 0, 0))]
        args += [kk, kk, kk, vv, vv, vv, kkc, vvc, bias, bias, bias]
    else:
        in_specs += [ctx, ctx]
        args += [kkc, vvc]
    in_specs += [_const_spec((3, CONV_WIDTH))]
    args += [conv_w]
    return pl.pallas_call(
        functools.partial(_mixout_kernel, tile_rows=r, local=local),
        grid=(nt, b),
        in_specs=in_specs,
        out_specs=tok(D_MODEL),
        out_shape=jax.ShapeDtypeStruct((b, n, D_MODEL), BF16),
        compiler_params=_params(2),
        name="mixout_local" if local else "mixout_ctx",
    )(*args)


def _chan_dft_kernel(x_ref, sh_ref, sc_ref, g_ref, cs_ref, y1_ref, y2_ref):
    h = _norm_mod(x_ref[0], g_ref[...], sh_ref[0], sc_ref[0]).astype(BF16)
    fg = FOURIER_GROUP
    for gi in range(N_FOURIER_GROUPS):
        yc = jnp.dot(h[:, gi * fg:(gi + 1) * fg], cs_ref[...], preferred_element_type=F32)
        y1_ref[0, :, gi * fg:(gi + 1) * fg] = yc[:, :fg].astype(BF16)
        y2_ref[0, :, gi * fg:(gi + 1) * fg] = yc[:, fg:].astype(BF16)


def _chan_dft(x, shift, scale, g, cs, tm):
    b, n, _ = x.shape
    tok = pl.BlockSpec((1, tm, D_MODEL), lambda i, bb: (bb, i, 0))
    mod = pl.BlockSpec((1, 1, D_MODEL), lambda i, bb: (bb, 0, 0))
    return pl.pallas_call(
        _chan_dft_kernel,
        grid=(n // tm, b),
        in_specs=[tok, mod, mod, _const_spec((1, D_MODEL)), _const_spec((FOURIER_GROUP, 2 * FOURIER_GROUP))],
        out_specs=[tok, tok],
        out_shape=[jax.ShapeDtypeStruct((b, n, D_MODEL), BF16)] * 2,
        compiler_params=_params(2),
        name="chan_dft",
    )(x, shift, scale, g, cs)


def _seq_dft_kernel(cn_ref, sn_ref, y1_ref, y2_ref, o_ref):
    z = jnp.dot(cn_ref[...], y1_ref[0], preferred_element_type=F32)
    z = z + jnp.dot(sn_ref[...], y2_ref[0], preferred_element_type=F32)
    o_ref[0] = z.astype(BF16)


def _seq_dft(cn, nsn, y1, y2, tm):
    b, n, _ = y1.shape
    tok = pl.BlockSpec((1, tm, D_MODEL), lambda bb, i: (bb, i, 0))
    mat = pl.BlockSpec((tm, n), lambda bb, i: (i, 0))
    full = pl.BlockSpec((1, n, D_MODEL), lambda bb, i: (bb, 0, 0), pipeline_mode=pl.Buffered(1))
    return pl.pallas_call(
        _seq_dft_kernel,
        grid=(b, n // tm),
        in_specs=[mat, mat, full, full],
        out_specs=tok,
        out_shape=jax.ShapeDtypeStruct((b, n, D_MODEL), BF16),
        compiler_params=_params(2),
        name="seq_dft",
    )(cn, nsn, y1, y2)


def _dft_mats(n, scale):
    j = jnp.arange(n, dtype=jnp.int32)
    m = (j[:, None] * j[None, :]) % n
    ang = m.astype(F32) * np.float32(2.0 * np.pi / n)
    return jnp.cos(ang) * scale, jnp.sin(ang) * scale


FFT_RADIX = 16
FFT_INNER = 256
FFT_N = FFT_RADIX * FFT_INNER
PERM_TILE = 512
PERM_B = PERM_TILE // FFT_RADIX


def _chan_dft_perm_kernel(x_ref, sh_ref, sc_ref, g_ref, p_ref, cs_ref, y_ref):
    fg = FOURIER_GROUP
    h = _norm_mod(x_ref[0], g_ref[...], sh_ref[0], sc_ref[0]).astype(BF16)
    hp = jnp.dot(p_ref[...], h, preferred_element_type=F32).astype(BF16)
    for gi in range(N_FOURIER_GROUPS):
        gs = slice(gi * fg, (gi + 1) * fg)
        yc = jnp.dot(hp[:, gs], cs_ref[...], preferred_element_type=F32)
        for a in range(FFT_RADIX):
            rs = slice(a * PERM_B, (a + 1) * PERM_B)
            y_ref[0, a, 0, :, gs] = yc[rs, :fg].astype(BF16)
            y_ref[0, a, 1, :, gs] = yc[rs, fg:].astype(BF16)


def _chan_dft_perm(x, shift, scale, g, cs):
    b, n, _ = x.shape
    r_out = np.arange(PERM_TILE)
    perm = np.zeros((PERM_TILE, PERM_TILE), np.float32)
    perm[r_out, FFT_RADIX * (r_out % PERM_B) + r_out // PERM_B] = 1.0
    tok = pl.BlockSpec((1, PERM_TILE, D_MODEL), lambda i, bb: (bb, i, 0))
    mod = pl.BlockSpec((1, 1, D_MODEL), lambda i, bb: (bb, 0, 0))
    return pl.pallas_call(
        _chan_dft_perm_kernel,
        grid=(n // PERM_TILE, b),
        in_specs=[tok, mod, mod, _const_spec((1, D_MODEL)), _const_spec((PERM_TILE, PERM_TILE)),
                  _const_spec((FOURIER_GROUP, 2 * FOURIER_GROUP))],
        out_specs=pl.BlockSpec((1, FFT_RADIX, 2, PERM_B, D_MODEL), lambda i, bb: (bb, 0, 0, i, 0)),
        out_shape=jax.ShapeDtypeStruct((b, FFT_RADIX, 2, n // FFT_RADIX, D_MODEL), BF16),
        compiler_params=_params(2),
        name="chan_dft_perm",
    )(x, shift, scale, g, jnp.asarray(perm, BF16), cs)


_C1 = float(np.cos(np.pi / 8))
_C2 = float(np.cos(np.pi / 4))
_C3 = float(np.cos(3 * np.pi / 8))


def _dft16_real(ur, ui):
    p = [ur[0]] + [ur[a] + ur[16 - a] for a in range(1, 8)] + [ur[8]]
    q = [p[a] + p[8 - a] for a in range(4)] + [p[4]]
    r = [p[a] - p[8 - a] for a in range(4)]
    s0, s1, s2 = q[0] + q[4], q[1] + q[3], q[2]
    t0, t1 = q[0] - q[4], q[1] - q[3]
    ca = [None] * 9
    ca[0] = s0 + s1 + s2
    ca[4] = s0 - s2
    ca[8] = s0 - s1 + s2
    ca[2] = t0 + _C2 * t1
    ca[6] = t0 - _C2 * t1
    e0, e1 = r[0] + _C2 * r[2], r[0] - _C2 * r[2]
    f, g = _C1 * r[1] + _C3 * r[3], _C3 * r[1] - _C1 * r[3]
    ca[1], ca[7], ca[3], ca[5] = e0 + f, e0 - f, e1 + g, e1 - g
    pp = [None] + [ui[a] - ui[16 - a] for a in range(1, 8)]
    qq = [None] + [pp[a] + pp[8 - a] for a in range(1, 4)] + [pp[4]]
    rr = [None] + [pp[a] - pp[8 - a] for a in range(1, 4)]
    sb = [None] * 8
    e0, e1 = _C2 * qq[2] + qq[4], _C2 * qq[2] - qq[4]
    f, g = _C3 * qq[1] + _C1 * qq[3], _C1 * qq[1] - _C3 * qq[3]
    sb[1], sb[7], sb[3], sb[5] = f + e0, f - e0, g + e1, g - e1
    w = _C2 * (rr[1] + rr[3])
    sb[2], sb[6], sb[4] = w + rr[2], w - rr[2], rr[1] - rr[3]
    out = [None] * 16
    out[0], out[8] = ca[0], ca[8]
    for c in range(1, 8):
        out[c] = ca[c] + sb[c]
        out[16 - c] = ca[c] - sb[c]
    return out


def _seq_fft_kernel(y_ref, l_ref, o_ref, u_ref):
    for a in range(FFT_RADIX):
        u = jnp.dot(l_ref[a, :, :FFT_INNER], y_ref[0, a, 0], preferred_element_type=F32)
        u_ref[a] = u + jnp.dot(l_ref[a, :, FFT_INNER:], y_ref[0, a, 1], preferred_element_type=F32)

    rt = 2 * SUBLANES

    def body(i, carry):
        r0 = pl.multiple_of(i * rt, rt)
        for lh in range(FOURIER_GROUP // LANES):
            ls = slice(lh * LANES, (lh + 1) * LANES)
            ur = [u_ref[a, pl.ds(r0, rt), ls] for a in range(FFT_RADIX)]
            ui = [u_ref[a, pl.ds(FFT_INNER + r0, rt), ls] for a in range(FFT_RADIX)]
            out = _dft16_real(ur, ui)
            for c in range(FFT_RADIX):
                o_ref[0, pl.ds(c * FFT_INNER + r0, rt), ls] = out[c].astype(BF16)
        return carry

    lax.fori_loop(0, FFT_INNER // rt, body, 0)


def _seq_fft(ycat, lmat):
    b = ycat.shape[0]
    ysp = pl.BlockSpec((1, FFT_RADIX, 2, FFT_INNER, FOURIER_GROUP), lambda bb, gi: (bb, 0, 0, 0, gi))
    return pl.pallas_call(
        _seq_fft_kernel,
        grid=(b, N_FOURIER_GROUPS),
        in_specs=[ysp, _const_spec((FFT_RADIX, 2 * FFT_INNER, 2 * FFT_INNER))],
        out_specs=pl.BlockSpec((1, FFT_N, FOURIER_GROUP), lambda bb, gi: (bb, 0, gi)),
        out_shape=jax.ShapeDtypeStruct((b, FFT_N, D_MODEL), BF16),
        scratch_shapes=[pltpu.VMEM((FFT_RADIX, 2 * FFT_INNER, FOURIER_GROUP), F32)],
        compiler_params=_params(2),
        name="seq_fft",
    )(ycat, lmat)


def _fft_stage_mats():
    a = jnp.arange(FFT_RADIX, dtype=jnp.int32)[:, None, None]
    d = jnp.arange(FFT_INNER, dtype=jnp.int32)[None, :, None]
    bb = jnp.arange(FFT_INNER, dtype=jnp.int32)[None, None, :]
    m = (d * (a + FFT_RADIX * bb)) % FFT_N
    ang = m.astype(F32) * np.float32(2.0 * np.pi / FFT_N)
    c = jnp.cos(ang) * (FFT_N ** -0.5)
    s = jnp.sin(ang) * (FFT_N ** -0.5)
    top = jnp.concatenate([c, -s], axis=2)
    bot = jnp.concatenate([-s, -c], axis=2)
    return jnp.concatenate([top, bot], axis=1).astype(BF16)


FF_CHUNK = 256


def _tail_kernel(*refs, final):
    if final:
        (x_ref, m_ref, g1_ref, wm_ref, sh_ref, sc_ref, gate_ref, g_ref, wg_ref, wu_ref, wd_ref,
         fg_ref, o_ref, acc_ref) = refs
    else:
        (x_ref, m_ref, g1_ref, wm_ref, sh_ref, sc_ref, gate_ref, g_ref, wg_ref, wu_ref, wd_ref,
         o_ref, acc_ref) = refs
    x = x_ref[0] + g1_ref[0] * jnp.dot(m_ref[0], wm_ref[...], preferred_element_type=F32)
    h = _norm_mod(x, g_ref[...], sh_ref[0], sc_ref[0]).astype(BF16)
    for c in range(D_FF // FF_CHUNK):
        cs = slice(c * FF_CHUNK, (c + 1) * FF_CHUNK)
        gt = jnp.dot(h, wg_ref[:, cs], preferred_element_type=F32)
        up = jnp.dot(h, wu_ref[:, cs], preferred_element_type=F32)
        a = (_silu(gt) * up).astype(BF16)
        part = jnp.dot(a, wd_ref[cs, :], preferred_element_type=F32)
        if c == 0:
            acc_ref[...] = part
        else:
            acc_ref[...] += part
    y = x + gate_ref[0] * acc_ref[...]
    if final:
        ms = jnp.mean(y * y, axis=-1, keepdims=True)
        y = (y * lax.rsqrt(ms + EPS)) * fg_ref[...]
    o_ref[0] = y


def _tail(x, m, gate1, wm, shift, scale, gate, g, wg, wu, wd, tm, final_g=None):
    b, n, _ = x.shape
    tok = pl.BlockSpec((1, tm, D_MODEL), lambda i, bb: (bb, i, 0))
    mod = pl.BlockSpec((1, 1, D_MODEL), lambda i, bb: (bb, 0, 0))
    in_specs = [tok, tok, mod, _const_spec((D_MODEL, D_MODEL)), mod, mod, mod, _const_spec((1, D_MODEL)),
                _const_spec((D_MODEL, D_FF)), _const_spec((D_MODEL, D_FF)), _const_spec((D_FF, D_MODEL))]
    args = [x, m, gate1, wm, shift, scale, gate, g, wg, wu, wd]
    if final_g is not None:
        in_specs.append(_const_spec((1, D_MODEL)))
        args.append(final_g)
    return pl.pallas_call(
        functools.partial(_tail_kernel, final=final_g is not None),
        grid=(n // tm, b),
        in_specs=in_specs,
        out_specs=tok,
        out_shape=jax.ShapeDtypeStruct((b, n, D_MODEL), F32),
        scratch_shapes=[pltpu.VMEM((tm, D_MODEL), F32)],
        compiler_params=_params(2),
        name="tail",
    )(*args)


def _rope_lane_tables(n):
    rows = n // GRID_W
    r = jnp.repeat(jnp.arange(rows), GRID_W).astype(F32)
    col = jnp.tile(jnp.arange(GRID_W), rows).astype(F32)
    quarter = HEAD_DIM // 4
    inv = ROPE_THETA ** (-jnp.arange(quarter, dtype=F32) / quarter)
    ang_r = r[:, None] * inv
    ang_c = col[:, None] * inv
    cr, sr, cc, sc = jnp.cos(ang_r), jnp.sin(ang_r), jnp.cos(ang_c), jnp.sin(ang_c)
    z = jnp.zeros_like(sr)
    cos = jnp.concatenate([cr, cr, cc, cc], axis=1)
    sa = jnp.concatenate([-sr, z, -sc, z], axis=1)
    sb = jnp.concatenate([z, sr, z, sc], axis=1)
    rep = lambda t: jnp.concatenate([t, t], axis=1)
    return rep(cos), rep(sa), rep(sb)


def _identity_lane_tables(n):
    return jnp.ones((n, LANES), F32), jnp.zeros((n, LANES), F32), jnp.zeros((n, LANES), F32)


def _extend_w_in(w):
    k = w[:, K_START:V_START]
    v = w[:, V_START:]
    dup = lambda t: jnp.concatenate([t[:, :HEAD_DIM], t[:, :HEAD_DIM], t[:, HEAD_DIM:], t[:, HEAD_DIM:]], axis=1)
    return jnp.concatenate([w[:, :K_START], dup(k), dup(v)], axis=1).astype(BF16)


def kernel(x, c, ctx, c_ctx, w_ada, b_ada, norm1_g, norm2_g, w_in, conv_w, sink,
           w_mix_out, w_fourier_out, w_ffn_gate, w_ffn_up, w_ffn_down, final_g):
    b, n, d = x.shape
    L = ctx.shape[1]
    tm = 512

    cond = jnp.concatenate([c, c_ctx[None, :], jnp.zeros((COND_ROWS - b - 1, d), F32)], axis=0)
    mods = _adaln(cond, w_ada, b_ada)

    def split(l, rows):
        m = mods[l, rows]
        return [m[:, None, k * d:(k + 1) * d] for k in range(6)]

    lat_rows = jnp.arange(b)
    ctx_rows = jnp.full((b,), b, jnp.int32)

    rope_lat = _rope_lane_tables(n)
    rope_ctx = _identity_lane_tables(L)
    cc, sc_ = _dft_mats(FOURIER_GROUP, FOURIER_GROUP ** -0.5)
    cs_chan = jnp.concatenate([cc, sc_], axis=1).astype(BF16)
    seq_mats = {}
    for m in {n, L} - {FFT_N}:
        cm, sm = _dft_mats(m, m ** -0.5)
        seq_mats[m] = (cm.astype(BF16), (-sm).astype(BF16))
    fft_l = _fft_stage_mats() if n == FFT_N else None

    xc = ctx
    for l in range(DEPTH):
        ctx_after = any(j % 2 == 0 for j in range(l + 1, DEPTH))
        ctx_here = (l % 2 == 0) or ctx_after
        sh1, sc1, g1, sh2, sc2, g2 = split(l, lat_rows)
        if ctx_here:
            csh1, csc1, cg1, csh2, csc2, cg2 = split(l, ctx_rows)
        n1 = norm1_g[l][None, :]
        n2 = norm2_g[l][None, :]
        if l % 2 == 0:
            e = l // 2
            w_ext = _extend_w_in(w_in[e])
            wm = w_mix_out[e].astype(BF16)
            sk = sink[e]
            bgu_c, q_c, kk_c, vv_c = _inproj(xc, csh1, csc1, n1, w_ext, *rope_ctx, tm=L)
            bgu, q, kk, vv = _inproj(x, sh1, sc1, n1, w_ext, *rope_lat, tm=tm)
            mix = _mixout(bgu, q, kk, vv, kk_c, vv_c, sk, conv_w[e], tile_rows=tm, local=True)
            if ctx_after:
                mix_c = _mixout(bgu_c, q_c, None, None, kk_c, vv_c, sk, conv_w[e], tile_rows=L, local=False)
        else:
            o = l // 2
            wm = w_fourier_out[o].astype(BF16)
            if n == FFT_N:
                mix = _seq_fft(_chan_dft_perm(x, sh1, sc1, n1, cs_chan), fft_l)
            else:
                mix = _seq_dft(*seq_mats[n], *_chan_dft(x, sh1, sc1, n1, cs_chan, tm=tm), tm=256)
            if ctx_after:
                mix_c = _seq_dft(*seq_mats[L], *_chan_dft(xc, csh1, csc1, n1, cs_chan, tm=L), tm=L)
        wg = w_ffn_gate[l].astype(BF16)
        wu = w_ffn_up[l].astype(BF16)
        wd = w_ffn_down[l].astype(BF16)
        x = _tail(x, mix, g1, wm, sh2, sc2, g2, n2, wg, wu, wd, tm=tm,
                  final_g=final_g[None, :] if l == DEPTH - 1 else None)
        if ctx_after:
            xc = _tail(xc, mix_c, cg1, wm, csh2, csc2, cg2, n2, wg, wu, wd, tm=L)
    return x
```

```python
import functools

import numpy as np
import jax
import jax.numpy as jnp
from jax import lax
from jax.experimental import pallas as pl
from jax.experimental.pallas import tpu as pltpu

F32 = jnp.float32
BF16 = jnp.bfloat16

D_MODEL = 1024
DEPTH = 4
GRID_W = 64
HEAD_DIM = 64
N_Q_HEADS = 8
N_KV_HEADS = 2
ATTN_WIDTH = N_Q_HEADS * HEAD_DIM
KV_WIDTH = N_KV_HEADS * HEAD_DIM
CONV_WIDTH = D_MODEL - ATTN_WIDTH
WINDOW = 128
BLOCK = 128
ROPE_THETA = 10000.0
N_FOURIER_GROUPS = 4
FOURIER_GROUP = D_MODEL // N_FOURIER_GROUPS
D_FF = 2816
EPS = 1e-6
Q_START = 3 * CONV_WIDTH
K_START = Q_START + ATTN_WIDTH
V_START = K_START + KV_WIDTH

LANES = 128
SUBLANES = 8
KV_REP = 2 * KV_WIDTH
NEG_BIG = -1e30
COND_ROWS = 24
VMEM_LIMIT = 56 * 1024 * 1024


def _params(n_axes):
    return pltpu.CompilerParams(dimension_semantics=("arbitrary",) * n_axes,
                                vmem_limit_bytes=VMEM_LIMIT)


def _const_spec(shape):
    nd = len(shape)
    return pl.BlockSpec(shape, lambda *_: (0,) * nd, pipeline_mode=pl.Buffered(1))


def _norm_mod(x, g, shift, scale):
    ms = jnp.mean(x * x, axis=-1, keepdims=True)
    y = x * lax.rsqrt(ms + EPS)
    return (y * g) * (1.0 + scale) + shift


def _silu(x):
    return x * (1.0 / (1.0 + jnp.exp(-x)))


def _adaln_kernel(c_ref, w_ref, b_ref, o_ref):
    a = _silu(c_ref[...])
    w = w_ref[0]
    a_hi = a.astype(BF16)
    a_lo = (a - a_hi.astype(F32)).astype(BF16)
    w_hi = w.astype(BF16)
    w_lo = (w - w_hi.astype(F32)).astype(BF16)
    acc = jnp.dot(a_hi, w_hi, preferred_element_type=F32)
    acc += jnp.dot(a_lo, w_hi, preferred_element_type=F32)
    acc += jnp.dot(a_hi, w_lo, preferred_element_type=F32)
    o_ref[0] = acc + b_ref[0]


def _adaln(cond, w_ada, b_ada):
    tn = 1024
    nt = (6 * D_MODEL) // tn
    return pl.pallas_call(
        _adaln_kernel,
        grid=(DEPTH, nt),
        in_specs=[
            pl.BlockSpec((COND_ROWS, D_MODEL), lambda l, j: (0, 0)),
            pl.BlockSpec((1, D_MODEL, tn), lambda l, j: (l, 0, j)),
            pl.BlockSpec((1, 1, tn), lambda l, j: (l, 0, j)),
        ],
        out_specs=pl.BlockSpec((1, COND_ROWS, tn), lambda l, j: (l, 0, j)),
        out_shape=jax.ShapeDtypeStruct((DEPTH, COND_ROWS, 6 * D_MODEL), F32),
        compiler_params=_params(2),
        name="adaln",
    )(cond, w_ada, b_ada.reshape(DEPTH, 1, 6 * D_MODEL))


IN_EXT = Q_START + ATTN_WIDTH + 2 * KV_REP
LOG2E = float(np.log2(np.e))
Q_SCALE = HEAD_DIM ** -0.5 * LOG2E


def _inproj_kernel(x_ref, sh_ref, sc_ref, g_ref, w_ref, cos_ref, sa_ref, sb_ref,
                   bgu_ref, q_ref, kk_ref, vv_ref):
    h = _norm_mod(x_ref[0], g_ref[...], sh_ref[0], sc_ref[0]).astype(BF16)
    p = jnp.dot(h, w_ref[...], preferred_element_type=F32)
    bgu_ref[0, :, :CONV_WIDTH] = p[:, :CONV_WIDTH].astype(BF16)
    bgu_ref[0, :, CONV_WIDTH:] = (p[:, CONV_WIDTH:2 * CONV_WIDTH]
                                  * p[:, 2 * CONV_WIDTH:Q_START]).astype(BF16)
    cos = cos_ref[...]
    sa = sa_ref[...]
    sb = sb_ref[...]

    def rope(t):
        return t * cos + pltpu.roll(t, LANES - 16, 1) * sa + pltpu.roll(t, 16, 1) * sb

    for j in range(ATTN_WIDTH // LANES):
        lo = Q_START + j * LANES
        q_ref[0, :, j * LANES:(j + 1) * LANES] = (rope(p[:, lo:lo + LANES]) * Q_SCALE).astype(BF16)
    k0 = Q_START + ATTN_WIDTH
    for j in range(KV_REP // LANES):
        lo = k0 + j * LANES
        kk_ref[0, :, j * LANES:(j + 1) * LANES] = rope(p[:, lo:lo + LANES]).astype(BF16)
    vv_ref[0] = p[:, k0 + KV_REP:].astype(BF16)


def _inproj(x, shift, scale, g, w_ext, cos, sa, sb, tm):
    b, n, _ = x.shape
    nt = n // tm
    tok = lambda w: pl.BlockSpec((1, tm, w), lambda i, bb: (bb, i, 0))
    mod = pl.BlockSpec((1, 1, D_MODEL), lambda i, bb: (bb, 0, 0))
    tab = pl.BlockSpec((tm, LANES), lambda i, bb: (i, 0))
    return pl.pallas_call(
        _inproj_kernel,
        grid=(nt, b),
        in_specs=[tok(D_MODEL), mod, mod, _const_spec((1, D_MODEL)),
                  _const_spec((D_MODEL, IN_EXT)), tab, tab, tab],
        out_specs=[tok(D_MODEL), tok(ATTN_WIDTH), tok(KV_REP), tok(KV_REP)],
        out_shape=[jax.ShapeDtypeStruct((b, n, D_MODEL), BF16),
                   jax.ShapeDtypeStruct((b, n, ATTN_WIDTH), BF16),
                   jax.ShapeDtypeStruct((b, n, KV_REP), BF16),
                   jax.ShapeDtypeStruct((b, n, KV_REP), BF16)],
        compiler_params=_params(2),
        name="inproj",
    )(x, shift, scale, g, w_ext, cos, sa, sb)


SM_ROWS = 16
SCORE_SLOTS = 2
MIX_ROWS = 1024


def _mixout_kernel(*refs, tile_rows, local):
    if local:
        (sink_ref, bgu_ref, bgu_p_ref, bgu_n_ref, q_ref,
         kk_ref, kk_p_ref, kk_n_ref, vv_ref, vv_p_ref, vv_n_ref, kkc_ref, vvc_ref,
         bias_mid_ref, bias_a_ref, bias_b_ref, cw_ref, o_ref, s_slots, e_slots) = refs
    else:
        (sink_ref, bgu_ref, bgu_p_ref, bgu_n_ref, q_ref,
         kkc_ref, vvc_ref, cw_ref, o_ref, s_slots, e_slots) = refs
    i = pl.program_id(0)
    n_tiles = pl.num_programs(0)
    r = tile_rows
    nb = r // BLOCK

    bg = bgu_ref[0, :, :CONV_WIDTH].astype(F32)
    u = bgu_ref[0, :, CONV_WIDTH:].astype(F32)
    has_prev = (i > 0).astype(F32)
    has_next = (i < n_tiles - 1).astype(F32)
    u_prev = bgu_p_ref[0, SUBLANES - 1:SUBLANES, CONV_WIDTH:].astype(F32) * has_prev
    u_next = bgu_n_ref[0, 0:1, CONV_WIDTH:].astype(F32) * has_next
    row = lax.broadcasted_iota(jnp.int32, (r, CONV_WIDTH), 0)
    u_up = jnp.where(row == 0, u_prev, pltpu.roll(u, 1, 0))
    u_dn = jnp.where(row == r - 1, u_next, pltpu.roll(u, r - 1, 0))
    cw = cw_ref[...]
    a_out = (bg * (u_up * cw[0:1] + u * cw[1:2] + u_dn * cw[2:3])).astype(BF16)

    lane = lax.broadcasted_iota(jnp.int32, (1, LANES), 1)
    first_half = lane < HEAD_DIM
    zero = jnp.zeros((), BF16)
    nt_dims = (((1,), (1,)), ((), ()))

    def split_k(t):
        return jnp.where(first_half, t, zero), jnp.where(first_half, zero, t)

    def split_v(t):
        va, vb = split_k(t)
        first = lax.broadcasted_iota(jnp.int32, t.shape, 1) < HEAD_DIM
        ones_a = jnp.where(first, 1.0, 0.0).astype(BF16)
        ones_b = jnp.where(first, 0.0, 1.0).astype(BF16)
        return jnp.concatenate([va, ones_a], axis=1), jnp.concatenate([vb, ones_b], axis=1)

    n_ctx = kkc_ref.shape[1]
    wc = 2 * n_ctx
    nl = 3 * BLOCK
    groups_per_pair = BLOCK // SM_ROWS
    b_cols = []
    for h in range(N_KV_HEADS):
        hs = slice(h * LANES, (h + 1) * LANES)
        kc_bd = jnp.concatenate(split_k(kkc_ref[0, :, hs]), axis=0)
        vc_ext = jnp.concatenate(split_v(vvc_ref[0, :, hs]), axis=0)
        if local:
            ka_all, kb_all = split_k(
                jnp.concatenate([kk_p_ref[0, :, hs], kk_ref[0, :, hs], kk_n_ref[0, :, hs]], axis=0))
            va_all, vb_all = split_v(
                jnp.concatenate([vv_p_ref[0, :, hs], vv_ref[0, :, hs], vv_n_ref[0, :, hs]], axis=0))
        sinks = [sink_ref[4 * h + i] * LOG2E for i in range(4)]
        outs = []
        for j in range(nb):
            rs = slice(j * BLOCK, (j + 1) * BLOCK)
            q2 = jnp.concatenate([q_ref[0, rs, 2 * h * LANES:(2 * h + 1) * LANES],
                                  q_ref[0, rs, (2 * h + 1) * LANES:(2 * h + 2) * LANES]], axis=0)
            slot = (h * nb + j) % SCORE_SLOTS
            s_ref = s_slots.at[slot]
            e_ref = e_slots.at[slot]
            s_ref[:, :wc] = lax.dot_general(q2, kc_bd, nt_dims, preferred_element_type=F32)
            cols_a = [slice(0, n_ctx)]
            cols_b = [slice(n_ctx, wc)]
            if local:
                ws = slice(j * BLOCK, j * BLOCK + nl)
                kl_bd = jnp.concatenate([ka_all[ws], kb_all[ws]], axis=0)
                if nb == 1:
                    bias = bias_a_ref[0] + bias_b_ref[0]
                elif j == 0:
                    bias = bias_a_ref[0]
                elif j == nb - 1:
                    bias = bias_b_ref[0]
                else:
                    bias = bias_mid_ref[0]
                s_ref[:, wc:] = lax.dot_general(q2, kl_bd, nt_dims, preferred_element_type=F32) + bias
                cols_a.append(slice(wc, wc + nl))
                cols_b.append(slice(wc + nl, wc + 2 * nl))
            sink_terms = []
            for g in range(2 * BLOCK // SM_ROWS):
                gr = slice(g * SM_ROWS, (g + 1) * SM_ROWS)
                pair = g // groups_per_pair
                terms = []
                for cols, sk in ((cols_a, sinks[2 * pair]), (cols_b, sinks[2 * pair + 1])):
                    ss = [s_ref[gr, c] for c in cols]
                    m = jnp.max(ss[0], axis=-1, keepdims=True)
                    for t in ss[1:]:
                        m = jnp.maximum(m, jnp.max(t, axis=-1, keepdims=True))
                    m = jnp.maximum(m, sk)
                    for c, t in zip(cols, ss):
                        e_ref[gr, c] = jnp.exp2(t - m).astype(BF16)
                    terms.append(jnp.exp2(sk - m))
                sink_terms.append(jnp.where(first_half, terms[0], terms[1]))
            o_ext = jnp.dot(e_ref[:, :wc], vc_ext, preferred_element_type=F32)
            if local:
                vl_ext = jnp.concatenate([va_all[ws], vb_all[ws]], axis=0)
                o_ext = o_ext + jnp.dot(e_ref[:, wc:], vl_ext, preferred_element_type=F32)
            o = o_ext[:, :LANES] / (o_ext[:, LANES:] + jnp.concatenate(sink_terms, axis=0))
            outs.append(jnp.concatenate([o[:BLOCK], o[BLOCK:]], axis=1))
        b_cols.append(jnp.concatenate(outs, axis=0) if nb > 1 else outs[0])
    o_ref[0, :, :CONV_WIDTH] = a_out
    o_ref[0, :, CONV_WIDTH:] = jnp.concatenate(b_cols, axis=1).astype(BF16)


def _band_bias():
    rr = np.arange(2 * BLOCK)[:, None] % BLOCK
    kidx = np.arange(6 * BLOCK)[None, :] % (3 * BLOCK)
    band = (kidx >= rr) & (kidx <= rr + 2 * WINDOW)
    mid = band
    first = band & (kidx >= BLOCK)
    last = band & (kidx < 2 * BLOCK)
    return jnp.asarray(np.where(np.stack([mid, first, last]), 0.0, NEG_BIG), dtype=F32)


def _mixout(bgu, q, kk, vv, kkc, vvc, sink, conv_w, tile_rows, local):
    b, n, _ = bgu.shape
    r = tile_rows
    nt = n // r
    nctx = kkc.shape[1]
    tok = lambda w: pl.BlockSpec((1, r, w), lambda i, bb: (bb, i, 0))
    r8 = r // SUBLANES
    halo8_p = pl.BlockSpec((1, SUBLANES, D_MODEL), lambda i, bb: (bb, jnp.maximum(i * r8 - 1, 0), 0))
    halo8_n = pl.BlockSpec((1, SUBLANES, D_MODEL),
                           lambda i, bb: (bb, jnp.minimum((i + 1) * r8, n // SUBLANES - 1), 0))
    ctx = pl.BlockSpec((1, nctx, KV_REP), lambda i, bb: (bb, 0, 0))
    smem = pl.BlockSpec(memory_space=pltpu.SMEM)
    in_specs = [smem, tok(D_MODEL), halo8_p, halo8_n, tok(ATTN_WIDTH)]
    args = [sink, bgu, bgu, bgu, q]
    if local:
        rb = r // BLOCK
        halo_p = pl.BlockSpec((1, BLOCK, KV_REP), lambda i, bb: (bb, jnp.maximum(i * rb - 1, 0), 0))
        halo_n = pl.BlockSpec((1, BLOCK, KV_REP),
                              lambda i, bb: (bb, jnp.minimum((i + 1) * rb, n // BLOCK - 1), 0))
        bias = _band_bias()
        bshape = (1, 2 * BLOCK, 6 * BLOCK)
        in_specs += [tok(KV_REP), halo_p, halo_n, tok(KV_REP), halo_p, halo_n, ctx, ctx,
                     pl.BlockSpec(bshape, lambda i, bb: (0, 0, 0)),
                     pl.BlockSpec(bshape, lambda i, bb: (jnp.where(i == 0, 1, 0), 0, 0)),
                     pl.BlockSpec(bshape, lambda i, bb: (jnp.where(i == nt - 1, 2, 0), 0, 0))]
        args += [kk, kk, kk, vv, vv, vv, kkc, vvc, bias, bias, bias]
    else:
        in_specs += [ctx, ctx]
        args += [kkc, vvc]
    in_specs += [_const_spec((3, CONV_WIDTH))]
    args += [conv_w]
    score_cols = 2 * nctx + (6 * BLOCK if local else 0)
    return pl.pallas_call(
        functools.partial(_mixout_kernel, tile_rows=r, local=local),
        grid=(nt, b),
        in_specs=in_specs,
        out_specs=tok(D_MODEL),
        out_shape=jax.ShapeDtypeStruct((b, n, D_MODEL), BF16),
        scratch_shapes=[pltpu.VMEM((SCORE_SLOTS, 2 * BLOCK, score_cols), F32),
                        pltpu.VMEM((SCORE_SLOTS, 2 * BLOCK, score_cols), BF16)],
        compiler_params=_params(2),
        name="mixout_local" if local else "mixout_ctx",
    )(*args)


def _chan_dft_kernel(x_ref, sh_ref, sc_ref, g_ref, cs_ref, y1_ref, y2_ref):
    h = _norm_mod(x_ref[0], g_ref[...], sh_ref[0], sc_ref[0]).astype(BF16)
    fg = FOURIER_GROUP
    for gi in range(N_FOURIER_GROUPS):
        yc = jnp.dot(h[:, gi * fg:(gi + 1) * fg], cs_ref[...], preferred_element_type=F32)
        y1_ref[0, :, gi * fg:(gi + 1) * fg] = yc[:, :fg].astype(BF16)
        y2_ref[0, :, gi * fg:(gi + 1) * fg] = yc[:, fg:].astype(BF16)


def _chan_dft(x, shift, scale, g, cs, tm):
    b, n, _ = x.shape
    tok = pl.BlockSpec((1, tm, D_MODEL), lambda i, bb: (bb, i, 0))
    mod = pl.BlockSpec((1, 1, D_MODEL), lambda i, bb: (bb, 0, 0))
    return pl.pallas_call(
        _chan_dft_kernel,
        grid=(n // tm, b),
        in_specs=[tok, mod, mod, _const_spec((1, D_MODEL)), _const_spec((FOURIER_GROUP, 2 * FOURIER_GROUP))],
        out_specs=[tok, tok],
        out_shape=[jax.ShapeDtypeStruct((b, n, D_MODEL), BF16)] * 2,
        compiler_params=_params(2),
        name="chan_dft",
    )(x, shift, scale, g, cs)


def _seq_dft_kernel(cn_ref, sn_ref, y1_ref, y2_ref, o_ref):
    z = jnp.dot(cn_ref[...], y1_ref[0], preferred_element_type=F32)
    z = z + jnp.dot(sn_ref[...], y2_ref[0], preferred_element_type=F32)
    o_ref[0] = z.astype(BF16)


def _seq_dft(cn, nsn, y1, y2, tm):
    b, n, _ = y1.shape
    tok = pl.BlockSpec((1, tm, D_MODEL), lambda bb, i: (bb, i, 0))
    mat = pl.BlockSpec((tm, n), lambda bb, i: (i, 0))
    full = pl.BlockSpec((1, n, D_MODEL), lambda bb, i: (bb, 0, 0), pipeline_mode=pl.Buffered(1))
    return pl.pallas_call(
        _seq_dft_kernel,
        grid=(b, n // tm),
        in_specs=[mat, mat, full, full],
        out_specs=tok,
        out_shape=jax.ShapeDtypeStruct((b, n, D_MODEL), BF16),
        compiler_params=_params(2),
        name="seq_dft",
    )(cn, nsn, y1, y2)


def _dft_mats(n, scale):
    j = jnp.arange(n, dtype=jnp.int32)
    m = (j[:, None] * j[None, :]) % n
    ang = m.astype(F32) * np.float32(2.0 * np.pi / n)
    return jnp.cos(ang) * scale, jnp.sin(ang) * scale


FFT_RADIX = 16
FFT_INNER = 256
FFT_N = FFT_RADIX * FFT_INNER
PERM_TILE = 512
PERM_B = PERM_TILE // FFT_RADIX


def _chan_dft_perm_kernel(x_ref, sh_ref, sc_ref, g_ref, p_ref, cs_ref, y_ref):
    fg = FOURIER_GROUP
    h = _norm_mod(x_ref[0], g_ref[...], sh_ref[0], sc_ref[0]).astype(BF16)
    hp = jnp.dot(p_ref[...], h, preferred_element_type=F32).astype(BF16)
    for gi in range(N_FOURIER_GROUPS):
        gs = slice(gi * fg, (gi + 1) * fg)
        yc = jnp.dot(hp[:, gs], cs_ref[...], preferred_element_type=F32)
        for a in range(FFT_RADIX):
            rs = slice(a * PERM_B, (a + 1) * PERM_B)
            y_ref[0, a, 0, :, gs] = yc[rs, :fg].astype(BF16)
            y_ref[0, a, 1, :, gs] = yc[rs, fg:].astype(BF16)


def _chan_dft_perm(x, shift, scale, g, cs):
    b, n, _ = x.shape
    r_out = np.arange(PERM_TILE)
    perm = np.zeros((PERM_TILE, PERM_TILE), np.float32)
    perm[r_out, FFT_RADIX * (r_out % PERM_B) + r_out // PERM_B] = 1.0
    tok = pl.BlockSpec((1, PERM_TILE, D_MODEL), lambda i, bb: (bb, i, 0))
    mod = pl.BlockSpec((1, 1, D_MODEL), lambda i, bb: (bb, 0, 0))
    return pl.pallas_call(
        _chan_dft_perm_kernel,
        grid=(n // PERM_TILE, b),
        in_specs=[tok, mod, mod, _const_spec((1, D_MODEL)), _const_spec((PERM_TILE, PERM_TILE)),
                  _const_spec((FOURIER_GROUP, 2 * FOURIER_GROUP))],
        out_specs=pl.BlockSpec((1, FFT_RADIX, 2, PERM_B, D_MODEL), lambda i, bb: (bb, 0, 0, i, 0)),
        out_shape=jax.ShapeDtypeStruct((b, FFT_RADIX, 2, n // FFT_RADIX, D_MODEL), BF16),
        compiler_params=_params(2),
        name="chan_dft_perm",
    )(x, shift, scale, g, jnp.asarray(perm, BF16), cs)


_C1 = float(np.cos(np.pi / 8))
_C2 = float(np.cos(np.pi / 4))
_C3 = float(np.cos(3 * np.pi / 8))


def _dft16_real(ur, ui):
    p = [ur[0]] + [ur[a] + ur[16 - a] for a in range(1, 8)] + [ur[8]]
    q = [p[a] + p[8 - a] for a in range(4)] + [p[4]]
    r = [p[a] - p[8 - a] for a in range(4)]
    s0, s1, s2 = q[0] + q[4], q[1] + q[3], q[2]
    t0, t1 = q[0] - q[4], q[1] - q[3]
    ca = [None] * 9
    ca[0] = s0 + s1 + s2
    ca[4] = s0 - s2
    ca[8] = s0 - s1 + s2
    ca[2] = t0 + _C2 * t1
    ca[6] = t0 - _C2 * t1
    e0, e1 = r[0] + _C2 * r[2], r[0] - _C2 * r[2]
    f, g = _C1 * r[1] + _C3 * r[3], _C3 * r[1] - _C1 * r[3]
    ca[1], ca[7], ca[3], ca[5] = e0 + f, e0 - f, e1 + g, e1 - g
    pp = [None] + [ui[a] - ui[16 - a] for a in range(1, 8)]
    qq = [None] + [pp[a] + pp[8 - a] for a in range(1, 4)] + [pp[4]]
    rr = [None] + [pp[a] - pp[8 - a] for a in range(1, 4)]
    sb = [None] * 8
    e0, e1 = _C2 * qq[2] + qq[4], _C2 * qq[2] - qq[4]
    f, g = _C3 * qq[1] + _C1 * qq[3], _C1 * qq[1] - _C3 * qq[3]
    sb[1], sb[7], sb[3], sb[5] = f + e0, f - e0, g + e1, g - e1
    w = _C2 * (rr[1] + rr[3])
    sb[2], sb[6], sb[4] = w + rr[2], w - rr[2], rr[1] - rr[3]
    out = [None] * 16
    out[0], out[8] = ca[0], ca[8]
    for c in range(1, 8):
        out[c] = ca[c] + sb[c]
        out[16 - c] = ca[c] - sb[c]
    return out


def _seq_fft_kernel(y_ref, l_ref, o_ref, u_ref):
    for a in range(FFT_RADIX):
        u = jnp.dot(l_ref[a, :, :FFT_INNER], y_ref[0, a, 0], preferred_element_type=F32)
        u_ref[a] = u + jnp.dot(l_ref[a, :, FFT_INNER:], y_ref[0, a, 1], preferred_element_type=F32)

    rt = 2 * SUBLANES

    def body(i, carry):
        r0 = pl.multiple_of(i * rt, rt)
        for lh in range(FOURIER_GROUP // LANES):
            ls = slice(lh * LANES, (lh + 1) * LANES)
            ur = [u_ref[a, pl.ds(r0, rt), ls] for a in range(FFT_RADIX)]
            ui = [u_ref[a, pl.ds(FFT_INNER + r0, rt), ls] for a in range(FFT_RADIX)]
            out = _dft16_real(ur, ui)
            for c in range(FFT_RADIX):
                o_ref[0, pl.ds(c * FFT_INNER + r0, rt), ls] = out[c].astype(BF16)
        return carry

    lax.fori_loop(0, FFT_INNER // rt, body, 0)


def _seq_fft(ycat, lmat):
    b = ycat.shape[0]
    ysp = pl.BlockSpec((1, FFT_RADIX, 2, FFT_INNER, FOURIER_GROUP), lambda bb, gi: (bb, 0, 0, 0, gi))
    return pl.pallas_call(
        _seq_fft_kernel,
        grid=(b, N_FOURIER_GROUPS),
        in_specs=[ysp, _const_spec((FFT_RADIX, 2 * FFT_INNER, 2 * FFT_INNER))],
        out_specs=pl.BlockSpec((1, FFT_N, FOURIER_GROUP), lambda bb, gi: (bb, 0, gi)),
        out_shape=jax.ShapeDtypeStruct((b, FFT_N, D_MODEL), BF16),
        scratch_shapes=[pltpu.VMEM((FFT_RADIX, 2 * FFT_INNER, FOURIER_GROUP), F32)],
        compiler_params=_params(2),
        name="seq_fft",
    )(ycat, lmat)


def _fft_stage_mats():
    a = jnp.arange(FFT_RADIX, dtype=jnp.int32)[:, None, None]
    d = jnp.arange(FFT_INNER, dtype=jnp.int32)[None, :, None]
    bb = jnp.arange(FFT_INNER, dtype=jnp.int32)[None, None, :]
    m = (d * (a + FFT_RADIX * bb)) % FFT_N
    ang = m.astype(F32) * np.float32(2.0 * np.pi / FFT_N)
    c = jnp.cos(ang) * (FFT_N ** -0.5)
    s = jnp.sin(ang) * (FFT_N ** -0.5)
    top = jnp.concatenate([c, -s], axis=2)
    bot = jnp.concatenate([-s, -c], axis=2)
    return jnp.concatenate([top, bot], axis=1).astype(BF16)


FF_CHUNK = 256


def _tail_kernel(*refs, final):
    if final:
        (x_ref, m_ref, g1_ref, wm_ref, sh_ref, sc_ref, gate_ref, g_ref, wg_ref, wu_ref, wd_ref,
         fg_ref, o_ref, acc_ref) = refs
    else:
        (x_ref, m_ref, g1_ref, wm_ref, sh_ref, sc_ref, gate_ref, g_ref, wg_ref, wu_ref, wd_ref,
         o_ref, acc_ref) = refs
    x = x_ref[0] + g1_ref[0] * jnp.dot(m_ref[0], wm_ref[...], preferred_element_type=F32)
    h = _norm_mod(x, g_ref[...], sh_ref[0], sc_ref[0]).astype(BF16)
    for c in range(D_FF // FF_CHUNK):
        cs = slice(c * FF_CHUNK, (c + 1) * FF_CHUNK)
        gt = jnp.dot(h, wg_ref[:, cs], preferred_element_type=F32)
        up = jnp.dot(h, wu_ref[:, cs], preferred_element_type=F32)
        a = (_silu(gt) * up).astype(BF16)
        part = jnp.dot(a, wd_ref[cs, :], preferred_element_type=F32)
        if c == 0:
            acc_ref[...] = part
        else:
            acc_ref[...] += part
    y = x + gate_ref[0] * acc_ref[...]
    if final:
        ms = jnp.mean(y * y, axis=-1, keepdims=True)
        y = (y * lax.rsqrt(ms + EPS)) * fg_ref[...]
    o_ref[0] = y


def _tail(x, m, gate1, wm, shift, scale, gate, g, wg, wu, wd, tm, final_g=None):
    b, n, _ = x.shape
    tok = pl.BlockSpec((1, tm, D_MODEL), lambda i, bb: (bb, i, 0))
    mod = pl.BlockSpec((1, 1, D_MODEL), lambda i, bb: (bb, 0, 0))
    in_specs = [tok, tok, mod, _const_spec((D_MODEL, D_MODEL)), mod, mod, mod, _const_spec((1, D_MODEL)),
                _const_spec((D_MODEL, D_FF)), _const_spec((D_MODEL, D_FF)), _const_spec((D_FF, D_MODEL))]
    args = [x, m, gate1, wm, shift, scale, gate, g, wg, wu, wd]
    if final_g is not None:
        in_specs.append(_const_spec((1, D_MODEL)))
        args.append(final_g)
    return pl.pallas_call(
        functools.partial(_tail_kernel, final=final_g is not None),
        grid=(n // tm, b),
        in_specs=in_specs,
        out_specs=tok,
        out_shape=jax.ShapeDtypeStruct((b, n, D_MODEL), F32),
        scratch_shapes=[pltpu.VMEM((tm, D_MODEL), F32)],
        compiler_params=_params(2),
        name="tail",
    )(*args)


def _rope_lane_tables(n):
    rows = n // GRID_W
    r = jnp.repeat(jnp.arange(rows), GRID_W).astype(F32)
    col = jnp.tile(jnp.arange(GRID_W), rows).astype(F32)
    quarter = HEAD_DIM // 4
    inv = ROPE_THETA ** (-jnp.arange(quarter, dtype=F32) / quarter)
    ang_r = r[:, None] * inv
    ang_c = col[:, None] * inv
    cr, sr, cc, sc = jnp.cos(ang_r), jnp.sin(ang_r), jnp.cos(ang_c), jnp.sin(ang_c)
    z = jnp.zeros_like(sr)
    cos = jnp.concatenate([cr, cr, cc, cc], axis=1)
    sa = jnp.concatenate([-sr, z, -sc, z], axis=1)
    sb = jnp.concatenate([z, sr, z, sc], axis=1)
    rep = lambda t: jnp.concatenate([t, t], axis=1)
    return rep(cos), rep(sa), rep(sb)


def _identity_lane_tables(n):
    return jnp.ones((n, LANES), F32), jnp.zeros((n, LANES), F32), jnp.zeros((n, LANES), F32)


def _extend_w_in(w):
    k = w[:, K_START:V_START]
    v = w[:, V_START:]
    dup = lambda t: jnp.concatenate([t[:, :HEAD_DIM], t[:, :HEAD_DIM], t[:, HEAD_DIM:], t[:, HEAD_DIM:]], axis=1)
    return jnp.concatenate([w[:, :K_START], dup(k), dup(v)], axis=1).astype(BF16)


def kernel(x, c, ctx, c_ctx, w_ada, b_ada, norm1_g, norm2_g, w_in, conv_w, sink,
           w_mix_out, w_fourier_out, w_ffn_gate, w_ffn_up, w_ffn_down, final_g):
    b, n, d = x.shape
    L = ctx.shape[1]
    tm = 512

    cond = jnp.concatenate([c, c_ctx[None, :], jnp.zeros((COND_ROWS - b - 1, d), F32)], axis=0)
    mods = _adaln(cond, w_ada, b_ada)

    def split(l, rows):
        m = mods[l, rows]
        return [m[:, None, k * d:(k + 1) * d] for k in range(6)]

    lat_rows = jnp.arange(b)
    ctx_rows = jnp.full((b,), b, jnp.int32)

    rope_lat = _rope_lane_tables(n)
    rope_ctx = _identity_lane_tables(L)
    cc, sc_ = _dft_mats(FOURIER_GROUP, FOURIER_GROUP ** -0.5)
    cs_chan = jnp.concatenate([cc, sc_], axis=1).astype(BF16)
    seq_mats = {}
    for m in {n, L} - {FFT_N}:
        cm, sm = _dft_mats(m, m ** -0.5)
        seq_mats[m] = (cm.astype(BF16), (-sm).astype(BF16))
    fft_l = _fft_stage_mats() if n == FFT_N else None

    xc = ctx
    for l in range(DEPTH):
        ctx_after = any(j % 2 == 0 for j in range(l + 1, DEPTH))
        ctx_here = (l % 2 == 0) or ctx_after
        sh1, sc1, g1, sh2, sc2, g2 = split(l, lat_rows)
        if ctx_here:
            csh1, csc1, cg1, csh2, csc2, cg2 = split(l, ctx_rows)
        n1 = norm1_g[l][None, :]
        n2 = norm2_g[l][None, :]
        if l % 2 == 0:
            e = l // 2
            w_ext = _extend_w_in(w_in[e])
            wm = w_mix_out[e].astype(BF16)
            sk = sink[e]
            bgu_c, q_c, kk_c, vv_c = _inproj(xc, csh1, csc1, n1, w_ext, *rope_ctx, tm=L)
            bgu, q, kk, vv = _inproj(x, sh1, sc1, n1, w_ext, *rope_lat, tm=tm)
            mix = _mixout(bgu, q, kk, vv, kk_c, vv_c, sk, conv_w[e], tile_rows=min(n, MIX_ROWS), local=True)
            if ctx_after:
                mix_c = _mixout(bgu_c, q_c, None, None, kk_c, vv_c, sk, conv_w[e], tile_rows=L, local=False)
        else:
            o = l // 2
            wm = w_fourier_out[o].astype(BF16)
            if n == FFT_N:
                mix = _seq_fft(_chan_dft_perm(x, sh1, sc1, n1, cs_chan), fft_l)
            else:
                mix = _seq_dft(*seq_mats[n], *_chan_dft(x, sh1, sc1, n1, cs_chan, tm=tm), tm=256)
            if ctx_after:
                mix_c = _seq_dft(*seq_mats[L], *_chan_dft(xc, csh1, csc1, n1, cs_chan, tm=L), tm=L)
        wg = w_ffn_gate[l].astype(BF16)
        wu = w_ffn_up[l].astype(BF16)
        wd = w_ffn_down[l].astype(BF16)
        x = _tail(x, mix, g1, wm, sh2, sc2, g2, n2, wg, wu, wd, tm=tm,
                  final_g=final_g[None, :] if l == DEPTH - 1 else None)
        if ctx_after:
            xc = _tail(xc, mix_c, cg1, wm, csh2, csc2, cg2, n2, wg, wu, wd, tm=L)
    return x
```

```python
import functools

import numpy as np
import jax
import jax.numpy as jnp
from jax import lax
from jax.experimental import pallas as pl
from jax.experimental.pallas import tpu as pltpu

F32 = jnp.float32
BF16 = jnp.bfloat16

D_MODEL = 1024
DEPTH = 4
GRID_W = 64
HEAD_DIM = 64
N_Q_HEADS = 8
N_KV_HEADS = 2
ATTN_WIDTH = N_Q_HEADS * HEAD_DIM
KV_WIDTH = N_KV_HEADS * HEAD_DIM
CONV_WIDTH = D_MODEL - ATTN_WIDTH
WINDOW = 128
BLOCK = 128
ROPE_THETA = 10000.0
N_FOURIER_GROUPS = 4
FOURIER_GROUP = D_MODEL // N_FOURIER_GROUPS
D_FF = 2816
EPS = 1e-6
Q_START = 3 * CONV_WIDTH
K_START = Q_START + ATTN_WIDTH
V_START = K_START + KV_WIDTH

LANES = 128
SUBLANES = 8
KV_REP = 2 * KV_WIDTH
NEG_BIG = -1e30
COND_ROWS = 24
VMEM_LIMIT = 56 * 1024 * 1024


def _params(n_axes):
    return pltpu.CompilerParams(dimension_semantics=("arbitrary",) * n_axes,
                                vmem_limit_bytes=VMEM_LIMIT)


def _const_spec(shape):
    nd = len(shape)
    return pl.BlockSpec(shape, lambda *_: (0,) * nd, pipeline_mode=pl.Buffered(1))


def _layer_spec(shape, layer):
    nd = len(shape)
    return pl.BlockSpec((None,) + tuple(shape), lambda *_: (layer,) + (0,) * nd, pipeline_mode=pl.Buffered(1))


def _norm_mod(x, g, shift, scale):
    ms = jnp.mean(x * x, axis=-1, keepdims=True)
    y = x * lax.rsqrt(ms + EPS)
    return (y * g) * (1.0 + scale) + shift


def _silu(x):
    return x * (1.0 / (1.0 + jnp.exp(-x)))


def _adaln_kernel(c_ref, w_ref, b_ref, o_ref):
    a = _silu(c_ref[...])
    w = w_ref[0]
    a_hi = a.astype(BF16)
    a_lo = (a - a_hi.astype(F32)).astype(BF16)
    w_hi = w.astype(BF16)
    w_lo = (w - w_hi.astype(F32)).astype(BF16)
    acc = jnp.dot(a_hi, w_hi, preferred_element_type=F32)
    acc += jnp.dot(a_lo, w_hi, preferred_element_type=F32)
    acc += jnp.dot(a_hi, w_lo, preferred_element_type=F32)
    o_ref[0] = acc + b_ref[0]


def _adaln(cond, w_ada, b_ada):
    tn = 1024
    nt = (6 * D_MODEL) // tn
    return pl.pallas_call(
        _adaln_kernel,
        grid=(DEPTH, nt),
        in_specs=[
            pl.BlockSpec((COND_ROWS, D_MODEL), lambda l, j: (0, 0)),
            pl.BlockSpec((1, D_MODEL, tn), lambda l, j: (l, 0, j)),
            pl.BlockSpec((1, 1, tn), lambda l, j: (l, 0, j)),
        ],
        out_specs=pl.BlockSpec((1, COND_ROWS, tn), lambda l, j: (l, 0, j)),
        out_shape=jax.ShapeDtypeStruct((DEPTH, COND_ROWS, 6 * D_MODEL), F32),
        compiler_params=_params(2),
        name="adaln",
    )(cond, w_ada, b_ada.reshape(DEPTH, 1, 6 * D_MODEL))


IN_WIDTH = V_START + KV_WIDTH
LOG2E = float(np.log2(np.e))
Q_SCALE = HEAD_DIM ** -0.5 * LOG2E


def _inproj_kernel(x_ref, sh_ref, sc_ref, g_ref, w_ref, cos_ref, sa_ref, sb_ref,
                   bgu_ref, q_ref, kk_ref, vv_ref):
    h = _norm_mod(x_ref[0], g_ref[...], sh_ref[0], sc_ref[0]).astype(BF16)
    p = jnp.dot(h, w_ref[...], preferred_element_type=F32)
    bgu_ref[0, :, :CONV_WIDTH] = p[:, :CONV_WIDTH].astype(BF16)
    bgu_ref[0, :, CONV_WIDTH:] = (p[:, CONV_WIDTH:2 * CONV_WIDTH]
                                  * p[:, 2 * CONV_WIDTH:Q_START]).astype(BF16)
    cos = cos_ref[...]
    sa = sa_ref[...]
    sb = sb_ref[...]

    def rope(t):
        return t * cos + pltpu.roll(t, LANES - 16, 1) * sa + pltpu.roll(t, 16, 1) * sb

    for j in range(ATTN_WIDTH // LANES):
        lo = Q_START + j * LANES
        q_ref[0, :, j * LANES:(j + 1) * LANES] = (rope(p[:, lo:lo + LANES]) * Q_SCALE).astype(BF16)
    first_half = lax.broadcasted_iota(jnp.int32, (1, LANES), 1) < HEAD_DIM
    for t, ref in ((rope(p[:, K_START:V_START]), kk_ref), (p[:, V_START:], vv_ref)):
        swapped = pltpu.roll(t, HEAD_DIM, 1)
        ref[0, :, :LANES] = jnp.where(first_half, t, swapped).astype(BF16)
        ref[0, :, LANES:] = jnp.where(first_half, swapped, t).astype(BF16)


def _inproj(x, shift, scale, g, w_in, layer, cos, sa, sb, tm):
    b, n, _ = x.shape
    nt = n // tm
    tok = lambda w: pl.BlockSpec((1, tm, w), lambda i, bb: (bb, i, 0))
    mod = pl.BlockSpec((1, 1, D_MODEL), lambda i, bb: (bb, 0, 0))
    tab = pl.BlockSpec((tm, LANES), lambda i, bb: (i, 0))
    return pl.pallas_call(
        _inproj_kernel,
        grid=(nt, b),
        in_specs=[tok(D_MODEL), mod, mod, _const_spec((1, D_MODEL)),
                  _layer_spec((D_MODEL, IN_WIDTH), layer), tab, tab, tab],
        out_specs=[tok(D_MODEL), tok(ATTN_WIDTH), tok(KV_REP), tok(KV_REP)],
        out_shape=[jax.ShapeDtypeStruct((b, n, D_MODEL), BF16),
                   jax.ShapeDtypeStruct((b, n, ATTN_WIDTH), BF16),
                   jax.ShapeDtypeStruct((b, n, KV_REP), BF16),
                   jax.ShapeDtypeStruct((b, n, KV_REP), BF16)],
        compiler_params=_params(2),
        name="inproj",
    )(x, shift, scale, g, w_in, cos, sa, sb)


SM_ROWS = 16
SCORE_SLOTS = 2
MIX_ROWS = 1024


def _mixout_kernel(*refs, tile_rows, local):
    if local:
        (sink_ref, bgu_ref, bgu_p_ref, bgu_n_ref, q_ref,
         kk_ref, kk_p_ref, kk_n_ref, vv_ref, vv_p_ref, vv_n_ref, kkc_ref, vvc_ref,
         bias_mid_ref, bias_a_ref, bias_b_ref, cw_ref, o_ref, s_slots, e_slots) = refs
    else:
        (sink_ref, bgu_ref, bgu_p_ref, bgu_n_ref, q_ref,
         kkc_ref, vvc_ref, cw_ref, o_ref, s_slots, e_slots) = refs
    i = pl.program_id(0)
    n_tiles = pl.num_programs(0)
    r = tile_rows
    nb = r // BLOCK

    bg = bgu_ref[0, :, :CONV_WIDTH].astype(F32)
    u = bgu_ref[0, :, CONV_WIDTH:].astype(F32)
    has_prev = (i > 0).astype(F32)
    has_next = (i < n_tiles - 1).astype(F32)
    u_prev = bgu_p_ref[0, SUBLANES - 1:SUBLANES, CONV_WIDTH:].astype(F32) * has_prev
    u_next = bgu_n_ref[0, 0:1, CONV_WIDTH:].astype(F32) * has_next
    row = lax.broadcasted_iota(jnp.int32, (r, CONV_WIDTH), 0)
    u_up = jnp.where(row == 0, u_prev, pltpu.roll(u, 1, 0))
    u_dn = jnp.where(row == r - 1, u_next, pltpu.roll(u, r - 1, 0))
    cw = cw_ref[...]
    a_out = (bg * (u_up * cw[0:1] + u * cw[1:2] + u_dn * cw[2:3])).astype(BF16)

    lane = lax.broadcasted_iota(jnp.int32, (1, LANES), 1)
    first_half = lane < HEAD_DIM
    zero = jnp.zeros((), BF16)
    nt_dims = (((1,), (1,)), ((), ()))

    def split_k(t):
        return jnp.where(first_half, t, zero), jnp.where(first_half, zero, t)

    def split_v(t):
        va, vb = split_k(t)
        first = lax.broadcasted_iota(jnp.int32, t.shape, 1) < HEAD_DIM
        ones_a = jnp.where(first, 1.0, 0.0).astype(BF16)
        ones_b = jnp.where(first, 0.0, 1.0).astype(BF16)
        return jnp.concatenate([va, ones_a], axis=1), jnp.concatenate([vb, ones_b], axis=1)

    n_ctx = kkc_ref.shape[1]
    wc = 2 * n_ctx
    nl = 3 * BLOCK
    groups_per_pair = BLOCK // SM_ROWS
    b_cols = []
    for h in range(N_KV_HEADS):
        hs = slice(h * LANES, (h + 1) * LANES)
        kc_bd = jnp.concatenate(split_k(kkc_ref[0, :, hs]), axis=0)
        vc_ext = jnp.concatenate(split_v(vvc_ref[0, :, hs]), axis=0)
        if local:
            ka_all, kb_all = split_k(
                jnp.concatenate([kk_p_ref[0, :, hs], kk_ref[0, :, hs], kk_n_ref[0, :, hs]], axis=0))
            va_all, vb_all = split_v(
                jnp.concatenate([vv_p_ref[0, :, hs], vv_ref[0, :, hs], vv_n_ref[0, :, hs]], axis=0))
        sinks = [sink_ref[4 * h + i] * LOG2E for i in range(4)]
        outs = []
        for j in range(nb):
            rs = slice(j * BLOCK, (j + 1) * BLOCK)
            q2 = jnp.concatenate([q_ref[0, rs, 2 * h * LANES:(2 * h + 1) * LANES],
                                  q_ref[0, rs, (2 * h + 1) * LANES:(2 * h + 2) * LANES]], axis=0)
            slot = (h * nb + j) % SCORE_SLOTS
            s_ref = s_slots.at[slot]
            e_ref = e_slots.at[slot]
            s_ref[:, :wc] = lax.dot_general(q2, kc_bd, nt_dims, preferred_element_type=F32)
            cols_a = [slice(0, n_ctx)]
            cols_b = [slice(n_ctx, wc)]
            if local:
                ws = slice(j * BLOCK, j * BLOCK + nl)
                kl_bd = jnp.concatenate([ka_all[ws], kb_all[ws]], axis=0)
                if nb == 1:
                    bias = bias_a_ref[0] + bias_b_ref[0]
                elif j == 0:
                    bias = bias_a_ref[0]
                elif j == nb - 1:
                    bias = bias_b_ref[0]
                else:
                    bias = bias_mid_ref[0]
                s_ref[:, wc:] = lax.dot_general(q2, kl_bd, nt_dims, preferred_element_type=F32) + bias
                cols_a.append(slice(wc, wc + nl))
                cols_b.append(slice(wc + nl, wc + 2 * nl))
            sink_terms = []
            for g in range(2 * BLOCK // SM_ROWS):
                gr = slice(g * SM_ROWS, (g + 1) * SM_ROWS)
                pair = g // groups_per_pair
                terms = []
                for cols, sk in ((cols_a, sinks[2 * pair]), (cols_b, sinks[2 * pair + 1])):
                    ss = [s_ref[gr, c] for c in cols]
                    m = jnp.max(ss[0], axis=-1, keepdims=True)
                    for t in ss[1:]:
                        m = jnp.maximum(m, jnp.max(t, axis=-1, keepdims=True))
                    m = jnp.maximum(m, sk)
                    for c, t in zip(cols, ss):
                        e_ref[gr, c] = jnp.exp2(t - m).astype(BF16)
                    terms.append(jnp.exp2(sk - m))
                sink_terms.append(jnp.where(first_half, terms[0], terms[1]))
            o_ext = jnp.dot(e_ref[:, :wc], vc_ext, preferred_element_type=F32)
            if local:
                vl_ext = jnp.concatenate([va_all[ws], vb_all[ws]], axis=0)
                o_ext = o_ext + jnp.dot(e_ref[:, wc:], vl_ext, preferred_element_type=F32)
            o = o_ext[:, :LANES] / (o_ext[:, LANES:] + jnp.concatenate(sink_terms, axis=0))
            outs.append(jnp.concatenate([o[:BLOCK], o[BLOCK:]], axis=1))
        b_cols.append(jnp.concatenate(outs, axis=0) if nb > 1 else outs[0])
    o_ref[0, :, :CONV_WIDTH] = a_out
    o_ref[0, :, CONV_WIDTH:] = jnp.concatenate(b_cols, axis=1).astype(BF16)


def _band_bias():
    rr = np.arange(2 * BLOCK)[:, None] % BLOCK
    kidx = np.arange(6 * BLOCK)[None, :] % (3 * BLOCK)
    band = (kidx >= rr) & (kidx <= rr + 2 * WINDOW)
    mid = band
    first = band & (kidx >= BLOCK)
    last = band & (kidx < 2 * BLOCK)
    return jnp.asarray(np.where(np.stack([mid, first, last]), 0.0, NEG_BIG), dtype=F32)


def _mixout(bgu, q, kk, vv, kkc, vvc, sink, conv_w, tile_rows, local):
    b, n, _ = bgu.shape
    r = tile_rows
    nt = n // r
    nctx = kkc.shape[1]
    tok = lambda w: pl.BlockSpec((1, r, w), lambda i, bb: (bb, i, 0))
    r8 = r // SUBLANES
    halo8_p = pl.BlockSpec((1, SUBLANES, D_MODEL), lambda i, bb: (bb, jnp.maximum(i * r8 - 1, 0), 0))
    halo8_n = pl.BlockSpec((1, SUBLANES, D_MODEL),
                           lambda i, bb: (bb, jnp.minimum((i + 1) * r8, n // SUBLANES - 1), 0))
    ctx = pl.BlockSpec((1, nctx, KV_REP), lambda i, bb: (bb, 0, 0))
    smem = pl.BlockSpec(memory_space=pltpu.SMEM)
    in_specs = [smem, tok(D_MODEL), halo8_p, halo8_n, tok(ATTN_WIDTH)]
    args = [sink, bgu, bgu, bgu, q]
    if local:
        rb = r // BLOCK
        halo_p = pl.BlockSpec((1, BLOCK, KV_REP), lambda i, bb: (bb, jnp.maximum(i * rb - 1, 0), 0))
        halo_n = pl.BlockSpec((1, BLOCK, KV_REP),
                              lambda i, bb: (bb, jnp.minimum((i + 1) * rb, n // BLOCK - 1), 0))
        bias = _band_bias()
        bshape = (1, 2 * BLOCK, 6 * BLOCK)
        in_specs += [tok(KV_REP), halo_p, halo_n, tok(KV_REP), halo_p, halo_n, ctx, ctx,
                     pl.BlockSpec(bshape, lambda i, bb: (0, 0, 0)),
                     pl.BlockSpec(bshape, lambda i, bb: (jnp.where(i == 0, 1, 0), 0, 0)),
                     pl.BlockSpec(bshape, lambda i, bb: (jnp.where(i == nt - 1, 2, 0), 0, 0))]
        args += [kk, kk, kk, vv, vv, vv, kkc, vvc, bias, bias, bias]
    else:
        in_specs += [ctx, ctx]
        args += [kkc, vvc]
    in_specs += [_const_spec((3, CONV_WIDTH))]
    args += [conv_w]
    score_cols = 2 * nctx + (6 * BLOCK if local else 0)
    return pl.pallas_call(
        functools.partial(_mixout_kernel, tile_rows=r, local=local),
        grid=(nt, b),
        in_specs=in_specs,
        out_specs=tok(D_MODEL),
        out_shape=jax.ShapeDtypeStruct((b, n, D_MODEL), BF16),
        scratch_shapes=[pltpu.VMEM((SCORE_SLOTS, 2 * BLOCK, score_cols), F32),
                        pltpu.VMEM((SCORE_SLOTS, 2 * BLOCK, score_cols), BF16)],
        compiler_params=_params(2),
        name="mixout_local" if local else "mixout_ctx",
    )(*args)


def _chan_dft_kernel(x_ref, sh_ref, sc_ref, g_ref, cs_ref, y1_ref, y2_ref):
    h = _norm_mod(x_ref[0], g_ref[...], sh_ref[0], sc_ref[0]).astype(BF16)
    fg = FOURIER_GROUP
    for gi in range(N_FOURIER_GROUPS):
        yc = jnp.dot(h[:, gi * fg:(gi + 1) * fg], cs_ref[...], preferred_element_type=F32)
        y1_ref[0, :, gi * fg:(gi + 1) * fg] = yc[:, :fg].astype(BF16)
        y2_ref[0, :, gi * fg:(gi + 1) * fg] = yc[:, fg:].astype(BF16)


def _chan_dft(x, shift, scale, g, cs, tm):
    b, n, _ = x.shape
    tok = pl.BlockSpec((1, tm, D_MODEL), lambda i, bb: (bb, i, 0))
    mod = pl.BlockSpec((1, 1, D_MODEL), lambda i, bb: (bb, 0, 0))
    return pl.pallas_call(
        _chan_dft_kernel,
        grid=(n // tm, b),
        in_specs=[tok, mod, mod, _const_spec((1, D_MODEL)), _const_spec((FOURIER_GROUP, 2 * FOURIER_GROUP))],
        out_specs=[tok, tok],
        out_shape=[jax.ShapeDtypeStruct((b, n, D_MODEL), BF16)] * 2,
        compiler_params=_params(2),
        name="chan_dft",
    )(x, shift, scale, g, cs)


def _seq_dft_kernel(cn_ref, sn_ref, y1_ref, y2_ref, o_ref):
    z = jnp.dot(cn_ref[...], y1_ref[0], preferred_element_type=F32)
    z = z + jnp.dot(sn_ref[...], y2_ref[0], preferred_element_type=F32)
    o_ref[0] = z.astype(BF16)


def _seq_dft(cn, nsn, y1, y2, tm):
    b, n, _ = y1.shape
    tok = pl.BlockSpec((1, tm, D_MODEL), lambda bb, i: (bb, i, 0))
    mat = pl.BlockSpec((tm, n), lambda bb, i: (i, 0))
    full = pl.BlockSpec((1, n, D_MODEL), lambda bb, i: (bb, 0, 0), pipeline_mode=pl.Buffered(1))
    return pl.pallas_call(
        _seq_dft_kernel,
        grid=(b, n // tm),
        in_specs=[mat, mat, full, full],
        out_specs=tok,
        out_shape=jax.ShapeDtypeStruct((b, n, D_MODEL), BF16),
        compiler_params=_params(2),
        name="seq_dft",
    )(cn, nsn, y1, y2)


def _dft_mats(n, scale):
    j = jnp.arange(n, dtype=jnp.int32)
    m = (j[:, None] * j[None, :]) % n
    ang = m.astype(F32) * np.float32(2.0 * np.pi / n)
    return jnp.cos(ang) * scale, jnp.sin(ang) * scale


FFT_RADIX = 16
FFT_INNER = 256
FFT_N = FFT_RADIX * FFT_INNER
PERM_TILE = 512
PERM_B = PERM_TILE // FFT_RADIX


def _chan_dft_perm_kernel(x_ref, sh_ref, sc_ref, g_ref, p_ref, cs_ref, y_ref):
    fg = FOURIER_GROUP
    h = _norm_mod(x_ref[0], g_ref[...], sh_ref[0], sc_ref[0]).astype(BF16)
    hp = jnp.dot(p_ref[...], h, preferred_element_type=F32).astype(BF16)
    for gi in range(N_FOURIER_GROUPS):
        gs = slice(gi * fg, (gi + 1) * fg)
        yc = jnp.dot(hp[:, gs], cs_ref[...], preferred_element_type=F32)
        for a in range(FFT_RADIX):
            rs = slice(a * PERM_B, (a + 1) * PERM_B)
            y_ref[0, a, 0, :, gs] = yc[rs, :fg].astype(BF16)
            y_ref[0, a, 1, :, gs] = yc[rs, fg:].astype(BF16)


def _chan_dft_perm(x, shift, scale, g, cs):
    b, n, _ = x.shape
    r_out = np.arange(PERM_TILE)
    perm = np.zeros((PERM_TILE, PERM_TILE), np.float32)
    perm[r_out, FFT_RADIX * (r_out % PERM_B) + r_out // PERM_B] = 1.0
    tok = pl.BlockSpec((1, PERM_TILE, D_MODEL), lambda i, bb: (bb, i, 0))
    mod = pl.BlockSpec((1, 1, D_MODEL), lambda i, bb: (bb, 0, 0))
    return pl.pallas_call(
        _chan_dft_perm_kernel,
        grid=(n // PERM_TILE, b),
        in_specs=[tok, mod, mod, _const_spec((1, D_MODEL)), _const_spec((PERM_TILE, PERM_TILE)),
                  _const_spec((FOURIER_GROUP, 2 * FOURIER_GROUP))],
        out_specs=pl.BlockSpec((1, FFT_RADIX, 2, PERM_B, D_MODEL), lambda i, bb: (bb, 0, 0, i, 0)),
        out_shape=jax.ShapeDtypeStruct((b, FFT_RADIX, 2, n // FFT_RADIX, D_MODEL), BF16),
        compiler_params=_params(2),
        name="chan_dft_perm",
    )(x, shift, scale, g, jnp.asarray(perm, BF16), cs)


_C1 = float(np.cos(np.pi / 8))
_C2 = float(np.cos(np.pi / 4))
_C3 = float(np.cos(3 * np.pi / 8))


def _dft16_real(ur, ui):
    p = [ur[0]] + [ur[a] + ur[16 - a] for a in range(1, 8)] + [ur[8]]
    q = [p[a] + p[8 - a] for a in range(4)] + [p[4]]
    r = [p[a] - p[8 - a] for a in range(4)]
    s0, s1, s2 = q[0] + q[4], q[1] + q[3], q[2]
    t0, t1 = q[0] - q[4], q[1] - q[3]
    ca = [None] * 9
    ca[0] = s0 + s1 + s2
    ca[4] = s0 - s2
    ca[8] = s0 - s1 + s2
    ca[2] = t0 + _C2 * t1
    ca[6] = t0 - _C2 * t1
    e0, e1 = r[0] + _C2 * r[2], r[0] - _C2 * r[2]
    f, g = _C1 * r[1] + _C3 * r[3], _C3 * r[1] - _C1 * r[3]
    ca[1], ca[7], ca[3], ca[5] = e0 + f, e0 - f, e1 + g, e1 - g
    pp = [None] + [ui[a] - ui[16 - a] for a in range(1, 8)]
    qq = [None] + [pp[a] + pp[8 - a] for a in range(1, 4)] + [pp[4]]
    rr = [None] + [pp[a] - pp[8 - a] for a in range(1, 4)]
    sb = [None] * 8
    e0, e1 = _C2 * qq[2] + qq[4], _C2 * qq[2] - qq[4]
    f, g = _C3 * qq[1] + _C1 * qq[3], _C1 * qq[1] - _C3 * qq[3]
    sb[1], sb[7], sb[3], sb[5] = f + e0, f - e0, g + e1, g - e1
    w = _C2 * (rr[1] + rr[3])
    sb[2], sb[6], sb[4] = w + rr[2], w - rr[2], rr[1] - rr[3]
    out = [None] * 16
    out[0], out[8] = ca[0], ca[8]
    for c in range(1, 8):
        out[c] = ca[c] + sb[c]
        out[16 - c] = ca[c] - sb[c]
    return out


def _seq_fft_kernel(y_ref, l_ref, o_ref, u_even_ref, u_odd_ref):
    t = pl.program_id(0)

    @pl.when(t == 0)
    def _():
        u_odd_ref[...] = jnp.zeros(u_odd_ref.shape, F32)

    def step(cur_ref, prev_ref):
        rt = 2 * SUBLANES
        for i in range(FFT_INNER // rt):
            r0 = i * rt
            for lh in range(FOURIER_GROUP // LANES):
                ls = slice(lh * LANES, (lh + 1) * LANES)
                ur = [prev_ref[a, r0:r0 + rt, ls] for a in range(FFT_RADIX)]
                ui = [prev_ref[a, FFT_INNER + r0:FFT_INNER + r0 + rt, ls] for a in range(FFT_RADIX)]
                out = _dft16_real(ur, ui)
                for c in range(FFT_RADIX):
                    o_ref[0, c * FFT_INNER + r0:c * FFT_INNER + r0 + rt, ls] = out[c].astype(BF16)
        for a in range(FFT_RADIX):
            u = jnp.dot(l_ref[a, :, :FFT_INNER], y_ref[0, a, 0], preferred_element_type=F32)
            cur_ref[a] = u + jnp.dot(l_ref[a, :, FFT_INNER:], y_ref[0, a, 1], preferred_element_type=F32)

    pl.when(t % 2 == 0)(lambda: step(u_even_ref, u_odd_ref))
    pl.when(t % 2 == 1)(lambda: step(u_odd_ref, u_even_ref))


def _seq_fft(ycat, lmat):
    b = ycat.shape[0]
    ng = N_FOURIER_GROUPS
    items = b * ng

    def y_map(t):
        tt = jnp.minimum(t, items - 1)
        return (tt // ng, 0, 0, 0, tt % ng)

    def o_map(t):
        tp = jnp.maximum(t - 1, 0)
        return (tp // ng, 0, tp % ng)

    return pl.pallas_call(
        _seq_fft_kernel,
        grid=(items + 1,),
        in_specs=[pl.BlockSpec((1, FFT_RADIX, 2, FFT_INNER, FOURIER_GROUP), y_map),
                  _const_spec((FFT_RADIX, 2 * FFT_INNER, 2 * FFT_INNER))],
        out_specs=pl.BlockSpec((1, FFT_N, FOURIER_GROUP), o_map),
        out_shape=jax.ShapeDtypeStruct((b, FFT_N, D_MODEL), BF16),
        scratch_shapes=[pltpu.VMEM((FFT_RADIX, 2 * FFT_INNER, FOURIER_GROUP), F32)] * 2,
        compiler_params=_params(1),
        name="seq_fft",
    )(ycat, lmat)


def _fft_stage_mats():
    a = jnp.arange(FFT_RADIX, dtype=jnp.int32)[:, None, None]
    d = jnp.arange(FFT_INNER, dtype=jnp.int32)[None, :, None]
    bb = jnp.arange(FFT_INNER, dtype=jnp.int32)[None, None, :]
    m = (d * (a + FFT_RADIX * bb)) % FFT_N
    ang = m.astype(F32) * np.float32(2.0 * np.pi / FFT_N)
    c = jnp.cos(ang) * (FFT_N ** -0.5)
    s = jnp.sin(ang) * (FFT_N ** -0.5)
    top = jnp.concatenate([c, -s], axis=2)
    bot = jnp.concatenate([-s, -c], axis=2)
    return jnp.concatenate([top, bot], axis=1).astype(BF16)


FF_CHUNK = 256
TAIL_ROWS = 1024


def _tail_kernel(*refs, final):
    if final:
        (x_ref, m_ref, g1_ref, wm_ref, sh_ref, sc_ref, gate_ref, g_ref, wg_ref, wu_ref, wd_ref,
         fg_ref, o_ref, acc_ref) = refs
    else:
        (x_ref, m_ref, g1_ref, wm_ref, sh_ref, sc_ref, gate_ref, g_ref, wg_ref, wu_ref, wd_ref,
         o_ref, acc_ref) = refs
    x = x_ref[0] + g1_ref[0] * jnp.dot(m_ref[0], wm_ref[...], preferred_element_type=F32)
    h = _norm_mod(x, g_ref[...], sh_ref[0], sc_ref[0]).astype(BF16)
    for c in range(D_FF // FF_CHUNK):
        cs = slice(c * FF_CHUNK, (c + 1) * FF_CHUNK)
        gt = jnp.dot(h, wg_ref[:, cs], preferred_element_type=F32)
        up = jnp.dot(h, wu_ref[:, cs], preferred_element_type=F32)
        a = (_silu(gt) * up).astype(BF16)
        part = jnp.dot(a, wd_ref[cs, :], preferred_element_type=F32)
        if c == 0:
            acc_ref[...] = part
        else:
            acc_ref[...] += part
    y = x + gate_ref[0] * acc_ref[...]
    if final:
        ms = jnp.mean(y * y, axis=-1, keepdims=True)
        y = (y * lax.rsqrt(ms + EPS)) * fg_ref[...]
    o_ref[0] = y


def _tail(x, m, gate1, wm, wm_layer, shift, scale, gate, g, wg, wu, wd, layer, tm, final_g=None):
    b, n, _ = x.shape
    tok = pl.BlockSpec((1, tm, D_MODEL), lambda i, bb: (bb, i, 0))
    mod = pl.BlockSpec((1, 1, D_MODEL), lambda i, bb: (bb, 0, 0))
    in_specs = [tok, tok, mod, _layer_spec((D_MODEL, D_MODEL), wm_layer), mod, mod, mod, _const_spec((1, D_MODEL)),
                _layer_spec((D_MODEL, D_FF), layer), _layer_spec((D_MODEL, D_FF), layer),
                _layer_spec((D_FF, D_MODEL), layer)]
    args = [x, m, gate1, wm, shift, scale, gate, g, wg, wu, wd]
    if final_g is not None:
        in_specs.append(_const_spec((1, D_MODEL)))
        args.append(final_g)
    return pl.pallas_call(
        functools.partial(_tail_kernel, final=final_g is not None),
        grid=(n // tm, b),
        in_specs=in_specs,
        out_specs=tok,
        out_shape=jax.ShapeDtypeStruct((b, n, D_MODEL), F32),
        scratch_shapes=[pltpu.VMEM((tm, D_MODEL), F32)],
        compiler_params=_params(2),
        name="tail",
    )(*args)


def _rope_lane_tables(n):
    rows = n // GRID_W
    r = jnp.repeat(jnp.arange(rows), GRID_W).astype(F32)
    col = jnp.tile(jnp.arange(GRID_W), rows).astype(F32)
    quarter = HEAD_DIM // 4
    inv = ROPE_THETA ** (-jnp.arange(quarter, dtype=F32) / quarter)
    ang_r = r[:, None] * inv
    ang_c = col[:, None] * inv
    cr, sr, cc, sc = jnp.cos(ang_r), jnp.sin(ang_r), jnp.cos(ang_c), jnp.sin(ang_c)
    z = jnp.zeros_like(sr)
    cos = jnp.concatenate([cr, cr, cc, cc], axis=1)
    sa = jnp.concatenate([-sr, z, -sc, z], axis=1)
    sb = jnp.concatenate([z, sr, z, sc], axis=1)
    rep = lambda t: jnp.concatenate([t, t], axis=1)
    return rep(cos), rep(sa), rep(sb)


def _identity_lane_tables(n):
    return jnp.ones((n, LANES), F32), jnp.zeros((n, LANES), F32), jnp.zeros((n, LANES), F32)


def kernel(x, c, ctx, c_ctx, w_ada, b_ada, norm1_g, norm2_g, w_in, conv_w, sink,
           w_mix_out, w_fourier_out, w_ffn_gate, w_ffn_up, w_ffn_down, final_g):
    b, n, d = x.shape
    L = ctx.shape[1]
    nc = b * L
    tm = 512

    cond = jnp.concatenate([c, c_ctx[None, :], jnp.zeros((COND_ROWS - b - 1, d), F32)], axis=0)
    mods = _adaln(cond, w_ada, b_ada)

    def split(l, lo, hi):
        m = mods[l, lo:hi]
        return [m[:, None, k * d:(k + 1) * d] for k in range(6)]

    rope_lat = _rope_lane_tables(n)
    rope_ctx = _identity_lane_tables(nc)
    w_in_b, w_mix_b, w_fo_b = w_in.astype(BF16), w_mix_out.astype(BF16), w_fourier_out.astype(BF16)
    wg_b, wu_b, wd_b = w_ffn_gate.astype(BF16), w_ffn_up.astype(BF16), w_ffn_down.astype(BF16)
    cc, sc_ = _dft_mats(FOURIER_GROUP, FOURIER_GROUP ** -0.5)
    cs_chan = jnp.concatenate([cc, sc_], axis=1).astype(BF16)
    seq_mats = {}
    for m in {n, L} - {FFT_N}:
        cm, sm = _dft_mats(m, m ** -0.5)
        seq_mats[m] = (cm.astype(BF16), (-sm).astype(BF16))
    fft_l = _fft_stage_mats() if n == FFT_N else None

    xc = ctx.reshape(1, nc, d)
    per_seq = lambda t: t.reshape(b, L, t.shape[-1])
    for l in range(DEPTH):
        ctx_after = any(j % 2 == 0 for j in range(l + 1, DEPTH))
        ctx_here = (l % 2 == 0) or ctx_after
        sh1, sc1, g1, sh2, sc2, g2 = split(l, 0, b)
        if ctx_here:
            csh1, csc1, cg1, csh2, csc2, cg2 = split(l, b, b + 1)
        n1 = norm1_g[l][None, :]
        n2 = norm2_g[l][None, :]
        if l % 2 == 0:
            e = l // 2
            wm, wm_layer = w_mix_b, e
            bgu_c, q_c, kk_c, vv_c = _inproj(xc, csh1, csc1, n1, w_in_b, e, *rope_ctx, tm=min(nc, tm))
            kk_c, vv_c = per_seq(kk_c), per_seq(vv_c)
            bgu, q, kk, vv = _inproj(x, sh1, sc1, n1, w_in_b, e, *rope_lat, tm=tm)
            mix = _mixout(bgu, q, kk, vv, kk_c, vv_c, sink[e], conv_w[e], tile_rows=min(n, MIX_ROWS), local=True)
            if ctx_after:
                mix_c = _mixout(per_seq(bgu_c), per_seq(q_c), None, None, kk_c, vv_c, sink[e], conv_w[e],
                                tile_rows=L, local=False)
        else:
            wm, wm_layer = w_fo_b, l // 2
            if n == FFT_N:
                mix = _seq_fft(_chan_dft_perm(x, sh1, sc1, n1, cs_chan), fft_l)
            else:
                mix = _seq_dft(*seq_mats[n], *_chan_dft(x, sh1, sc1, n1, cs_chan, tm=tm), tm=256)
            if ctx_after:
                y1c, y2c = _chan_dft(xc, csh1, csc1, n1, cs_chan, tm=min(nc, tm))
                mix_c = _seq_dft(*seq_mats[L], per_seq(y1c), per_seq(y2c), tm=L)
        x = _tail(x, mix, g1, wm, wm_layer, sh2, sc2, g2, n2, wg_b, wu_b, wd_b, l, tm=min(n, TAIL_ROWS),
                  final_g=final_g[None, :] if l == DEPTH - 1 else None)
        if ctx_after:
            xc = _tail(xc, mix_c.reshape(1, nc, d), cg1, wm, wm_layer, csh2, csc2, cg2, n2, wg_b, wu_b, wd_b, l,
                       tm=min(nc, TAIL_ROWS))
    return x
```

```python
import functools

import numpy as np
import jax
import jax.numpy as jnp
from jax import lax
from jax.experimental import pallas as pl
from jax.experimental.pallas import tpu as pltpu

F32 = jnp.float32
BF16 = jnp.bfloat16

D_MODEL = 1024
DEPTH = 4
GRID_W = 64
HEAD_DIM = 64
N_Q_HEADS = 8
N_KV_HEADS = 2
ATTN_WIDTH = N_Q_HEADS * HEAD_DIM
KV_WIDTH = N_KV_HEADS * HEAD_DIM
CONV_WIDTH = D_MODEL - ATTN_WIDTH
WINDOW = 128
BLOCK = 128
ROPE_THETA = 10000.0
N_FOURIER_GROUPS = 4
FOURIER_GROUP = D_MODEL // N_FOURIER_GROUPS
D_FF = 2816
EPS = 1e-6
Q_START = 3 * CONV_WIDTH
K_START = Q_START + ATTN_WIDTH
V_START = K_START + KV_WIDTH

LANES = 128
SUBLANES = 8
KV_REP = 2 * KV_WIDTH
NEG_BIG = -1e30
COND_ROWS = 24
VMEM_LIMIT = 56 * 1024 * 1024


def _params(n_axes):
    return pltpu.CompilerParams(dimension_semantics=("arbitrary",) * n_axes,
                                vmem_limit_bytes=VMEM_LIMIT)


def _const_spec(shape):
    nd = len(shape)
    return pl.BlockSpec(shape, lambda *_: (0,) * nd, pipeline_mode=pl.Buffered(1))


def _layer_spec(shape, layer):
    nd = len(shape)
    return pl.BlockSpec((None,) + tuple(shape), lambda *_: (layer,) + (0,) * nd, pipeline_mode=pl.Buffered(1))


def _norm_mod(x, g, shift, scale):
    ms = jnp.mean(x * x, axis=-1, keepdims=True)
    y = x * lax.rsqrt(ms + EPS)
    return (y * g) * (1.0 + scale) + shift


def _silu(x):
    return x * (1.0 / (1.0 + jnp.exp(-x)))


def _adaln_kernel(c_ref, w_ref, b_ref, o_ref):
    a = _silu(c_ref[...])
    w = w_ref[0]
    a_hi = a.astype(BF16)
    a_lo = (a - a_hi.astype(F32)).astype(BF16)
    w_hi = w.astype(BF16)
    w_lo = (w - w_hi.astype(F32)).astype(BF16)
    acc = jnp.dot(a_hi, w_hi, preferred_element_type=F32)
    acc += jnp.dot(a_lo, w_hi, preferred_element_type=F32)
    acc += jnp.dot(a_hi, w_lo, preferred_element_type=F32)
    o_ref[0] = acc + b_ref[0]


def _adaln(cond, w_ada, b_ada):
    tn = 1024
    nt = (6 * D_MODEL) // tn
    return pl.pallas_call(
        _adaln_kernel,
        grid=(DEPTH, nt),
        in_specs=[
            pl.BlockSpec((COND_ROWS, D_MODEL), lambda l, j: (0, 0)),
            pl.BlockSpec((1, D_MODEL, tn), lambda l, j: (l, 0, j)),
            pl.BlockSpec((1, 1, tn), lambda l, j: (l, 0, j)),
        ],
        out_specs=pl.BlockSpec((1, COND_ROWS, tn), lambda l, j: (l, 0, j)),
        out_shape=jax.ShapeDtypeStruct((DEPTH, COND_ROWS, 6 * D_MODEL), F32),
        compiler_params=_params(2),
        name="adaln",
    )(cond, w_ada, b_ada.reshape(DEPTH, 1, 6 * D_MODEL))


IN_WIDTH = V_START + KV_WIDTH
LOG2E = float(np.log2(np.e))
Q_SCALE = HEAD_DIM ** -0.5 * LOG2E
INPROJ_ROWS = 1024
INPROJ_SUB = 512


def _inproj_kernel(x_ref, sh_ref, sc_ref, g_ref, w_ref, cos_ref, sa_ref, sb_ref,
                   bgu_ref, q_ref, kk_ref, vv_ref):
    first_half = lax.broadcasted_iota(jnp.int32, (1, LANES), 1) < HEAD_DIM
    tm = x_ref.shape[1]
    sub = min(tm, INPROJ_SUB)
    for s0 in range(0, tm, sub):
        rs = slice(s0, s0 + sub)
        h = _norm_mod(x_ref[0, rs, :], g_ref[...], sh_ref[0], sc_ref[0]).astype(BF16)
        p = jnp.dot(h, w_ref[...], preferred_element_type=F32)
        bgu_ref[0, rs, :CONV_WIDTH] = p[:, :CONV_WIDTH].astype(BF16)
        bgu_ref[0, rs, CONV_WIDTH:] = (p[:, CONV_WIDTH:2 * CONV_WIDTH]
                                       * p[:, 2 * CONV_WIDTH:Q_START]).astype(BF16)
        cos = cos_ref[rs, :]
        sa = sa_ref[rs, :]
        sb = sb_ref[rs, :]

        def rope(t):
            return t * cos + pltpu.roll(t, LANES - 16, 1) * sa + pltpu.roll(t, 16, 1) * sb

        for j in range(ATTN_WIDTH // LANES):
            lo = Q_START + j * LANES
            q_ref[0, rs, j * LANES:(j + 1) * LANES] = (rope(p[:, lo:lo + LANES]) * Q_SCALE).astype(BF16)
        for t, ref in ((rope(p[:, K_START:V_START]), kk_ref), (p[:, V_START:], vv_ref)):
            swapped = pltpu.roll(t, HEAD_DIM, 1)
            ref[0, rs, :LANES] = jnp.where(first_half, t, swapped).astype(BF16)
            ref[0, rs, LANES:] = jnp.where(first_half, swapped, t).astype(BF16)


def _inproj(x, shift, scale, g, w_in, layer, cos, sa, sb, tm):
    b, n, _ = x.shape
    nt = n // tm
    tok = lambda w: pl.BlockSpec((1, tm, w), lambda i, bb: (bb, i, 0))
    mod = pl.BlockSpec((1, 1, D_MODEL), lambda i, bb: (bb, 0, 0))
    tab = pl.BlockSpec((tm, LANES), lambda i, bb: (i, 0))
    return pl.pallas_call(
        _inproj_kernel,
        grid=(nt, b),
        in_specs=[tok(D_MODEL), mod, mod, _const_spec((1, D_MODEL)),
                  _layer_spec((D_MODEL, IN_WIDTH), layer), tab, tab, tab],
        out_specs=[tok(D_MODEL), tok(ATTN_WIDTH), tok(KV_REP), tok(KV_REP)],
        out_shape=[jax.ShapeDtypeStruct((b, n, D_MODEL), BF16),
                   jax.ShapeDtypeStruct((b, n, ATTN_WIDTH), BF16),
                   jax.ShapeDtypeStruct((b, n, KV_REP), BF16),
                   jax.ShapeDtypeStruct((b, n, KV_REP), BF16)],
        compiler_params=_params(2),
        name="inproj",
    )(x, shift, scale, g, w_in, cos, sa, sb)


SM_ROWS = 16
SCORE_SLOTS = 2
MIX_ROWS = 1024


def _mixout_kernel(*refs, tile_rows, local):
    if local:
        (sink_ref, bgu_ref, bgu_p_ref, bgu_n_ref, q_ref,
         kk_ref, kk_p_ref, kk_n_ref, vv_ref, vv_p_ref, vv_n_ref, kkc_ref, vvc_ref,
         bias_mid_ref, bias_a_ref, bias_b_ref, cw_ref, o_ref, s_slots, e_slots) = refs
    else:
        (sink_ref, bgu_ref, bgu_p_ref, bgu_n_ref, q_ref,
         kkc_ref, vvc_ref, cw_ref, o_ref, s_slots, e_slots) = refs
    i = pl.program_id(0)
    n_tiles = pl.num_programs(0)
    r = tile_rows
    nb = r // BLOCK

    bg = bgu_ref[0, :, :CONV_WIDTH].astype(F32)
    u = bgu_ref[0, :, CONV_WIDTH:].astype(F32)
    has_prev = (i > 0).astype(F32)
    has_next = (i < n_tiles - 1).astype(F32)
    u_prev = bgu_p_ref[0, SUBLANES - 1:SUBLANES, CONV_WIDTH:].astype(F32) * has_prev
    u_next = bgu_n_ref[0, 0:1, CONV_WIDTH:].astype(F32) * has_next
    row = lax.broadcasted_iota(jnp.int32, (r, CONV_WIDTH), 0)
    u_up = jnp.where(row == 0, u_prev, pltpu.roll(u, 1, 0))
    u_dn = jnp.where(row == r - 1, u_next, pltpu.roll(u, r - 1, 0))
    cw = cw_ref[...]
    a_out = (bg * (u_up * cw[0:1] + u * cw[1:2] + u_dn * cw[2:3])).astype(BF16)

    lane = lax.broadcasted_iota(jnp.int32, (1, LANES), 1)
    first_half = lane < HEAD_DIM
    zero = jnp.zeros((), BF16)
    nt_dims = (((1,), (1,)), ((), ()))

    def split_k(t):
        return jnp.where(first_half, t, zero), jnp.where(first_half, zero, t)

    def split_v(t):
        va, vb = split_k(t)
        first = lax.broadcasted_iota(jnp.int32, t.shape, 1) < HEAD_DIM
        ones_a = jnp.where(first, 1.0, 0.0).astype(BF16)
        ones_b = jnp.where(first, 0.0, 1.0).astype(BF16)
        return jnp.concatenate([va, ones_a], axis=1), jnp.concatenate([vb, ones_b], axis=1)

    n_ctx = kkc_ref.shape[1]
    wc = 2 * n_ctx
    nl = 3 * BLOCK
    groups_per_pair = BLOCK // SM_ROWS
    b_cols = []
    for h in range(N_KV_HEADS):
        hs = slice(h * LANES, (h + 1) * LANES)
        kc_bd = jnp.concatenate(split_k(kkc_ref[0, :, hs]), axis=0)
        vc_ext = jnp.concatenate(split_v(vvc_ref[0, :, hs]), axis=0)
        if local:
            ka_all, kb_all = split_k(
                jnp.concatenate([kk_p_ref[0, :, hs], kk_ref[0, :, hs], kk_n_ref[0, :, hs]], axis=0))
            va_all, vb_all = split_v(
                jnp.concatenate([vv_p_ref[0, :, hs], vv_ref[0, :, hs], vv_n_ref[0, :, hs]], axis=0))
        sinks = [sink_ref[4 * h + i] * LOG2E for i in range(4)]
        outs = []
        for j in range(nb):
            rs = slice(j * BLOCK, (j + 1) * BLOCK)
            q2 = jnp.concatenate([q_ref[0, rs, 2 * h * LANES:(2 * h + 1) * LANES],
                                  q_ref[0, rs, (2 * h + 1) * LANES:(2 * h + 2) * LANES]], axis=0)
            slot = (h * nb + j) % SCORE_SLOTS
            s_ref = s_slots.at[slot]
            e_ref = e_slots.at[slot]
            s_ref[:, :wc] = lax.dot_general(q2, kc_bd, nt_dims, preferred_element_type=F32)
            cols_a = [slice(0, n_ctx)]
            cols_b = [slice(n_ctx, wc)]
            if local:
                ws = slice(j * BLOCK, j * BLOCK + nl)
                kl_bd = jnp.concatenate([ka_all[ws], kb_all[ws]], axis=0)
                if nb == 1:
                    bias = bias_a_ref[0] + bias_b_ref[0]
                elif j == 0:
                    bias = bias_a_ref[0]
                elif j == nb - 1:
                    bias = bias_b_ref[0]
                else:
                    bias = bias_mid_ref[0]
                s_ref[:, wc:] = lax.dot_general(q2, kl_bd, nt_dims, preferred_element_type=F32) + bias
                cols_a.append(slice(wc, wc + nl))
                cols_b.append(slice(wc + nl, wc + 2 * nl))
            sink_terms = []
            for g in range(2 * BLOCK // SM_ROWS):
                gr = slice(g * SM_ROWS, (g + 1) * SM_ROWS)
                pair = g // groups_per_pair
                terms = []
                for cols, sk in ((cols_a, sinks[2 * pair]), (cols_b, sinks[2 * pair + 1])):
                    ss = [s_ref[gr, c] for c in cols]
                    m = jnp.max(ss[0], axis=-1, keepdims=True)
                    for t in ss[1:]:
                        m = jnp.maximum(m, jnp.max(t, axis=-1, keepdims=True))
                    m = jnp.maximum(m, sk)
                    for c, t in zip(cols, ss):
                        e_ref[gr, c] = jnp.exp2(t - m).astype(BF16)
                    terms.append(jnp.exp2(sk - m))
                sink_terms.append(jnp.where(first_half, terms[0], terms[1]))
            o_ext = jnp.dot(e_ref[:, :wc], vc_ext, preferred_element_type=F32)
            if local:
                vl_ext = jnp.concatenate([va_all[ws], vb_all[ws]], axis=0)
                o_ext = o_ext + jnp.dot(e_ref[:, wc:], vl_ext, preferred_element_type=F32)
            o = o_ext[:, :LANES] / (o_ext[:, LANES:] + jnp.concatenate(sink_terms, axis=0))
            outs.append(jnp.concatenate([o[:BLOCK], o[BLOCK:]], axis=1))
        b_cols.append(jnp.concatenate(outs, axis=0) if nb > 1 else outs[0])
    o_ref[0, :, :CONV_WIDTH] = a_out
    o_ref[0, :, CONV_WIDTH:] = jnp.concatenate(b_cols, axis=1).astype(BF16)


def _band_bias():
    rr = np.arange(2 * BLOCK)[:, None] % BLOCK
    kidx = np.arange(6 * BLOCK)[None, :] % (3 * BLOCK)
    band = (kidx >= rr) & (kidx <= rr + 2 * WINDOW)
    mid = band
    first = band & (kidx >= BLOCK)
    last = band & (kidx < 2 * BLOCK)
    return jnp.asarray(np.where(np.stack([mid, first, last]), 0.0, NEG_BIG), dtype=F32)


def _mixout(bgu, q, kk, vv, kkc, vvc, sink, conv_w, tile_rows, local):
    b, n, _ = bgu.shape
    r = tile_rows
    nt = n // r
    nctx = kkc.shape[1]
    tok = lambda w: pl.BlockSpec((1, r, w), lambda i, bb: (bb, i, 0))
    r8 = r // SUBLANES
    halo8_p = pl.BlockSpec((1, SUBLANES, D_MODEL), lambda i, bb: (bb, jnp.maximum(i * r8 - 1, 0), 0))
    halo8_n = pl.BlockSpec((1, SUBLANES, D_MODEL),
                           lambda i, bb: (bb, jnp.minimum((i + 1) * r8, n // SUBLANES - 1), 0))
    ctx = pl.BlockSpec((1, nctx, KV_REP), lambda i, bb: (bb, 0, 0))
    smem = pl.BlockSpec(memory_space=pltpu.SMEM)
    in_specs = [smem, tok(D_MODEL), halo8_p, halo8_n, tok(ATTN_WIDTH)]
    args = [sink, bgu, bgu, bgu, q]
    if local:
        rb = r // BLOCK
        halo_p = pl.BlockSpec((1, BLOCK, KV_REP), lambda i, bb: (bb, jnp.maximum(i * rb - 1, 0), 0))
        halo_n = pl.BlockSpec((1, BLOCK, KV_REP),
                              lambda i, bb: (bb, jnp.minimum((i + 1) * rb, n // BLOCK - 1), 0))
        bias = _band_bias()
        bshape = (1, 2 * BLOCK, 6 * BLOCK)
        in_specs += [tok(KV_REP), halo_p, halo_n, tok(KV_REP), halo_p, halo_n, ctx, ctx,
                     pl.BlockSpec(bshape, lambda i, bb: (0, 0, 0)),
                     pl.BlockSpec(bshape, lambda i, bb: (jnp.where(i == 0, 1, 0), 0, 0)),
                     pl.BlockSpec(bshape, lambda i, bb: (jnp.where(i == nt - 1, 2, 0), 0, 0))]
        args += [kk, kk, kk, vv, vv, vv, kkc, vvc, bias, bias, bias]
    else:
        in_specs += [ctx, ctx]
        args += [kkc, vvc]
    in_specs += [_const_spec((3, CONV_WIDTH))]
    args += [conv_w]
    score_cols = 2 * nctx + (6 * BLOCK if local else 0)
    return pl.pallas_call(
        functools.partial(_mixout_kernel, tile_rows=r, local=local),
        grid=(nt, b),
        in_specs=in_specs,
        out_specs=tok(D_MODEL),
        out_shape=jax.ShapeDtypeStruct((b, n, D_MODEL), BF16),
        scratch_shapes=[pltpu.VMEM((SCORE_SLOTS, 2 * BLOCK, score_cols), F32),
                        pltpu.VMEM((SCORE_SLOTS, 2 * BLOCK, score_cols), BF16)],
        compiler_params=_params(2),
        name="mixout_local" if local else "mixout_ctx",
    )(*args)


def _chan_dft_kernel(x_ref, sh_ref, sc_ref, g_ref, cs_ref, y1_ref, y2_ref):
    h = _norm_mod(x_ref[0], g_ref[...], sh_ref[0], sc_ref[0]).astype(BF16)
    fg = FOURIER_GROUP
    for gi in range(N_FOURIER_GROUPS):
        yc = jnp.dot(h[:, gi * fg:(gi + 1) * fg], cs_ref[...], preferred_element_type=F32)
        y1_ref[0, :, gi * fg:(gi + 1) * fg] = yc[:, :fg].astype(BF16)
        y2_ref[0, :, gi * fg:(gi + 1) * fg] = yc[:, fg:].astype(BF16)


def _chan_dft(x, shift, scale, g, cs, tm):
    b, n, _ = x.shape
    tok = pl.BlockSpec((1, tm, D_MODEL), lambda i, bb: (bb, i, 0))
    mod = pl.BlockSpec((1, 1, D_MODEL), lambda i, bb: (bb, 0, 0))
    return pl.pallas_call(
        _chan_dft_kernel,
        grid=(n // tm, b),
        in_specs=[tok, mod, mod, _const_spec((1, D_MODEL)), _const_spec((FOURIER_GROUP, 2 * FOURIER_GROUP))],
        out_specs=[tok, tok],
        out_shape=[jax.ShapeDtypeStruct((b, n, D_MODEL), BF16)] * 2,
        compiler_params=_params(2),
        name="chan_dft",
    )(x, shift, scale, g, cs)


def _seq_dft_kernel(cn_ref, sn_ref, y1_ref, y2_ref, o_ref):
    z = jnp.dot(cn_ref[...], y1_ref[0], preferred_element_type=F32)
    z = z + jnp.dot(sn_ref[...], y2_ref[0], preferred_element_type=F32)
    o_ref[0] = z.astype(BF16)


def _seq_dft(cn, nsn, y1, y2, tm):
    b, n, _ = y1.shape
    tok = pl.BlockSpec((1, tm, D_MODEL), lambda bb, i: (bb, i, 0))
    mat = pl.BlockSpec((tm, n), lambda bb, i: (i, 0))
    full = pl.BlockSpec((1, n, D_MODEL), lambda bb, i: (bb, 0, 0), pipeline_mode=pl.Buffered(1))
    return pl.pallas_call(
        _seq_dft_kernel,
        grid=(b, n // tm),
        in_specs=[mat, mat, full, full],
        out_specs=tok,
        out_shape=jax.ShapeDtypeStruct((b, n, D_MODEL), BF16),
        compiler_params=_params(2),
        name="seq_dft",
    )(cn, nsn, y1, y2)


def _dft_mats(n, scale):
    j = jnp.arange(n, dtype=jnp.int32)
    m = (j[:, None] * j[None, :]) % n
    ang = m.astype(F32) * np.float32(2.0 * np.pi / n)
    return jnp.cos(ang) * scale, jnp.sin(ang) * scale


FFT_RADIX = 16
FFT_INNER = 256
FFT_N = FFT_RADIX * FFT_INNER
PERM_TILE = 512
PERM_B = PERM_TILE // FFT_RADIX
PERM_STEP = 2 * PERM_TILE


def _chan_dft_perm_kernel(x_ref, sh_ref, sc_ref, g_ref, p_ref, cs_ref, y_ref):
    fg = FOURIER_GROUP
    for t in range(x_ref.shape[1] // PERM_TILE):
        h = _norm_mod(x_ref[0, t * PERM_TILE:(t + 1) * PERM_TILE, :], g_ref[...], sh_ref[0], sc_ref[0]).astype(BF16)
        hp = jnp.dot(p_ref[...], h, preferred_element_type=F32).astype(BF16)
        for gi in range(N_FOURIER_GROUPS):
            gs = slice(gi * fg, (gi + 1) * fg)
            yc = jnp.dot(hp[:, gs], cs_ref[...], preferred_element_type=F32)
            for a in range(FFT_RADIX):
                rs = slice(a * PERM_B, (a + 1) * PERM_B)
                ws = slice(t * PERM_B, (t + 1) * PERM_B)
                y_ref[0, a, 0, ws, gs] = yc[rs, :fg].astype(BF16)
                y_ref[0, a, 1, ws, gs] = yc[rs, fg:].astype(BF16)


def _chan_dft_perm(x, shift, scale, g, cs):
    b, n, _ = x.shape
    r_out = np.arange(PERM_TILE)
    perm = np.zeros((PERM_TILE, PERM_TILE), np.float32)
    perm[r_out, FFT_RADIX * (r_out % PERM_B) + r_out // PERM_B] = 1.0
    tok = pl.BlockSpec((1, PERM_STEP, D_MODEL), lambda i, bb: (bb, i, 0))
    mod = pl.BlockSpec((1, 1, D_MODEL), lambda i, bb: (bb, 0, 0))
    return pl.pallas_call(
        _chan_dft_perm_kernel,
        grid=(n // PERM_STEP, b),
        in_specs=[tok, mod, mod, _const_spec((1, D_MODEL)), _const_spec((PERM_TILE, PERM_TILE)),
                  _const_spec((FOURIER_GROUP, 2 * FOURIER_GROUP))],
        out_specs=pl.BlockSpec((1, FFT_RADIX, 2, PERM_STEP // FFT_RADIX, D_MODEL), lambda i, bb: (bb, 0, 0, i, 0)),
        out_shape=jax.ShapeDtypeStruct((b, FFT_RADIX, 2, n // FFT_RADIX, D_MODEL), BF16),
        compiler_params=_params(2),
        name="chan_dft_perm",
    )(x, shift, scale, g, jnp.asarray(perm, BF16), cs)


_C1 = float(np.cos(np.pi / 8))
_C2 = float(np.cos(np.pi / 4))
_C3 = float(np.cos(3 * np.pi / 8))


def _dft16_real(ur, ui):
    p = [ur[0]] + [ur[a] + ur[16 - a] for a in range(1, 8)] + [ur[8]]
    q = [p[a] + p[8 - a] for a in range(4)] + [p[4]]
    r = [p[a] - p[8 - a] for a in range(4)]
    s0, s1, s2 = q[0] + q[4], q[1] + q[3], q[2]
    t0, t1 = q[0] - q[4], q[1] - q[3]
    ca = [None] * 9
    ca[0] = s0 + s1 + s2
    ca[4] = s0 - s2
    ca[8] = s0 - s1 + s2
    ca[2] = t0 + _C2 * t1
    ca[6] = t0 - _C2 * t1
    e0, e1 = r[0] + _C2 * r[2], r[0] - _C2 * r[2]
    f, g = _C1 * r[1] + _C3 * r[3], _C3 * r[1] - _C1 * r[3]
    ca[1], ca[7], ca[3], ca[5] = e0 + f, e0 - f, e1 + g, e1 - g
    pp = [None] + [ui[a] - ui[16 - a] for a in range(1, 8)]
    qq = [None] + [pp[a] + pp[8 - a] for a in range(1, 4)] + [pp[4]]
    rr = [None] + [pp[a] - pp[8 - a] for a in range(1, 4)]
    sb = [None] * 8
    e0, e1 = _C2 * qq[2] + qq[4], _C2 * qq[2] - qq[4]
    f, g = _C3 * qq[1] + _C1 * qq[3], _C1 * qq[1] - _C3 * qq[3]
    sb[1], sb[7], sb[3], sb[5] = f + e0, f - e0, g + e1, g - e1
    w = _C2 * (rr[1] + rr[3])
    sb[2], sb[6], sb[4] = w + rr[2], w - rr[2], rr[1] - rr[3]
    out = [None] * 16
    out[0], out[8] = ca[0], ca[8]
    for c in range(1, 8):
        out[c] = ca[c] + sb[c]
        out[16 - c] = ca[c] - sb[c]
    return out


def _seq_fft_kernel(y_ref, l_ref, o_ref, u_even_ref, u_odd_ref):
    t = pl.program_id(0)

    @pl.when(t == 0)
    def _():
        u_odd_ref[...] = jnp.zeros(u_odd_ref.shape, F32)

    def step(cur_ref, prev_ref):
        rt = 2 * SUBLANES
        for i in range(FFT_INNER // rt):
            r0 = i * rt
            for lh in range(FOURIER_GROUP // LANES):
                ls = slice(lh * LANES, (lh + 1) * LANES)
                ur = [prev_ref[a, r0:r0 + rt, ls] for a in range(FFT_RADIX)]
                ui = [prev_ref[a, FFT_INNER + r0:FFT_INNER + r0 + rt, ls] for a in range(FFT_RADIX)]
                out = _dft16_real(ur, ui)
                for c in range(FFT_RADIX):
                    o_ref[0, c * FFT_INNER + r0:c * FFT_INNER + r0 + rt, ls] = out[c].astype(BF16)
        for a in range(FFT_RADIX):
            u = jnp.dot(l_ref[a, :, :FFT_INNER], y_ref[0, a, 0], preferred_element_type=F32)
            cur_ref[a] = u + jnp.dot(l_ref[a, :, FFT_INNER:], y_ref[0, a, 1], preferred_element_type=F32)

    pl.when(t % 2 == 0)(lambda: step(u_even_ref, u_odd_ref))
    pl.when(t % 2 == 1)(lambda: step(u_odd_ref, u_even_ref))


def _seq_fft(ycat, lmat):
    b = ycat.shape[0]
    ng = N_FOURIER_GROUPS
    items = b * ng

    def y_map(t):
        tt = jnp.minimum(t, items - 1)
        return (tt // ng, 0, 0, 0, tt % ng)

    def o_map(t):
        tp = jnp.maximum(t - 1, 0)
        return (tp // ng, 0, tp % ng)

    return pl.pallas_call(
        _seq_fft_kernel,
        grid=(items + 1,),
        in_specs=[pl.BlockSpec((1, FFT_RADIX, 2, FFT_INNER, FOURIER_GROUP), y_map),
                  _const_spec((FFT_RADIX, 2 * FFT_INNER, 2 * FFT_INNER))],
        out_specs=pl.BlockSpec((1, FFT_N, FOURIER_GROUP), o_map),
        out_shape=jax.ShapeDtypeStruct((b, FFT_N, D_MODEL), BF16),
        scratch_shapes=[pltpu.VMEM((FFT_RADIX, 2 * FFT_INNER, FOURIER_GROUP), F32)] * 2,
        compiler_params=_params(1),
        name="seq_fft",
    )(ycat, lmat)


def _fft_stage_mats():
    a = jnp.arange(FFT_RADIX, dtype=jnp.int32)[:, None, None]
    d = jnp.arange(FFT_INNER, dtype=jnp.int32)[None, :, None]
    bb = jnp.arange(FFT_INNER, dtype=jnp.int32)[None, None, :]
    m = (d * (a + FFT_RADIX * bb)) % FFT_N
    ang = m.astype(F32) * np.float32(2.0 * np.pi / FFT_N)
    c = jnp.cos(ang) * (FFT_N ** -0.5)
    s = jnp.sin(ang) * (FFT_N ** -0.5)
    top = jnp.concatenate([c, -s], axis=2)
    bot = jnp.concatenate([-s, -c], axis=2)
    return jnp.concatenate([top, bot], axis=1).astype(BF16)


FF_CHUNK = 256
TAIL_ROWS = 1024


def _tail_kernel(*refs, final):
    if final:
        (x_ref, m_ref, g1_ref, wm_ref, sh_ref, sc_ref, gate_ref, g_ref, wg_ref, wu_ref, wd_ref,
         fg_ref, o_ref, acc_ref) = refs
    else:
        (x_ref, m_ref, g1_ref, wm_ref, sh_ref, sc_ref, gate_ref, g_ref, wg_ref, wu_ref, wd_ref,
         o_ref, acc_ref) = refs
    x = x_ref[0] + g1_ref[0] * jnp.dot(m_ref[0], wm_ref[...], preferred_element_type=F32)
    h = _norm_mod(x, g_ref[...], sh_ref[0], sc_ref[0]).astype(BF16)
    for c in range(D_FF // FF_CHUNK):
        cs = slice(c * FF_CHUNK, (c + 1) * FF_CHUNK)
        gt = jnp.dot(h, wg_ref[:, cs], preferred_element_type=F32)
        up = jnp.dot(h, wu_ref[:, cs], preferred_element_type=F32)
        a = (_silu(gt) * up).astype(BF16)
        part = jnp.dot(a, wd_ref[cs, :], preferred_element_type=F32)
        if c == 0:
            acc_ref[...] = part
        else:
            acc_ref[...] += part
    y = x + gate_ref[0] * acc_ref[...]
    if final:
        ms = jnp.mean(y * y, axis=-1, keepdims=True)
        y = (y * lax.rsqrt(ms + EPS)) * fg_ref[...]
    o_ref[0] = y


def _tail(x, m, gate1, wm, wm_layer, shift, scale, gate, g, wg, wu, wd, layer, tm, final_g=None):
    b, n, _ = x.shape
    tok = pl.BlockSpec((1, tm, D_MODEL), lambda i, bb: (bb, i, 0))
    mod = pl.BlockSpec((1, 1, D_MODEL), lambda i, bb: (bb, 0, 0))
    in_specs = [tok, tok, mod, _layer_spec((D_MODEL, D_MODEL), wm_layer), mod, mod, mod, _const_spec((1, D_MODEL)),
                _layer_spec((D_MODEL, D_FF), layer), _layer_spec((D_MODEL, D_FF), layer),
                _layer_spec((D_FF, D_MODEL), layer)]
    args = [x, m, gate1, wm, shift, scale, gate, g, wg, wu, wd]
    if final_g is not None:
        in_specs.append(_const_spec((1, D_MODEL)))
        args.append(final_g)
    return pl.pallas_call(
        functools.partial(_tail_kernel, final=final_g is not None),
        grid=(n // tm, b),
        in_specs=in_specs,
        out_specs=tok,
        out_shape=jax.ShapeDtypeStruct((b, n, D_MODEL), F32),
        scratch_shapes=[pltpu.VMEM((tm, D_MODEL), F32)],
        compiler_params=_params(2),
        name="tail",
    )(*args)


def _rope_lane_tables(n):
    rows = n // GRID_W
    r = jnp.repeat(jnp.arange(rows), GRID_W).astype(F32)
    col = jnp.tile(jnp.arange(GRID_W), rows).astype(F32)
    quarter = HEAD_DIM // 4
    inv = ROPE_THETA ** (-jnp.arange(quarter, dtype=F32) / quarter)
    ang_r = r[:, None] * inv
    ang_c = col[:, None] * inv
    cr, sr, cc, sc = jnp.cos(ang_r), jnp.sin(ang_r), jnp.cos(ang_c), jnp.sin(ang_c)
    z = jnp.zeros_like(sr)
    cos = jnp.concatenate([cr, cr, cc, cc], axis=1)
    sa = jnp.concatenate([-sr, z, -sc, z], axis=1)
    sb = jnp.concatenate([z, sr, z, sc], axis=1)
    rep = lambda t: jnp.concatenate([t, t], axis=1)
    return rep(cos), rep(sa), rep(sb)


def _identity_lane_tables(n):
    return jnp.ones((n, LANES), F32), jnp.zeros((n, LANES), F32), jnp.zeros((n, LANES), F32)


def kernel(x, c, ctx, c_ctx, w_ada, b_ada, norm1_g, norm2_g, w_in, conv_w, sink,
           w_mix_out, w_fourier_out, w_ffn_gate, w_ffn_up, w_ffn_down, final_g):
    b, n, d = x.shape
    L = ctx.shape[1]
    nc = b * L
    tm = 512

    cond = jnp.concatenate([c, c_ctx[None, :], jnp.zeros((COND_ROWS - b - 1, d), F32)], axis=0)
    mods = _adaln(cond, w_ada, b_ada)

    def split(l, lo, hi):
        m = mods[l, lo:hi]
        return [m[:, None, k * d:(k + 1) * d] for k in range(6)]

    rope_lat = _rope_lane_tables(n)
    rope_ctx = _identity_lane_tables(nc)
    w_in_b, w_mix_b, w_fo_b = w_in.astype(BF16), w_mix_out.astype(BF16), w_fourier_out.astype(BF16)
    wg_b, wu_b, wd_b = w_ffn_gate.astype(BF16), w_ffn_up.astype(BF16), w_ffn_down.astype(BF16)
    cc, sc_ = _dft_mats(FOURIER_GROUP, FOURIER_GROUP ** -0.5)
    cs_chan = jnp.concatenate([cc, sc_], axis=1).astype(BF16)
    seq_mats = {}
    for m in {n, L} - {FFT_N}:
        cm, sm = _dft_mats(m, m ** -0.5)
        seq_mats[m] = (cm.astype(BF16), (-sm).astype(BF16))
    fft_l = _fft_stage_mats() if n == FFT_N else None

    xc = ctx.reshape(1, nc, d)
    per_seq = lambda t: t.reshape(b, L, t.shape[-1])
    for l in range(DEPTH):
        ctx_after = any(j % 2 == 0 for j in range(l + 1, DEPTH))
        ctx_here = (l % 2 == 0) or ctx_after
        sh1, sc1, g1, sh2, sc2, g2 = split(l, 0, b)
        if ctx_here:
            csh1, csc1, cg1, csh2, csc2, cg2 = split(l, b, b + 1)
        n1 = norm1_g[l][None, :]
        n2 = norm2_g[l][None, :]
        if l % 2 == 0:
            e = l // 2
            wm, wm_layer = w_mix_b, e
            bgu_c, q_c, kk_c, vv_c = _inproj(xc, csh1, csc1, n1, w_in_b, e, *rope_ctx, tm=min(nc, INPROJ_ROWS))
            kk_c, vv_c = per_seq(kk_c), per_seq(vv_c)
            bgu, q, kk, vv = _inproj(x, sh1, sc1, n1, w_in_b, e, *rope_lat, tm=min(n, INPROJ_ROWS))
            mix = _mixout(bgu, q, kk, vv, kk_c, vv_c, sink[e], conv_w[e], tile_rows=min(n, MIX_ROWS), local=True)
            if ctx_after:
                mix_c = _mixout(per_seq(bgu_c), per_seq(q_c), None, None, kk_c, vv_c, sink[e], conv_w[e],
                                tile_rows=L, local=False)
        else:
            wm, wm_layer = w_fo_b, l // 2
            if n == FFT_N:
                mix = _seq_fft(_chan_dft_perm(x, sh1, sc1, n1, cs_chan), fft_l)
            else:
                mix = _seq_dft(*seq_mats[n], *_chan_dft(x, sh1, sc1, n1, cs_chan, tm=tm), tm=256)
            if ctx_after:
                y1c, y2c = _chan_dft(xc, csh1, csc1, n1, cs_chan, tm=min(nc, tm))
                mix_c = _seq_dft(*seq_mats[L], per_seq(y1c), per_seq(y2c), tm=L)
        x = _tail(x, mix, g1, wm, wm_layer, sh2, sc2, g2, n2, wg_b, wu_b, wd_b, l, tm=min(n, TAIL_ROWS),
                  final_g=final_g[None, :] if l == DEPTH - 1 else None)
        if ctx_after:
            xc = _tail(xc, mix_c.reshape(1, nc, d), cg1, wm, wm_layer, csh2, csc2, cg2, n2, wg_b, wu_b, wd_b, l,
                       tm=min(nc, TAIL_ROWS))
    return x
```

```python
import functools

import numpy as np
import jax
import jax.numpy as jnp
from jax import lax
from jax.experimental import pallas as pl
from jax.experimental.pallas import tpu as pltpu

F32 = jnp.float32
BF16 = jnp.bfloat16

D_MODEL = 1024
DEPTH = 4
GRID_W = 64
HEAD_DIM = 64
N_Q_HEADS = 8
N_KV_HEADS = 2
ATTN_WIDTH = N_Q_HEADS * HEAD_DIM
KV_WIDTH = N_KV_HEADS * HEAD_DIM
CONV_WIDTH = D_MODEL - ATTN_WIDTH
WINDOW = 128
BLOCK = 128
ROPE_THETA = 10000.0
N_FOURIER_GROUPS = 4
FOURIER_GROUP = D_MODEL // N_FOURIER_GROUPS
D_FF = 2816
EPS = 1e-6
Q_START = 3 * CONV_WIDTH
K_START = Q_START + ATTN_WIDTH
V_START = K_START + KV_WIDTH

LANES = 128
SUBLANES = 8
KV_REP = 2 * KV_WIDTH
NEG_BIG = -1e30
COND_ROWS = 24
VMEM_LIMIT = 56 * 1024 * 1024


def _params(n_axes):
    return pltpu.CompilerParams(dimension_semantics=("arbitrary",) * n_axes,
                                vmem_limit_bytes=VMEM_LIMIT)


def _const_spec(shape):
    nd = len(shape)
    return pl.BlockSpec(shape, lambda *_: (0,) * nd, pipeline_mode=pl.Buffered(1))


def _layer_spec(shape, layer):
    nd = len(shape)
    return pl.BlockSpec((None,) + tuple(shape), lambda *_: (layer,) + (0,) * nd, pipeline_mode=pl.Buffered(1))


def _norm_mod(x, g, shift, scale):
    ms = jnp.mean(x * x, axis=-1, keepdims=True)
    y = x * lax.rsqrt(ms + EPS)
    return (y * g) * (1.0 + scale) + shift


def _silu(x):
    return x * (1.0 / (1.0 + jnp.exp(-x)))


def _adaln_kernel(c_ref, w_ref, b_ref, o_ref):
    a = _silu(c_ref[...])
    w = w_ref[0]
    a_hi = a.astype(BF16)
    a_lo = (a - a_hi.astype(F32)).astype(BF16)
    w_hi = w.astype(BF16)
    w_lo = (w - w_hi.astype(F32)).astype(BF16)
    acc = jnp.dot(a_hi, w_hi, preferred_element_type=F32)
    acc += jnp.dot(a_lo, w_hi, preferred_element_type=F32)
    acc += jnp.dot(a_hi, w_lo, preferred_element_type=F32)
    o_ref[0] = acc + b_ref[0]


def _adaln(cond, w_ada, b_ada):
    tn = 1024
    nt = (6 * D_MODEL) // tn
    return pl.pallas_call(
        _adaln_kernel,
        grid=(DEPTH, nt),
        in_specs=[
            pl.BlockSpec((COND_ROWS, D_MODEL), lambda l, j: (0, 0)),
            pl.BlockSpec((1, D_MODEL, tn), lambda l, j: (l, 0, j)),
            pl.BlockSpec((1, 1, tn), lambda l, j: (l, 0, j)),
        ],
        out_specs=pl.BlockSpec((1, COND_ROWS, tn), lambda l, j: (l, 0, j)),
        out_shape=jax.ShapeDtypeStruct((DEPTH, COND_ROWS, 6 * D_MODEL), F32),
        compiler_params=_params(2),
        name="adaln",
    )(cond, w_ada, b_ada.reshape(DEPTH, 1, 6 * D_MODEL))


IN_WIDTH = V_START + KV_WIDTH
LOG2E = float(np.log2(np.e))
Q_SCALE = HEAD_DIM ** -0.5 * LOG2E
INPROJ_ROWS = 1024
INPROJ_SUB = 512


def _inproj_kernel(x_ref, sh_ref, sc_ref, g_ref, w_ref, cos_ref, sa_ref, sb_ref,
                   bgu_ref, q_ref, kk_ref, vv_ref):
    first_half = lax.broadcasted_iota(jnp.int32, (1, LANES), 1) < HEAD_DIM
    tm = x_ref.shape[1]
    sub = min(tm, INPROJ_SUB)
    for s0 in range(0, tm, sub):
        rs = slice(s0, s0 + sub)
        h = _norm_mod(x_ref[0, rs, :], g_ref[...], sh_ref[0], sc_ref[0]).astype(BF16)
        p = jnp.dot(h, w_ref[...], preferred_element_type=F32)
        bgu_ref[0, rs, :CONV_WIDTH] = p[:, :CONV_WIDTH].astype(BF16)
        bgu_ref[0, rs, CONV_WIDTH:] = (p[:, CONV_WIDTH:2 * CONV_WIDTH]
                                       * p[:, 2 * CONV_WIDTH:Q_START]).astype(BF16)
        cos = cos_ref[rs, :]
        sa = sa_ref[rs, :]
        sb = sb_ref[rs, :]

        def rope(t):
            return t * cos + pltpu.roll(t, LANES - 16, 1) * sa + pltpu.roll(t, 16, 1) * sb

        for j in range(ATTN_WIDTH // LANES):
            lo = Q_START + j * LANES
            q_ref[0, rs, j * LANES:(j + 1) * LANES] = (rope(p[:, lo:lo + LANES]) * Q_SCALE).astype(BF16)
        for t, ref in ((rope(p[:, K_START:V_START]), kk_ref), (p[:, V_START:], vv_ref)):
            swapped = pltpu.roll(t, HEAD_DIM, 1)
            ref[0, rs, :LANES] = jnp.where(first_half, t, swapped).astype(BF16)
            ref[0, rs, LANES:] = jnp.where(first_half, swapped, t).astype(BF16)


def _inproj(x, shift, scale, g, w_in, layer, cos, sa, sb, tm):
    b, n, _ = x.shape
    nt = n // tm
    tok = lambda w: pl.BlockSpec((1, tm, w), lambda i, bb: (bb, i, 0))
    mod = pl.BlockSpec((1, 1, D_MODEL), lambda i, bb: (bb, 0, 0))
    tab = pl.BlockSpec((tm, LANES), lambda i, bb: (i, 0))
    return pl.pallas_call(
        _inproj_kernel,
        grid=(nt, b),
        in_specs=[tok(D_MODEL), mod, mod, _const_spec((1, D_MODEL)),
                  _layer_spec((D_MODEL, IN_WIDTH), layer), tab, tab, tab],
        out_specs=[tok(D_MODEL), tok(ATTN_WIDTH), tok(KV_REP), tok(KV_REP)],
        out_shape=[jax.ShapeDtypeStruct((b, n, D_MODEL), BF16),
                   jax.ShapeDtypeStruct((b, n, ATTN_WIDTH), BF16),
                   jax.ShapeDtypeStruct((b, n, KV_REP), BF16),
                   jax.ShapeDtypeStruct((b, n, KV_REP), BF16)],
        compiler_params=_params(2),
        name="inproj",
    )(x, shift, scale, g, w_in, cos, sa, sb)


CONV_ROWS = 32
SM_ROWS = 16
SCORE_SLOTS = 2
MIX_ROWS = 1024


def _mixout_kernel(*refs, tile_rows, local):
    if local:
        (sink_ref, bgu_ref, bgu_p_ref, bgu_n_ref, q_ref,
         kk_ref, kk_p_ref, kk_n_ref, vv_ref, vv_p_ref, vv_n_ref, kkc_ref, vvc_ref,
         bias_mid_ref, bias_a_ref, bias_b_ref, cw_ref, o_ref, s_slots, e_slots) = refs
    else:
        (sink_ref, bgu_ref, bgu_p_ref, bgu_n_ref, q_ref,
         kkc_ref, vvc_ref, cw_ref, o_ref, s_slots, e_slots) = refs
    i = pl.program_id(0)
    n_tiles = pl.num_programs(0)
    r = tile_rows
    nb = r // BLOCK

    has_prev = (i > 0).astype(F32)
    has_next = (i < n_tiles - 1).astype(F32)
    cw = cw_ref[...]
    cr = CONV_ROWS
    pk = 2 * SUBLANES
    row = lax.broadcasted_iota(jnp.int32, (cr, CONV_WIDTH), 0)

    def conv_chunk(r0):
        bg = bgu_ref[0, r0:r0 + cr, :CONV_WIDTH].astype(F32)
        u = bgu_ref[0, r0:r0 + cr, CONV_WIDTH:].astype(F32)
        if r0 == 0:
            u_prev = bgu_p_ref[0, SUBLANES - 1:SUBLANES, CONV_WIDTH:].astype(F32) * has_prev
        else:
            u_prev = bgu_ref[0, r0 - pk:r0, CONV_WIDTH:].astype(F32)[pk - 1:pk]
        if r0 + cr == r:
            u_next = bgu_n_ref[0, 0:1, CONV_WIDTH:].astype(F32) * has_next
        else:
            u_next = bgu_ref[0, r0 + cr:r0 + cr + pk, CONV_WIDTH:].astype(F32)[0:1]
        u_up = jnp.where(row == 0, u_prev, pltpu.roll(u, 1, 0))
        u_dn = jnp.where(row == cr - 1, u_next, pltpu.roll(u, cr - 1, 0))
        o_ref[0, r0:r0 + cr, :CONV_WIDTH] = (bg * (u_up * cw[0:1] + u * cw[1:2] + u_dn * cw[2:3])).astype(BF16)

    lane = lax.broadcasted_iota(jnp.int32, (1, LANES), 1)
    first_half = lane < HEAD_DIM
    zero = jnp.zeros((), BF16)
    nt_dims = (((1,), (1,)), ((), ()))

    def split_k(t):
        return jnp.where(first_half, t, zero), jnp.where(first_half, zero, t)

    def split_v(t):
        va, vb = split_k(t)
        first = lax.broadcasted_iota(jnp.int32, t.shape, 1) < HEAD_DIM
        ones_a = jnp.where(first, 1.0, 0.0).astype(BF16)
        ones_b = jnp.where(first, 0.0, 1.0).astype(BF16)
        return jnp.concatenate([va, ones_a], axis=1), jnp.concatenate([vb, ones_b], axis=1)

    n_ctx = kkc_ref.shape[1]
    wc = 2 * n_ctx
    nl = 3 * BLOCK
    groups_per_pair = BLOCK // SM_ROWS
    cols_a = [slice(0, n_ctx)] + ([slice(wc, wc + nl)] if local else [])
    cols_b = [slice(n_ctx, wc)] + ([slice(wc + nl, wc + 2 * nl)] if local else [])

    heads = []
    for h in range(N_KV_HEADS):
        hs = slice(h * LANES, (h + 1) * LANES)
        hd = dict(kc_bd=jnp.concatenate(split_k(kkc_ref[0, :, hs]), axis=0),
                  vc_ext=jnp.concatenate(split_v(vvc_ref[0, :, hs]), axis=0),
                  sinks=[sink_ref[4 * h + i] * LOG2E for i in range(4)])
        if local:
            hd["k"] = split_k(jnp.concatenate([kk_p_ref[0, :, hs], kk_ref[0, :, hs], kk_n_ref[0, :, hs]], axis=0))
            hd["v"] = split_v(jnp.concatenate([vv_p_ref[0, :, hs], vv_ref[0, :, hs], vv_n_ref[0, :, hs]], axis=0))
        heads.append(hd)

    blocks = [(h, j) for h in range(N_KV_HEADS) for j in range(nb)]

    def scores(idx):
        h, j = blocks[idx]
        hd = heads[h]
        rs = slice(j * BLOCK, (j + 1) * BLOCK)
        q2 = jnp.concatenate([q_ref[0, rs, 2 * h * LANES:(2 * h + 1) * LANES],
                              q_ref[0, rs, (2 * h + 1) * LANES:(2 * h + 2) * LANES]], axis=0)
        s_ref = s_slots.at[idx % SCORE_SLOTS]
        s_ref[:, :wc] = lax.dot_general(q2, hd["kc_bd"], nt_dims, preferred_element_type=F32)
        if local:
            ws = slice(j * BLOCK, j * BLOCK + nl)
            kl_bd = jnp.concatenate([hd["k"][0][ws], hd["k"][1][ws]], axis=0)
            if nb == 1:
                bias = bias_a_ref[0] + bias_b_ref[0]
            elif j == 0:
                bias = bias_a_ref[0]
            elif j == nb - 1:
                bias = bias_b_ref[0]
            else:
                bias = bias_mid_ref[0]
            s_ref[:, wc:] = lax.dot_general(q2, kl_bd, nt_dims, preferred_element_type=F32) + bias

    def softmax_pv(idx):
        h, j = blocks[idx]
        hd = heads[h]
        s_ref = s_slots.at[idx % SCORE_SLOTS]
        e_ref = e_slots.at[idx % SCORE_SLOTS]
        sink_terms = []
        for g in range(2 * BLOCK // SM_ROWS):
            gr = slice(g * SM_ROWS, (g + 1) * SM_ROWS)
            pair = g // groups_per_pair
            terms = []
            for cols, sk in ((cols_a, hd["sinks"][2 * pair]), (cols_b, hd["sinks"][2 * pair + 1])):
                ss = [s_ref[gr, c] for c in cols]
                m = jnp.max(ss[0], axis=-1, keepdims=True)
                for t in ss[1:]:
                    m = jnp.maximum(m, jnp.max(t, axis=-1, keepdims=True))
                m = jnp.maximum(m, sk)
                for c, t in zip(cols, ss):
                    e_ref[gr, c] = jnp.exp2(t - m).astype(BF16)
                terms.append(jnp.exp2(sk - m))
            sink_terms.append(jnp.where(first_half, terms[0], terms[1]))
        o_ext = jnp.dot(e_ref[:, :wc], hd["vc_ext"], preferred_element_type=F32)
        if local:
            ws = slice(j * BLOCK, j * BLOCK + nl)
            vl_ext = jnp.concatenate([hd["v"][0][ws], hd["v"][1][ws]], axis=0)
            o_ext = o_ext + jnp.dot(e_ref[:, wc:], vl_ext, preferred_element_type=F32)
        o = o_ext[:, :LANES] / (o_ext[:, LANES:] + jnp.concatenate(sink_terms, axis=0))
        return jnp.concatenate([o[:BLOCK], o[BLOCK:]], axis=1)

    conv_starts = list(range(0, r, cr))
    per_block = -(-len(conv_starts) // len(blocks))
    outs = {}
    scores(0)
    for idx in range(len(blocks)):
        if idx + 1 < len(blocks):
            scores(idx + 1)
        for r0 in conv_starts[idx * per_block:(idx + 1) * per_block]:
            conv_chunk(r0)
        outs[blocks[idx]] = softmax_pv(idx)
    b_cols = [jnp.concatenate([outs[(h, j)] for j in range(nb)], axis=0) for h in range(N_KV_HEADS)]
    o_ref[0, :, CONV_WIDTH:] = jnp.concatenate(b_cols, axis=1).astype(BF16)


def _band_bias():
    rr = np.arange(2 * BLOCK)[:, None] % BLOCK
    kidx = np.arange(6 * BLOCK)[None, :] % (3 * BLOCK)
    band = (kidx >= rr) & (kidx <= rr + 2 * WINDOW)
    mid = band
    first = band & (kidx >= BLOCK)
    last = band & (kidx < 2 * BLOCK)
    return jnp.asarray(np.where(np.stack([mid, first, last]), 0.0, NEG_BIG), dtype=F32)


def _mixout(bgu, q, kk, vv, kkc, vvc, sink, conv_w, tile_rows, local):
    b, n, _ = bgu.shape
    r = tile_rows
    nt = n // r
    nctx = kkc.shape[1]
    tok = lambda w: pl.BlockSpec((1, r, w), lambda i, bb: (bb, i, 0))
    r8 = r // SUBLANES
    halo8_p = pl.BlockSpec((1, SUBLANES, D_MODEL), lambda i, bb: (bb, jnp.maximum(i * r8 - 1, 0), 0))
    halo8_n = pl.BlockSpec((1, SUBLANES, D_MODEL),
                           lambda i, bb: (bb, jnp.minimum((i + 1) * r8, n // SUBLANES - 1), 0))
    ctx = pl.BlockSpec((1, nctx, KV_REP), lambda i, bb: (bb, 0, 0))
    smem = pl.BlockSpec(memory_space=pltpu.SMEM)
    in_specs = [smem, tok(D_MODEL), halo8_p, halo8_n, tok(ATTN_WIDTH)]
    args = [sink, bgu, bgu, bgu, q]
    if local:
        rb = r // BLOCK
        halo_p = pl.BlockSpec((1, BLOCK, KV_REP), lambda i, bb: (bb, jnp.maximum(i * rb - 1, 0), 0))
        halo_n = pl.BlockSpec((1, BLOCK, KV_REP),
                              lambda i, bb: (bb, jnp.minimum((i + 1) * rb, n // BLOCK - 1), 0))
        bias = _band_bias()
        bshape = (1, 2 * BLOCK, 6 * BLOCK)
        in_specs += [tok(KV_REP), halo_p, halo_n, tok(KV_REP), halo_p, halo_n, ctx, ctx,
                     pl.BlockSpec(bshape, lambda i, bb: (0, 0, 0)),
                     pl.BlockSpec(bshape, lambda i, bb: (jnp.where(i == 0, 1, 0), 0, 0)),
                     pl.BlockSpec(bshape, lambda i, bb: (jnp.where(i == nt - 1, 2, 0), 0, 0))]
        args += [kk, kk, kk, vv, vv, vv, kkc, vvc, bias, bias, bias]
    else:
        in_specs += [ctx, ctx]
        args += [kkc, vvc]
    in_specs += [_const_spec((3, CONV_WIDTH))]
    args += [conv_w]
    score_cols = 2 * nctx + (6 * BLOCK if local else 0)
    return pl.pallas_call(
        functools.partial(_mixout_kernel, tile_rows=r, local=local),
        grid=(nt, b),
        in_specs=in_specs,
        out_specs=tok(D_MODEL),
        out_shape=jax.ShapeDtypeStruct((b, n, D_MODEL), BF16),
        scratch_shapes=[pltpu.VMEM((SCORE_SLOTS, 2 * BLOCK, score_cols), F32),
                        pltpu.VMEM((SCORE_SLOTS, 2 * BLOCK, score_cols), BF16)],
        compiler_params=_params(2),
        name="mixout_local" if local else "mixout_ctx",
    )(*args)


def _chan_dft_kernel(x_ref, sh_ref, sc_ref, g_ref, cs_ref, y1_ref, y2_ref):
    h = _norm_mod(x_ref[0], g_ref[...], sh_ref[0], sc_ref[0]).astype(BF16)
    fg = FOURIER_GROUP
    for gi in range(N_FOURIER_GROUPS):
        yc = jnp.dot(h[:, gi * fg:(gi + 1) * fg], cs_ref[...], preferred_element_type=F32)
        y1_ref[0, :, gi * fg:(gi + 1) * fg] = yc[:, :fg].astype(BF16)
        y2_ref[0, :, gi * fg:(gi + 1) * fg] = yc[:, fg:].astype(BF16)


def _chan_dft(x, shift, scale, g, cs, tm):
    b, n, _ = x.shape
    tok = pl.BlockSpec((1, tm, D_MODEL), lambda i, bb: (bb, i, 0))
    mod = pl.BlockSpec((1, 1, D_MODEL), lambda i, bb: (bb, 0, 0))
    return pl.pallas_call(
        _chan_dft_kernel,
        grid=(n // tm, b),
        in_specs=[tok, mod, mod, _const_spec((1, D_MODEL)), _const_spec((FOURIER_GROUP, 2 * FOURIER_GROUP))],
        out_specs=[tok, tok],
        out_shape=[jax.ShapeDtypeStruct((b, n, D_MODEL), BF16)] * 2,
        compiler_params=_params(2),
        name="chan_dft",
    )(x, shift, scale, g, cs)


def _seq_dft_kernel(cn_ref, sn_ref, y1_ref, y2_ref, o_ref):
    z = jnp.dot(cn_ref[...], y1_ref[0], preferred_element_type=F32)
    z = z + jnp.dot(sn_ref[...], y2_ref[0], preferred_element_type=F32)
    o_ref[0] = z.astype(BF16)


def _seq_dft(cn, nsn, y1, y2, tm):
    b, n, _ = y1.shape
    tok = pl.BlockSpec((1, tm, D_MODEL), lambda bb, i: (bb, i, 0))
    mat = pl.BlockSpec((tm, n), lambda bb, i: (i, 0))
    full = pl.BlockSpec((1, n, D_MODEL), lambda bb, i: (bb, 0, 0), pipeline_mode=pl.Buffered(1))
    return pl.pallas_call(
        _seq_dft_kernel,
        grid=(b, n // tm),
        in_specs=[mat, mat, full, full],
        out_specs=tok,
        out_shape=jax.ShapeDtypeStruct((b, n, D_MODEL), BF16),
        compiler_params=_params(2),
        name="seq_dft",
    )(cn, nsn, y1, y2)


def _dft_mats(n, scale):
    j = jnp.arange(n, dtype=jnp.int32)
    m = (j[:, None] * j[None, :]) % n
    ang = m.astype(F32) * np.float32(2.0 * np.pi / n)
    return jnp.cos(ang) * scale, jnp.sin(ang) * scale


FFT_RADIX = 16
FFT_INNER = 256
FFT_N = FFT_RADIX * FFT_INNER
PERM_TILE = 512
PERM_B = PERM_TILE // FFT_RADIX
PERM_STEP = 2 * PERM_TILE


def _chan_dft_perm_kernel(x_ref, sh_ref, sc_ref, g_ref, p_ref, cs_ref, y_ref):
    fg = FOURIER_GROUP
    for t in range(x_ref.shape[1] // PERM_TILE):
        h = _norm_mod(x_ref[0, t * PERM_TILE:(t + 1) * PERM_TILE, :], g_ref[...], sh_ref[0], sc_ref[0]).astype(BF16)
        hp = jnp.dot(p_ref[...], h, preferred_element_type=F32).astype(BF16)
        for gi in range(N_FOURIER_GROUPS):
            gs = slice(gi * fg, (gi + 1) * fg)
            yc = jnp.dot(hp[:, gs], cs_ref[...], preferred_element_type=F32)
            for a in range(FFT_RADIX):
                rs = slice(a * PERM_B, (a + 1) * PERM_B)
                ws = slice(t * PERM_B, (t + 1) * PERM_B)
                y_ref[0, a, 0, ws, gs] = yc[rs, :fg].astype(BF16)
                y_ref[0, a, 1, ws, gs] = yc[rs, fg:].astype(BF16)


def _chan_dft_perm(x, shift, scale, g, cs):
    b, n, _ = x.shape
    r_out = np.arange(PERM_TILE)
    perm = np.zeros((PERM_TILE, PERM_TILE), np.float32)
    perm[r_out, FFT_RADIX * (r_out % PERM_B) + r_out // PERM_B] = 1.0
    tok = pl.BlockSpec((1, PERM_STEP, D_MODEL), lambda i, bb: (bb, i, 0))
    mod = pl.BlockSpec((1, 1, D_MODEL), lambda i, bb: (bb, 0, 0))
    return pl.pallas_call(
        _chan_dft_perm_kernel,
        grid=(n // PERM_STEP, b),
        in_specs=[tok, mod, mod, _const_spec((1, D_MODEL)), _const_spec((PERM_TILE, PERM_TILE)),
                  _const_spec((FOURIER_GROUP, 2 * FOURIER_GROUP))],
        out_specs=pl.BlockSpec((1, FFT_RADIX, 2, PERM_STEP // FFT_RADIX, D_MODEL), lambda i, bb: (bb, 0, 0, i, 0)),
        out_shape=jax.ShapeDtypeStruct((b, FFT_RADIX, 2, n // FFT_RADIX, D_MODEL), BF16),
        compiler_params=_params(2),
        name="chan_dft_perm",
    )(x, shift, scale, g, jnp.asarray(perm, BF16), cs)


_C1 = float(np.cos(np.pi / 8))
_C2 = float(np.cos(np.pi / 4))
_C3 = float(np.cos(3 * np.pi / 8))


def _dft16_real(ur, ui):
    p = [ur[0]] + [ur[a] + ur[16 - a] for a in range(1, 8)] + [ur[8]]
    q = [p[a] + p[8 - a] for a in range(4)] + [p[4]]
    r = [p[a] - p[8 - a] for a in range(4)]
    s0, s1, s2 = q[0] + q[4], q[1] + q[3], q[2]
    t0, t1 = q[0] - q[4], q[1] - q[3]
    ca = [None] * 9
    ca[0] = s0 + s1 + s2
    ca[4] = s0 - s2
    ca[8] = s0 - s1 + s2
    ca[2] = t0 + _C2 * t1
    ca[6] = t0 - _C2 * t1
    e0, e1 = r[0] + _C2 * r[2], r[0] - _C2 * r[2]
    f, g = _C1 * r[1] + _C3 * r[3], _C3 * r[1] - _C1 * r[3]
    ca[1], ca[7], ca[3], ca[5] = e0 + f, e0 - f, e1 + g, e1 - g
    pp = [None] + [ui[a] - ui[16 - a] for a in range(1, 8)]
    qq = [None] + [pp[a] + pp[8 - a] for a in range(1, 4)] + [pp[4]]
    rr = [None] + [pp[a] - pp[8 - a] for a in range(1, 4)]
    sb = [None] * 8
    e0, e1 = _C2 * qq[2] + qq[4], _C2 * qq[2] - qq[4]
    f, g = _C3 * qq[1] + _C1 * qq[3], _C1 * qq[1] - _C3 * qq[3]
    sb[1], sb[7], sb[3], sb[5] = f + e0, f - e0, g + e1, g - e1
    w = _C2 * (rr[1] + rr[3])
    sb[2], sb[6], sb[4] = w + rr[2], w - rr[2], rr[1] - rr[3]
    out = [None] * 16
    out[0], out[8] = ca[0], ca[8]
    for c in range(1, 8):
        out[c] = ca[c] + sb[c]
        out[16 - c] = ca[c] - sb[c]
    return out


def _seq_fft_kernel(y_ref, l_ref, o_ref, u_even_ref, u_odd_ref):
    t = pl.program_id(0)

    @pl.when(t == 0)
    def _():
        u_odd_ref[...] = jnp.zeros(u_odd_ref.shape, F32)

    def step(cur_ref, prev_ref):
        rt = 2 * SUBLANES
        for i in range(FFT_INNER // rt):
            r0 = i * rt
            for lh in range(FOURIER_GROUP // LANES):
                ls = slice(lh * LANES, (lh + 1) * LANES)
                ur = [prev_ref[a, r0:r0 + rt, ls] for a in range(FFT_RADIX)]
                ui = [prev_ref[a, FFT_INNER + r0:FFT_INNER + r0 + rt, ls] for a in range(FFT_RADIX)]
                out = _dft16_real(ur, ui)
                for c in range(FFT_RADIX):
                    o_ref[0, c * FFT_INNER + r0:c * FFT_INNER + r0 + rt, ls] = out[c].astype(BF16)
        for a in range(FFT_RADIX):
            u = jnp.dot(l_ref[a, :, :FFT_INNER], y_ref[0, a, 0], preferred_element_type=F32)
            cur_ref[a] = u + jnp.dot(l_ref[a, :, FFT_INNER:], y_ref[0, a, 1], preferred_element_type=F32)

    pl.when(t % 2 == 0)(lambda: step(u_even_ref, u_odd_ref))
    pl.when(t % 2 == 1)(lambda: step(u_odd_ref, u_even_ref))


def _seq_fft(ycat, lmat):
    b = ycat.shape[0]
    ng = N_FOURIER_GROUPS
    items = b * ng

    def y_map(t):
        tt = jnp.minimum(t, items - 1)
        return (tt // ng, 0, 0, 0, tt % ng)

    def o_map(t):
        tp = jnp.maximum(t - 1, 0)
        return (tp // ng, 0, tp % ng)

    return pl.pallas_call(
        _seq_fft_kernel,
        grid=(items + 1,),
        in_specs=[pl.BlockSpec((1, FFT_RADIX, 2, FFT_INNER, FOURIER_GROUP), y_map),
                  _const_spec((FFT_RADIX, 2 * FFT_INNER, 2 * FFT_INNER))],
        out_specs=pl.BlockSpec((1, FFT_N, FOURIER_GROUP), o_map),
        out_shape=jax.ShapeDtypeStruct((b, FFT_N, D_MODEL), BF16),
        scratch_shapes=[pltpu.VMEM((FFT_RADIX, 2 * FFT_INNER, FOURIER_GROUP), F32)] * 2,
        compiler_params=_params(1),
        name="seq_fft",
    )(ycat, lmat)


def _fft_stage_mats():
    a = jnp.arange(FFT_RADIX, dtype=jnp.int32)[:, None, None]
    d = jnp.arange(FFT_INNER, dtype=jnp.int32)[None, :, None]
    bb = jnp.arange(FFT_INNER, dtype=jnp.int32)[None, None, :]
    m = (d * (a + FFT_RADIX * bb)) % FFT_N
    ang = m.astype(F32) * np.float32(2.0 * np.pi / FFT_N)
    c = jnp.cos(ang) * (FFT_N ** -0.5)
    s = jnp.sin(ang) * (FFT_N ** -0.5)
    top = jnp.concatenate([c, -s], axis=2)
    bot = jnp.concatenate([-s, -c], axis=2)
    return jnp.concatenate([top, bot], axis=1).astype(BF16)


FF_CHUNK = 256
TAIL_ROWS = 1024


def _tail_kernel(*refs, final):
    if final:
        (x_ref, m_ref, g1_ref, wm_ref, sh_ref, sc_ref, gate_ref, g_ref, wg_ref, wu_ref, wd_ref,
         fg_ref, o_ref, acc_ref) = refs
    else:
        (x_ref, m_ref, g1_ref, wm_ref, sh_ref, sc_ref, gate_ref, g_ref, wg_ref, wu_ref, wd_ref,
         o_ref, acc_ref) = refs
    x = x_ref[0] + g1_ref[0] * jnp.dot(m_ref[0], wm_ref[...], preferred_element_type=F32)
    h = _norm_mod(x, g_ref[...], sh_ref[0], sc_ref[0]).astype(BF16)
    for c in range(D_FF // FF_CHUNK):
        cs = slice(c * FF_CHUNK, (c + 1) * FF_CHUNK)
        gt = jnp.dot(h, wg_ref[:, cs], preferred_element_type=F32)
        up = jnp.dot(h, wu_ref[:, cs], preferred_element_type=F32)
        a = (_silu(gt) * up).astype(BF16)
        part = jnp.dot(a, wd_ref[cs, :], preferred_element_type=F32)
        if c == 0:
            acc_ref[...] = part
        else:
            acc_ref[...] += part
    y = x + gate_ref[0] * acc_ref[...]
    if final:
        ms = jnp.mean(y * y, axis=-1, keepdims=True)
        y = (y * lax.rsqrt(ms + EPS)) * fg_ref[...]
    o_ref[0] = y


def _tail(x, m, gate1, wm, wm_layer, shift, scale, gate, g, wg, wu, wd, layer, tm, final_g=None):
    b, n, _ = x.shape
    tok = pl.BlockSpec((1, tm, D_MODEL), lambda i, bb: (bb, i, 0))
    mod = pl.BlockSpec((1, 1, D_MODEL), lambda i, bb: (bb, 0, 0))
    in_specs = [tok, tok, mod, _layer_spec((D_MODEL, D_MODEL), wm_layer), mod, mod, mod, _const_spec((1, D_MODEL)),
                _layer_spec((D_MODEL, D_FF), layer), _layer_spec((D_MODEL, D_FF), layer),
                _layer_spec((D_FF, D_MODEL), layer)]
    args = [x, m, gate1, wm, shift, scale, gate, g, wg, wu, wd]
    if final_g is not None:
        in_specs.append(_const_spec((1, D_MODEL)))
        args.append(final_g)
    return pl.pallas_call(
        functools.partial(_tail_kernel, final=final_g is not None),
        grid=(n // tm, b),
        in_specs=in_specs,
        out_specs=tok,
        out_shape=jax.ShapeDtypeStruct((b, n, D_MODEL), F32),
        scratch_shapes=[pltpu.VMEM((tm, D_MODEL), F32)],
        compiler_params=_params(2),
        name="tail",
    )(*args)


def _rope_lane_tables(n):
    rows = n // GRID_W
    r = jnp.repeat(jnp.arange(rows), GRID_W).astype(F32)
    col = jnp.tile(jnp.arange(GRID_W), rows).astype(F32)
    quarter = HEAD_DIM // 4
    inv = ROPE_THETA ** (-jnp.arange(quarter, dtype=F32) / quarter)
    ang_r = r[:, None] * inv
    ang_c = col[:, None] * inv
    cr, sr, cc, sc = jnp.cos(ang_r), jnp.sin(ang_r), jnp.cos(ang_c), jnp.sin(ang_c)
    z = jnp.zeros_like(sr)
    cos = jnp.concatenate([cr, cr, cc, cc], axis=1)
    sa = jnp.concatenate([-sr, z, -sc, z], axis=1)
    sb = jnp.concatenate([z, sr, z, sc], axis=1)
    rep = lambda t: jnp.concatenate([t, t], axis=1)
    return rep(cos), rep(sa), rep(sb)


def _identity_lane_tables(n):
    return jnp.ones((n, LANES), F32), jnp.zeros((n, LANES), F32), jnp.zeros((n, LANES), F32)


def kernel(x, c, ctx, c_ctx, w_ada, b_ada, norm1_g, norm2_g, w_in, conv_w, sink,
           w_mix_out, w_fourier_out, w_ffn_gate, w_ffn_up, w_ffn_down, final_g):
    b, n, d = x.shape
    L = ctx.shape[1]
    nc = b * L
    tm = 512

    cond = jnp.concatenate([c, c_ctx[None, :], jnp.zeros((COND_ROWS - b - 1, d), F32)], axis=0)
    mods = _adaln(cond, w_ada, b_ada)

    def split(l, lo, hi):
        m = mods[l, lo:hi]
        return [m[:, None, k * d:(k + 1) * d] for k in range(6)]

    rope_lat = _rope_lane_tables(n)
    rope_ctx = _identity_lane_tables(nc)
    w_in_b, w_mix_b, w_fo_b = w_in.astype(BF16), w_mix_out.astype(BF16), w_fourier_out.astype(BF16)
    wg_b, wu_b, wd_b = w_ffn_gate.astype(BF16), w_ffn_up.astype(BF16), w_ffn_down.astype(BF16)
    cc, sc_ = _dft_mats(FOURIER_GROUP, FOURIER_GROUP ** -0.5)
    cs_chan = jnp.concatenate([cc, sc_], axis=1).astype(BF16)
    seq_mats = {}
    for m in {n, L} - {FFT_N}:
        cm, sm = _dft_mats(m, m ** -0.5)
        seq_mats[m] = (cm.astype(BF16), (-sm).astype(BF16))
    fft_l = _fft_stage_mats() if n == FFT_N else None

    xc = ctx.reshape(1, nc, d)
    per_seq = lambda t: t.reshape(b, L, t.shape[-1])
    for l in range(DEPTH):
        ctx_after = any(j % 2 == 0 for j in range(l + 1, DEPTH))
        ctx_here = (l % 2 == 0) or ctx_after
        sh1, sc1, g1, sh2, sc2, g2 = split(l, 0, b)
        if ctx_here:
            csh1, csc1, cg1, csh2, csc2, cg2 = split(l, b, b + 1)
        n1 = norm1_g[l][None, :]
        n2 = norm2_g[l][None, :]
        if l % 2 == 0:
            e = l // 2
            wm, wm_layer = w_mix_b, e
            bgu_c, q_c, kk_c, vv_c = _inproj(xc, csh1, csc1, n1, w_in_b, e, *rope_ctx, tm=min(nc, INPROJ_ROWS))
            kk_c, vv_c = per_seq(kk_c), per_seq(vv_c)
            bgu, q, kk, vv = _inproj(x, sh1, sc1, n1, w_in_b, e, *rope_lat, tm=min(n, INPROJ_ROWS))
            mix = _mixout(bgu, q, kk, vv, kk_c, vv_c, sink[e], conv_w[e], tile_rows=min(n, MIX_ROWS), local=True)
            if ctx_after:
                mix_c = _mixout(per_seq(bgu_c), per_seq(q_c), None, None, kk_c, vv_c, sink[e], conv_w[e],
                                tile_rows=L, local=False)
        else:
            wm, wm_layer = w_fo_b, l // 2
            if n == FFT_N:
                mix = _seq_fft(_chan_dft_perm(x, sh1, sc1, n1, cs_chan), fft_l)
            else:
                mix = _seq_dft(*seq_mats[n], *_chan_dft(x, sh1, sc1, n1, cs_chan, tm=tm), tm=256)
            if ctx_after:
                y1c, y2c = _chan_dft(xc, csh1, csc1, n1, cs_chan, tm=min(nc, tm))
                mix_c = _seq_dft(*seq_mats[L], per_seq(y1c), per_seq(y2c), tm=L)
        x = _tail(x, mix, g1, wm, wm_layer, sh2, sc2, g2, n2, wg_b, wu_b, wd_b, l, tm=min(n, TAIL_ROWS),
                  final_g=final_g[None, :] if l == DEPTH - 1 else None)
        if ctx_after:
            xc = _tail(xc, mix_c.reshape(1, nc, d), cg1, wm, wm_layer, csh2, csc2, cg2, n2, wg_b, wu_b, wd_b, l,
                       tm=min(nc, TAIL_ROWS))
    return x
```

```python
import functools

import numpy as np
import jax
import jax.numpy as jnp
from jax import lax
from jax.experimental import pallas as pl
from jax.experimental.pallas import tpu as pltpu

F32 = jnp.float32
BF16 = jnp.bfloat16

D_MODEL = 1024
DEPTH = 4
GRID_W = 64
HEAD_DIM = 64
N_Q_HEADS = 8
N_KV_HEADS = 2
ATTN_WIDTH = N_Q_HEADS * HEAD_DIM
KV_WIDTH = N_KV_HEADS * HEAD_DIM
CONV_WIDTH = D_MODEL - ATTN_WIDTH
WINDOW = 128
BLOCK = 128
ROPE_THETA = 10000.0
N_FOURIER_GROUPS = 4
FOURIER_GROUP = D_MODEL // N_FOURIER_GROUPS
D_FF = 2816
EPS = 1e-6
Q_START = 3 * CONV_WIDTH
K_START = Q_START + ATTN_WIDTH
V_START = K_START + KV_WIDTH

LANES = 128
SUBLANES = 8
KV_REP = 2 * KV_WIDTH
NEG_BIG = -1e30
COND_ROWS = 24
VMEM_LIMIT = 56 * 1024 * 1024


def _params(n_axes):
    return pltpu.CompilerParams(dimension_semantics=("arbitrary",) * n_axes,
                                vmem_limit_bytes=VMEM_LIMIT)


def _const_spec(shape):
    nd = len(shape)
    return pl.BlockSpec(shape, lambda *_: (0,) * nd, pipeline_mode=pl.Buffered(1))


def _layer_spec(shape, layer):
    nd = len(shape)
    return pl.BlockSpec((None,) + tuple(shape), lambda *_: (layer,) + (0,) * nd, pipeline_mode=pl.Buffered(1))


def _norm_mod(x, g, shift, scale):
    ms = jnp.mean(x * x, axis=-1, keepdims=True)
    y = x * lax.rsqrt(ms + EPS)
    return (y * g) * (1.0 + scale) + shift


def _silu(x):
    return x * (1.0 / (1.0 + jnp.exp(-x)))


def _adaln_kernel(c_ref, w_ref, b_ref, o_ref):
    a = _silu(c_ref[...])
    w = w_ref[0]
    a_hi = a.astype(BF16)
    a_lo = (a - a_hi.astype(F32)).astype(BF16)
    w_hi = w.astype(BF16)
    w_lo = (w - w_hi.astype(F32)).astype(BF16)
    acc = jnp.dot(a_hi, w_hi, preferred_element_type=F32)
    acc += jnp.dot(a_lo, w_hi, preferred_element_type=F32)
    acc += jnp.dot(a_hi, w_lo, preferred_element_type=F32)
    o_ref[0] = acc + b_ref[0]


def _adaln(cond, w_ada, b_ada):
    tn = 1024
    nt = (6 * D_MODEL) // tn
    return pl.pallas_call(
        _adaln_kernel,
        grid=(DEPTH, nt),
        in_specs=[
            pl.BlockSpec((COND_ROWS, D_MODEL), lambda l, j: (0, 0)),
            pl.BlockSpec((1, D_MODEL, tn), lambda l, j: (l, 0, j)),
            pl.BlockSpec((1, 1, tn), lambda l, j: (l, 0, j)),
        ],
        out_specs=pl.BlockSpec((1, COND_ROWS, tn), lambda l, j: (l, 0, j)),
        out_shape=jax.ShapeDtypeStruct((DEPTH, COND_ROWS, 6 * D_MODEL), F32),
        compiler_params=_params(2),
        name="adaln",
    )(cond, w_ada, b_ada.reshape(DEPTH, 1, 6 * D_MODEL))


IN_WIDTH = V_START + KV_WIDTH
LOG2E = float(np.log2(np.e))
Q_SCALE = HEAD_DIM ** -0.5 * LOG2E
INPROJ_ROWS = 2048
INPROJ_SUB = 512


def _inproj_kernel(x_ref, sh_ref, sc_ref, g_ref, w_ref, cos_ref, sa_ref, sb_ref,
                   bgu_ref, q_ref, kk_ref, vv_ref):
    first_half = lax.broadcasted_iota(jnp.int32, (1, LANES), 1) < HEAD_DIM
    tm = x_ref.shape[1]
    sub = min(tm, INPROJ_SUB)
    for s0 in range(0, tm, sub):
        rs = slice(s0, s0 + sub)
        h = _norm_mod(x_ref[0, rs, :], g_ref[...], sh_ref[0], sc_ref[0]).astype(BF16)
        p = jnp.dot(h, w_ref[...], preferred_element_type=F32)
        bgu_ref[0, rs, :CONV_WIDTH] = p[:, :CONV_WIDTH].astype(BF16)
        bgu_ref[0, rs, CONV_WIDTH:] = (p[:, CONV_WIDTH:2 * CONV_WIDTH]
                                       * p[:, 2 * CONV_WIDTH:Q_START]).astype(BF16)
        cos = cos_ref[rs, :]
        sa = sa_ref[rs, :]
        sb = sb_ref[rs, :]

        def rope(t):
            return t * cos + pltpu.roll(t, LANES - 16, 1) * sa + pltpu.roll(t, 16, 1) * sb

        for j in range(ATTN_WIDTH // LANES):
            lo = Q_START + j * LANES
            q_ref[0, rs, j * LANES:(j + 1) * LANES] = (rope(p[:, lo:lo + LANES]) * Q_SCALE).astype(BF16)
        for t, ref in ((rope(p[:, K_START:V_START]), kk_ref), (p[:, V_START:], vv_ref)):
            swapped = pltpu.roll(t, HEAD_DIM, 1)
            ref[0, rs, :LANES] = jnp.where(first_half, t, swapped).astype(BF16)
            ref[0, rs, LANES:] = jnp.where(first_half, swapped, t).astype(BF16)


def _inproj(x, shift, scale, g, w_in, layer, cos, sa, sb, tm):
    b, n, _ = x.shape
    nt = n // tm
    tok = lambda w: pl.BlockSpec((1, tm, w), lambda i, bb: (bb, i, 0))
    mod = pl.BlockSpec((1, 1, D_MODEL), lambda i, bb: (bb, 0, 0))
    tab = pl.BlockSpec((tm, LANES), lambda i, bb: (i, 0))
    return pl.pallas_call(
        _inproj_kernel,
        grid=(nt, b),
        in_specs=[tok(D_MODEL), mod, mod, _const_spec((1, D_MODEL)),
                  _layer_spec((D_MODEL, IN_WIDTH), layer), tab, tab, tab],
        out_specs=[tok(D_MODEL), tok(ATTN_WIDTH), tok(KV_REP), tok(KV_REP)],
        out_shape=[jax.ShapeDtypeStruct((b, n, D_MODEL), BF16),
                   jax.ShapeDtypeStruct((b, n, ATTN_WIDTH), BF16),
                   jax.ShapeDtypeStruct((b, n, KV_REP), BF16),
                   jax.ShapeDtypeStruct((b, n, KV_REP), BF16)],
        compiler_params=_params(2),
        name="inproj",
    )(x, shift, scale, g, w_in, cos, sa, sb)


CONV_ROWS = 32
SM_ROWS = 16
SCORE_SLOTS = 2
MIX_ROWS = 1024


def _mixout_kernel(*refs, tile_rows, local):
    if local:
        (sink_ref, bgu_ref, bgu_p_ref, bgu_n_ref, q_ref,
         kk_ref, kk_p_ref, kk_n_ref, vv_ref, vv_p_ref, vv_n_ref, kkc_ref, vvc_ref,
         bias_mid_ref, bias_a_ref, bias_b_ref, cw_ref, o_ref, s_slots, e_slots) = refs
    else:
        (sink_ref, bgu_ref, bgu_p_ref, bgu_n_ref, q_ref,
         kkc_ref, vvc_ref, cw_ref, o_ref, s_slots, e_slots) = refs
    i = pl.program_id(0)
    n_tiles = pl.num_programs(0)
    r = tile_rows
    nb = r // BLOCK

    has_prev = (i > 0).astype(F32)
    has_next = (i < n_tiles - 1).astype(F32)
    cw = cw_ref[...]
    cr = CONV_ROWS
    pk = 2 * SUBLANES
    row = lax.broadcasted_iota(jnp.int32, (cr, CONV_WIDTH), 0)

    def conv_chunk(r0):
        bg = bgu_ref[0, r0:r0 + cr, :CONV_WIDTH].astype(F32)
        u = bgu_ref[0, r0:r0 + cr, CONV_WIDTH:].astype(F32)
        if r0 == 0:
            u_prev = bgu_p_ref[0, SUBLANES - 1:SUBLANES, CONV_WIDTH:].astype(F32) * has_prev
        else:
            u_prev = bgu_ref[0, r0 - pk:r0, CONV_WIDTH:].astype(F32)[pk - 1:pk]
        if r0 + cr == r:
            u_next = bgu_n_ref[0, 0:1, CONV_WIDTH:].astype(F32) * has_next
        else:
            u_next = bgu_ref[0, r0 + cr:r0 + cr + pk, CONV_WIDTH:].astype(F32)[0:1]
        u_up = jnp.where(row == 0, u_prev, pltpu.roll(u, 1, 0))
        u_dn = jnp.where(row == cr - 1, u_next, pltpu.roll(u, cr - 1, 0))
        o_ref[0, r0:r0 + cr, :CONV_WIDTH] = (bg * (u_up * cw[0:1] + u * cw[1:2] + u_dn * cw[2:3])).astype(BF16)

    lane = lax.broadcasted_iota(jnp.int32, (1, LANES), 1)
    first_half = lane < HEAD_DIM
    zero = jnp.zeros((), BF16)
    nt_dims = (((1,), (1,)), ((), ()))

    def split_k(t):
        return jnp.where(first_half, t, zero), jnp.where(first_half, zero, t)

    def split_v(t):
        va, vb = split_k(t)
        first = lax.broadcasted_iota(jnp.int32, t.shape, 1) < HEAD_DIM
        ones_a = jnp.where(first, 1.0, 0.0).astype(BF16)
        ones_b = jnp.where(first, 0.0, 1.0).astype(BF16)
        return jnp.concatenate([va, ones_a], axis=1), jnp.concatenate([vb, ones_b], axis=1)

    n_ctx = kkc_ref.shape[1]
    wc = 2 * n_ctx
    nl = 3 * BLOCK
    groups_per_pair = BLOCK // SM_ROWS
    cols_a = [slice(0, n_ctx)] + ([slice(wc, wc + nl)] if local else [])
    cols_b = [slice(n_ctx, wc)] + ([slice(wc + nl, wc + 2 * nl)] if local else [])

    heads = []
    for h in range(N_KV_HEADS):
        hs = slice(h * LANES, (h + 1) * LANES)
        hd = dict(kc_bd=jnp.concatenate(split_k(kkc_ref[0, :, hs]), axis=0),
                  vc_ext=jnp.concatenate(split_v(vvc_ref[0, :, hs]), axis=0),
                  sinks=[sink_ref[4 * h + i] * LOG2E for i in range(4)])
        if local:
            hd["k"] = split_k(jnp.concatenate([kk_p_ref[0, :, hs], kk_ref[0, :, hs], kk_n_ref[0, :, hs]], axis=0))
            hd["v"] = split_v(jnp.concatenate([vv_p_ref[0, :, hs], vv_ref[0, :, hs], vv_n_ref[0, :, hs]], axis=0))
        heads.append(hd)

    blocks = [(h, j) for h in range(N_KV_HEADS) for j in range(nb)]

    def scores(idx):
        h, j = blocks[idx]
        hd = heads[h]
        rs = slice(j * BLOCK, (j + 1) * BLOCK)
        q2 = jnp.concatenate([q_ref[0, rs, 2 * h * LANES:(2 * h + 1) * LANES],
                              q_ref[0, rs, (2 * h + 1) * LANES:(2 * h + 2) * LANES]], axis=0)
        s_ref = s_slots.at[idx % SCORE_SLOTS]
        s_ref[:, :wc] = lax.dot_general(q2, hd["kc_bd"], nt_dims, preferred_element_type=F32)
        if local:
            ws = slice(j * BLOCK, j * BLOCK + nl)
            kl_bd = jnp.concatenate([hd["k"][0][ws], hd["k"][1][ws]], axis=0)
            if nb == 1:
                bias = bias_a_ref[0] + bias_b_ref[0]
            elif j == 0:
                bias = bias_a_ref[0]
            elif j == nb - 1:
                bias = bias_b_ref[0]
            else:
                bias = bias_mid_ref[0]
            s_ref[:, wc:] = lax.dot_general(q2, kl_bd, nt_dims, preferred_element_type=F32) + bias

    def softmax_pv(idx):
        h, j = blocks[idx]
        hd = heads[h]
        s_ref = s_slots.at[idx % SCORE_SLOTS]
        e_ref = e_slots.at[idx % SCORE_SLOTS]
        sink_terms = []
        for g in range(2 * BLOCK // SM_ROWS):
            gr = slice(g * SM_ROWS, (g + 1) * SM_ROWS)
            pair = g // groups_per_pair
            terms = []
            for cols, sk in ((cols_a, hd["sinks"][2 * pair]), (cols_b, hd["sinks"][2 * pair + 1])):
                ss = [s_ref[gr, c] for c in cols]
                m = jnp.max(ss[0], axis=-1, keepdims=True)
                for t in ss[1:]:
                    m = jnp.maximum(m, jnp.max(t, axis=-1, keepdims=True))
                m = jnp.maximum(m, sk)
                for c, t in zip(cols, ss):
                    e_ref[gr, c] = jnp.exp2(t - m).astype(BF16)
                terms.append(jnp.exp2(sk - m))
            sink_terms.append(jnp.where(first_half, terms[0], terms[1]))
        o_ext = jnp.dot(e_ref[:, :wc], hd["vc_ext"], preferred_element_type=F32)
        if local:
            ws = slice(j * BLOCK, j * BLOCK + nl)
            vl_ext = jnp.concatenate([hd["v"][0][ws], hd["v"][1][ws]], axis=0)
            o_ext = o_ext + jnp.dot(e_ref[:, wc:], vl_ext, preferred_element_type=F32)
        o = o_ext[:, :LANES] / (o_ext[:, LANES:] + jnp.concatenate(sink_terms, axis=0))
        return jnp.concatenate([o[:BLOCK], o[BLOCK:]], axis=1)

    conv_starts = list(range(0, r, cr))
    per_block = -(-len(conv_starts) // len(blocks))
    outs = {}
    scores(0)
    for idx in range(len(blocks)):
        if idx + 1 < len(blocks):
            scores(idx + 1)
        for r0 in conv_starts[idx * per_block:(idx + 1) * per_block]:
            conv_chunk(r0)
        outs[blocks[idx]] = softmax_pv(idx)
    b_cols = [jnp.concatenate([outs[(h, j)] for j in range(nb)], axis=0) for h in range(N_KV_HEADS)]
    o_ref[0, :, CONV_WIDTH:] = jnp.concatenate(b_cols, axis=1).astype(BF16)


def _band_bias():
    rr = np.arange(2 * BLOCK)[:, None] % BLOCK
    kidx = np.arange(6 * BLOCK)[None, :] % (3 * BLOCK)
    band = (kidx >= rr) & (kidx <= rr + 2 * WINDOW)
    mid = band
    first = band & (kidx >= BLOCK)
    last = band & (kidx < 2 * BLOCK)
    return jnp.asarray(np.where(np.stack([mid, first, last]), 0.0, NEG_BIG), dtype=F32)


def _mixout(bgu, q, kk, vv, kkc, vvc, sink, conv_w, tile_rows, local):
    b, n, _ = bgu.shape
    r = tile_rows
    nt = n // r
    nctx = kkc.shape[1]
    tok = lambda w: pl.BlockSpec((1, r, w), lambda i, bb: (bb, i, 0))
    r8 = r // SUBLANES
    halo8_p = pl.BlockSpec((1, SUBLANES, D_MODEL), lambda i, bb: (bb, jnp.maximum(i * r8 - 1, 0), 0))
    halo8_n = pl.BlockSpec((1, SUBLANES, D_MODEL),
                           lambda i, bb: (bb, jnp.minimum((i + 1) * r8, n // SUBLANES - 1), 0))
    ctx = pl.BlockSpec((1, nctx, KV_REP), lambda i, bb: (bb, 0, 0))
    smem = pl.BlockSpec(memory_space=pltpu.SMEM)
    in_specs = [smem, tok(D_MODEL), halo8_p, halo8_n, tok(ATTN_WIDTH)]
    args = [sink, bgu, bgu, bgu, q]
    if local:
        rb = r // BLOCK
        halo_p = pl.BlockSpec((1, BLOCK, KV_REP), lambda i, bb: (bb, jnp.maximum(i * rb - 1, 0), 0))
        halo_n = pl.BlockSpec((1, BLOCK, KV_REP),
                              lambda i, bb: (bb, jnp.minimum((i + 1) * rb, n // BLOCK - 1), 0))
        bias = _band_bias()
        bshape = (1, 2 * BLOCK, 6 * BLOCK)
        in_specs += [tok(KV_REP), halo_p, halo_n, tok(KV_REP), halo_p, halo_n, ctx, ctx,
                     pl.BlockSpec(bshape, lambda i, bb: (0, 0, 0)),
                     pl.BlockSpec(bshape, lambda i, bb: (jnp.where(i == 0, 1, 0), 0, 0)),
                     pl.BlockSpec(bshape, lambda i, bb: (jnp.where(i == nt - 1, 2, 0), 0, 0))]
        args += [kk, kk, kk, vv, vv, vv, kkc, vvc, bias, bias, bias]
    else:
        in_specs += [ctx, ctx]
        args += [kkc, vvc]
    in_specs += [_const_spec((3, CONV_WIDTH))]
    args += [conv_w]
    score_cols = 2 * nctx + (6 * BLOCK if local else 0)
    return pl.pallas_call(
        functools.partial(_mixout_kernel, tile_rows=r, local=local),
        grid=(nt, b),
        in_specs=in_specs,
        out_specs=tok(D_MODEL),
        out_shape=jax.ShapeDtypeStruct((b, n, D_MODEL), BF16),
        scratch_shapes=[pltpu.VMEM((SCORE_SLOTS, 2 * BLOCK, score_cols), F32),
                        pltpu.VMEM((SCORE_SLOTS, 2 * BLOCK, score_cols), BF16)],
        compiler_params=_params(2),
        name="mixout_local" if local else "mixout_ctx",
    )(*args)


def _chan_dft_kernel(x_ref, sh_ref, sc_ref, g_ref, cs_ref, y1_ref, y2_ref):
    h = _norm_mod(x_ref[0], g_ref[...], sh_ref[0], sc_ref[0]).astype(BF16)
    fg = FOURIER_GROUP
    for gi in range(N_FOURIER_GROUPS):
        yc = jnp.dot(h[:, gi * fg:(gi + 1) * fg], cs_ref[...], preferred_element_type=F32)
        y1_ref[0, :, gi * fg:(gi + 1) * fg] = yc[:, :fg].astype(BF16)
        y2_ref[0, :, gi * fg:(gi + 1) * fg] = yc[:, fg:].astype(BF16)


def _chan_dft(x, shift, scale, g, cs, tm):
    b, n, _ = x.shape
    tok = pl.BlockSpec((1, tm, D_MODEL), lambda i, bb: (bb, i, 0))
    mod = pl.BlockSpec((1, 1, D_MODEL), lambda i, bb: (bb, 0, 0))
    return pl.pallas_call(
        _chan_dft_kernel,
        grid=(n // tm, b),
        in_specs=[tok, mod, mod, _const_spec((1, D_MODEL)), _const_spec((FOURIER_GROUP, 2 * FOURIER_GROUP))],
        out_specs=[tok, tok],
        out_shape=[jax.ShapeDtypeStruct((b, n, D_MODEL), BF16)] * 2,
        compiler_params=_params(2),
        name="chan_dft",
    )(x, shift, scale, g, cs)


def _seq_dft_kernel(cn_ref, sn_ref, y1_ref, y2_ref, o_ref):
    z = jnp.dot(cn_ref[...], y1_ref[0], preferred_element_type=F32)
    z = z + jnp.dot(sn_ref[...], y2_ref[0], preferred_element_type=F32)
    o_ref[0] = z.astype(BF16)


def _seq_dft(cn, nsn, y1, y2, tm):
    b, n, _ = y1.shape
    tok = pl.BlockSpec((1, tm, D_MODEL), lambda bb, i: (bb, i, 0))
    mat = pl.BlockSpec((tm, n), lambda bb, i: (i, 0))
    full = pl.BlockSpec((1, n, D_MODEL), lambda bb, i: (bb, 0, 0), pipeline_mode=pl.Buffered(1))
    return pl.pallas_call(
        _seq_dft_kernel,
        grid=(b, n // tm),
        in_specs=[mat, mat, full, full],
        out_specs=tok,
        out_shape=jax.ShapeDtypeStruct((b, n, D_MODEL), BF16),
        compiler_params=_params(2),
        name="seq_dft",
    )(cn, nsn, y1, y2)


def _dft_mats(n, scale):
    j = np.arange(n)
    ang = ((j[:, None] * j[None, :]) % n) * (2.0 * np.pi / n)
    return (np.cos(ang) * scale).astype(np.float32), (np.sin(ang) * scale).astype(np.float32)


FFT_RADIX = 16
FFT_INNER = 256
FFT_N = FFT_RADIX * FFT_INNER
PERM_TILE = 512
PERM_B = PERM_TILE // FFT_RADIX
PERM_STEP = 4 * PERM_TILE


def _chan_dft_perm_kernel(x_ref, sh_ref, sc_ref, g_ref, p_ref, cs_ref, y_ref):
    fg = FOURIER_GROUP
    for t in range(x_ref.shape[1] // PERM_TILE):
        h = _norm_mod(x_ref[0, t * PERM_TILE:(t + 1) * PERM_TILE, :], g_ref[...], sh_ref[0], sc_ref[0]).astype(BF16)
        hp = jnp.dot(p_ref[...], h, preferred_element_type=F32).astype(BF16)
        for gi in range(N_FOURIER_GROUPS):
            gs = slice(gi * fg, (gi + 1) * fg)
            yc = jnp.dot(hp[:, gs], cs_ref[...], preferred_element_type=F32)
            for a in range(FFT_RADIX):
                rs = slice(a * PERM_B, (a + 1) * PERM_B)
                ws = slice(t * PERM_B, (t + 1) * PERM_B)
                y_ref[0, a, 0, ws, gs] = yc[rs, :fg].astype(BF16)
                y_ref[0, a, 1, ws, gs] = yc[rs, fg:].astype(BF16)


def _chan_dft_perm(x, shift, scale, g, cs):
    b, n, _ = x.shape
    r_out = np.arange(PERM_TILE)
    perm = np.zeros((PERM_TILE, PERM_TILE), np.float32)
    perm[r_out, FFT_RADIX * (r_out % PERM_B) + r_out // PERM_B] = 1.0
    tok = pl.BlockSpec((1, PERM_STEP, D_MODEL), lambda i, bb: (bb, i, 0))
    mod = pl.BlockSpec((1, 1, D_MODEL), lambda i, bb: (bb, 0, 0))
    return pl.pallas_call(
        _chan_dft_perm_kernel,
        grid=(n // PERM_STEP, b),
        in_specs=[tok, mod, mod, _const_spec((1, D_MODEL)), _const_spec((PERM_TILE, PERM_TILE)),
                  _const_spec((FOURIER_GROUP, 2 * FOURIER_GROUP))],
        out_specs=pl.BlockSpec((1, FFT_RADIX, 2, PERM_STEP // FFT_RADIX, D_MODEL), lambda i, bb: (bb, 0, 0, i, 0)),
        out_shape=jax.ShapeDtypeStruct((b, FFT_RADIX, 2, n // FFT_RADIX, D_MODEL), BF16),
        compiler_params=_params(2),
        name="chan_dft_perm",
    )(x, shift, scale, g, jnp.asarray(perm, BF16), cs)


_C1 = float(np.cos(np.pi / 8))
_C2 = float(np.cos(np.pi / 4))
_C3 = float(np.cos(3 * np.pi / 8))


def _dft16_real(ur, ui):
    p = [ur[0]] + [ur[a] + ur[16 - a] for a in range(1, 8)] + [ur[8]]
    q = [p[a] + p[8 - a] for a in range(4)] + [p[4]]
    r = [p[a] - p[8 - a] for a in range(4)]
    s0, s1, s2 = q[0] + q[4], q[1] + q[3], q[2]
    t0, t1 = q[0] - q[4], q[1] - q[3]
    ca = [None] * 9
    ca[0] = s0 + s1 + s2
    ca[4] = s0 - s2
    ca[8] = s0 - s1 + s2
    ca[2] = t0 + _C2 * t1
    ca[6] = t0 - _C2 * t1
    e0, e1 = r[0] + _C2 * r[2], r[0] - _C2 * r[2]
    f, g = _C1 * r[1] + _C3 * r[3], _C3 * r[1] - _C1 * r[3]
    ca[1], ca[7], ca[3], ca[5] = e0 + f, e0 - f, e1 + g, e1 - g
    pp = [None] + [ui[a] - ui[16 - a] for a in range(1, 8)]
    qq = [None] + [pp[a] + pp[8 - a] for a in range(1, 4)] + [pp[4]]
    rr = [None] + [pp[a] - pp[8 - a] for a in range(1, 4)]
    sb = [None] * 8
    e0, e1 = _C2 * qq[2] + qq[4], _C2 * qq[2] - qq[4]
    f, g = _C3 * qq[1] + _C1 * qq[3], _C1 * qq[1] - _C3 * qq[3]
    sb[1], sb[7], sb[3], sb[5] = f + e0, f - e0, g + e1, g - e1
    w = _C2 * (rr[1] + rr[3])
    sb[2], sb[6], sb[4] = w + rr[2], w - rr[2], rr[1] - rr[3]
    out = [None] * 16
    out[0], out[8] = ca[0], ca[8]
    for c in range(1, 8):
        out[c] = ca[c] + sb[c]
        out[16 - c] = ca[c] - sb[c]
    return out


def _seq_fft_kernel(y_ref, l_ref, o_ref, u_even_ref, u_odd_ref):
    t = pl.program_id(0)

    @pl.when(t == 0)
    def _():
        u_odd_ref[...] = jnp.zeros(u_odd_ref.shape, F32)

    def step(cur_ref, prev_ref):
        rt = 2 * SUBLANES
        for i in range(FFT_INNER // rt):
            r0 = i * rt
            for lh in range(FOURIER_GROUP // LANES):
                ls = slice(lh * LANES, (lh + 1) * LANES)
                ur = [prev_ref[a, r0:r0 + rt, ls] for a in range(FFT_RADIX)]
                ui = [prev_ref[a, FFT_INNER + r0:FFT_INNER + r0 + rt, ls] for a in range(FFT_RADIX)]
                out = _dft16_real(ur, ui)
                for c in range(FFT_RADIX):
                    o_ref[0, c * FFT_INNER + r0:c * FFT_INNER + r0 + rt, ls] = out[c].astype(BF16)
        for a in range(FFT_RADIX):
            u = jnp.dot(l_ref[a, :, :FFT_INNER], y_ref[0, a, 0], preferred_element_type=F32)
            cur_ref[a] = u + jnp.dot(l_ref[a, :, FFT_INNER:], y_ref[0, a, 1], preferred_element_type=F32)

    pl.when(t % 2 == 0)(lambda: step(u_even_ref, u_odd_ref))
    pl.when(t % 2 == 1)(lambda: step(u_odd_ref, u_even_ref))


def _seq_fft(ycat, lmat):
    b = ycat.shape[0]
    ng = N_FOURIER_GROUPS
    items = b * ng

    def y_map(t):
        tt = jnp.minimum(t, items - 1)
        return (tt // ng, 0, 0, 0, tt % ng)

    def o_map(t):
        tp = jnp.maximum(t - 1, 0)
        return (tp // ng, 0, tp % ng)

    return pl.pallas_call(
        _seq_fft_kernel,
        grid=(items + 1,),
        in_specs=[pl.BlockSpec((1, FFT_RADIX, 2, FFT_INNER, FOURIER_GROUP), y_map),
                  _const_spec((FFT_RADIX, 2 * FFT_INNER, 2 * FFT_INNER))],
        out_specs=pl.BlockSpec((1, FFT_N, FOURIER_GROUP), o_map),
        out_shape=jax.ShapeDtypeStruct((b, FFT_N, D_MODEL), BF16),
        scratch_shapes=[pltpu.VMEM((FFT_RADIX, 2 * FFT_INNER, FOURIER_GROUP), F32)] * 2,
        compiler_params=_params(1),
        name="seq_fft",
    )(ycat, lmat)


def _fft_stage_mats():
    a = np.arange(FFT_RADIX)[:, None, None]
    d = np.arange(FFT_INNER)[None, :, None]
    bb = np.arange(FFT_INNER)[None, None, :]
    ang = ((d * (a + FFT_RADIX * bb)) % FFT_N) * (2.0 * np.pi / FFT_N)
    c = np.cos(ang) * (FFT_N ** -0.5)
    s = np.sin(ang) * (FFT_N ** -0.5)
    top = np.concatenate([c, -s], axis=2)
    bot = np.concatenate([-s, -c], axis=2)
    return jnp.asarray(np.concatenate([top, bot], axis=1).astype(np.float32), BF16)


FF_CHUNK = 256
TAIL_ROWS = 1024


def _tail_kernel(*refs, final):
    if final:
        (x_ref, m_ref, g1_ref, wm_ref, sh_ref, sc_ref, gate_ref, g_ref, wg_ref, wu_ref, wd_ref,
         fg_ref, o_ref, acc_ref) = refs
    else:
        (x_ref, m_ref, g1_ref, wm_ref, sh_ref, sc_ref, gate_ref, g_ref, wg_ref, wu_ref, wd_ref,
         o_ref, acc_ref) = refs
    x = x_ref[0] + g1_ref[0] * jnp.dot(m_ref[0], wm_ref[...], preferred_element_type=F32)
    h = _norm_mod(x, g_ref[...], sh_ref[0], sc_ref[0]).astype(BF16)
    for c in range(D_FF // FF_CHUNK):
        cs = slice(c * FF_CHUNK, (c + 1) * FF_CHUNK)
        gt = jnp.dot(h, wg_ref[:, cs], preferred_element_type=F32)
        up = jnp.dot(h, wu_ref[:, cs], preferred_element_type=F32)
        a = (_silu(gt) * up).astype(BF16)
        part = jnp.dot(a, wd_ref[cs, :], preferred_element_type=F32)
        if c == 0:
            acc_ref[...] = part
        else:
            acc_ref[...] += part
    y = x + gate_ref[0] * acc_ref[...]
    if final:
        ms = jnp.mean(y * y, axis=-1, keepdims=True)
        y = (y * lax.rsqrt(ms + EPS)) * fg_ref[...]
    o_ref[0] = y


def _tail(x, m, gate1, wm, wm_layer, shift, scale, gate, g, wg, wu, wd, layer, tm, final_g=None):
    b, n, _ = x.shape
    tok = pl.BlockSpec((1, tm, D_MODEL), lambda i, bb: (bb, i, 0))
    mod = pl.BlockSpec((1, 1, D_MODEL), lambda i, bb: (bb, 0, 0))
    in_specs = [tok, tok, mod, _layer_spec((D_MODEL, D_MODEL), wm_layer), mod, mod, mod, _const_spec((1, D_MODEL)),
                _layer_spec((D_MODEL, D_FF), layer), _layer_spec((D_MODEL, D_FF), layer),
                _layer_spec((D_FF, D_MODEL), layer)]
    args = [x, m, gate1, wm, shift, scale, gate, g, wg, wu, wd]
    if final_g is not None:
        in_specs.append(_const_spec((1, D_MODEL)))
        args.append(final_g)
    return pl.pallas_call(
        functools.partial(_tail_kernel, final=final_g is not None),
        grid=(n // tm, b),
        in_specs=in_specs,
        out_specs=tok,
        out_shape=jax.ShapeDtypeStruct((b, n, D_MODEL), F32),
        scratch_shapes=[pltpu.VMEM((tm, D_MODEL), F32)],
        compiler_params=_params(2),
        name="tail",
    )(*args)


def _rope_lane_tables(n):
    rows = n // GRID_W
    r = np.repeat(np.arange(rows), GRID_W).astype(np.float64)
    col = np.tile(np.arange(GRID_W), rows).astype(np.float64)
    quarter = HEAD_DIM // 4
    inv = ROPE_THETA ** (-np.arange(quarter, dtype=np.float64) / quarter)
    ang_r = r[:, None] * inv
    ang_c = col[:, None] * inv
    cr, sr, cc, sc = np.cos(ang_r), np.sin(ang_r), np.cos(ang_c), np.sin(ang_c)
    z = np.zeros_like(sr)
    cos = np.concatenate([cr, cr, cc, cc], axis=1)
    sa = np.concatenate([-sr, z, -sc, z], axis=1)
    sb = np.concatenate([z, sr, z, sc], axis=1)
    rep = lambda t: jnp.asarray(np.concatenate([t, t], axis=1).astype(np.float32))
    return rep(cos), rep(sa), rep(sb)


def _identity_lane_tables(n):
    one, zero = np.ones((n, LANES), np.float32), np.zeros((n, LANES), np.float32)
    return jnp.asarray(one), jnp.asarray(zero), jnp.asarray(zero)


def kernel(x, c, ctx, c_ctx, w_ada, b_ada, norm1_g, norm2_g, w_in, conv_w, sink,
           w_mix_out, w_fourier_out, w_ffn_gate, w_ffn_up, w_ffn_down, final_g):
    b, n, d = x.shape
    L = ctx.shape[1]
    nc = b * L
    tm = 512

    cond = jnp.concatenate([c, c_ctx[None, :], jnp.zeros((COND_ROWS - b - 1, d), F32)], axis=0)
    mods = _adaln(cond, w_ada, b_ada)

    def split(l, lo, hi):
        m = mods[l, lo:hi]
        return [m[:, None, k * d:(k + 1) * d] for k in range(6)]

    rope_lat = _rope_lane_tables(n)
    rope_ctx = _identity_lane_tables(nc)
    w_in_b, w_mix_b, w_fo_b = w_in.astype(BF16), w_mix_out.astype(BF16), w_fourier_out.astype(BF16)
    wg_b, wu_b, wd_b = w_ffn_gate.astype(BF16), w_ffn_up.astype(BF16), w_ffn_down.astype(BF16)
    cc, sc_ = _dft_mats(FOURIER_GROUP, FOURIER_GROUP ** -0.5)
    cs_chan = jnp.asarray(np.concatenate([cc, sc_], axis=1), BF16)
    seq_mats = {}
    for m in {n, L} - {FFT_N}:
        cm, sm = _dft_mats(m, m ** -0.5)
        seq_mats[m] = (jnp.asarray(cm, BF16), jnp.asarray(-sm, BF16))
    fft_l = _fft_stage_mats() if n == FFT_N else None

    xc = ctx.reshape(1, nc, d)
    per_seq = lambda t: t.reshape(b, L, t.shape[-1])
    for l in range(DEPTH):
        ctx_after = any(j % 2 == 0 for j in range(l + 1, DEPTH))
        ctx_here = (l % 2 == 0) or ctx_after
        sh1, sc1, g1, sh2, sc2, g2 = split(l, 0, b)
        if ctx_here:
            csh1, csc1, cg1, csh2, csc2, cg2 = split(l, b, b + 1)
        n1 = norm1_g[l][None, :]
        n2 = norm2_g[l][None, :]
        if l % 2 == 0:
            e = l // 2
            wm, wm_layer = w_mix_b, e
            bgu_c, q_c, kk_c, vv_c = _inproj(xc, csh1, csc1, n1, w_in_b, e, *rope_ctx, tm=min(nc, INPROJ_ROWS))
            kk_c, vv_c = per_seq(kk_c), per_seq(vv_c)
            bgu, q, kk, vv = _inproj(x, sh1, sc1, n1, w_in_b, e, *rope_lat, tm=min(n, INPROJ_ROWS))
            mix = _mixout(bgu, q, kk, vv, kk_c, vv_c, sink[e], conv_w[e], tile_rows=min(n, MIX_ROWS), local=True)
            if ctx_after:
                mix_c = _mixout(per_seq(bgu_c), per_seq(q_c), None, None, kk_c, vv_c, sink[e], conv_w[e],
                                tile_rows=L, local=False)
        else:
            wm, wm_layer = w_fo_b, l // 2
            if n == FFT_N:
                mix = _seq_fft(_chan_dft_perm(x, sh1, sc1, n1, cs_chan), fft_l)
            else:
                mix = _seq_dft(*seq_mats[n], *_chan_dft(x, sh1, sc1, n1, cs_chan, tm=tm), tm=256)
            if ctx_after:
                y1c, y2c = _chan_dft(xc, csh1, csc1, n1, cs_chan, tm=min(nc, tm))
                mix_c = _seq_dft(*seq_mats[L], per_seq(y1c), per_seq(y2c), tm=L)
        x = _tail(x, mix, g1, wm, wm_layer, sh2, sc2, g2, n2, wg_b, wu_b, wd_b, l, tm=min(n, TAIL_ROWS),
                  final_g=final_g[None, :] if l == DEPTH - 1 else None)
        if ctx_after:
            xc = _tail(xc, mix_c.reshape(1, nc, d), cg1, wm, wm_layer, csh2, csc2, cg2, n2, wg_b, wu_b, wd_b, l,
                       tm=min(nc, TAIL_ROWS))
    return x
```

```python
import functools

import numpy as np
import jax
import jax.numpy as jnp
from jax import lax
from jax.experimental import pallas as pl
from jax.experimental.pallas import tpu as pltpu

F32 = jnp.float32
BF16 = jnp.bfloat16

D_MODEL = 1024
DEPTH = 4
GRID_W = 64
HEAD_DIM = 64
N_Q_HEADS = 8
N_KV_HEADS = 2
ATTN_WIDTH = N_Q_HEADS * HEAD_DIM
KV_WIDTH = N_KV_HEADS * HEAD_DIM
CONV_WIDTH = D_MODEL - ATTN_WIDTH
WINDOW = 128
BLOCK = 128
ROPE_THETA = 10000.0
N_FOURIER_GROUPS = 4
FOURIER_GROUP = D_MODEL // N_FOURIER_GROUPS
D_FF = 2816
EPS = 1e-6
Q_START = 3 * CONV_WIDTH
K_START = Q_START + ATTN_WIDTH
V_START = K_START + KV_WIDTH

LANES = 128
SUBLANES = 8
KV_REP = 2 * KV_WIDTH
NEG_BIG = -1e30
COND_ROWS = 24
VMEM_LIMIT = 56 * 1024 * 1024


def _params(n_axes):
    return pltpu.CompilerParams(dimension_semantics=("arbitrary",) * n_axes,
                                vmem_limit_bytes=VMEM_LIMIT)


def _const_spec(shape):
    nd = len(shape)
    return pl.BlockSpec(shape, lambda *_: (0,) * nd, pipeline_mode=pl.Buffered(1))


def _layer_spec(shape, layer):
    nd = len(shape)
    return pl.BlockSpec((None,) + tuple(shape), lambda *_: (layer,) + (0,) * nd, pipeline_mode=pl.Buffered(1))


def _norm_mod(x, g, shift, scale):
    ms = jnp.mean(x * x, axis=-1, keepdims=True)
    y = x * lax.rsqrt(ms + EPS)
    return (y * g) * (1.0 + scale) + shift


def _silu(x):
    return x * (1.0 / (1.0 + jnp.exp(-x)))


def _adaln_kernel(c_ref, w_ref, b_ref, o_ref):
    a = _silu(c_ref[...])
    w = w_ref[0]
    a_hi = a.astype(BF16)
    a_lo = (a - a_hi.astype(F32)).astype(BF16)
    w_hi = w.astype(BF16)
    w_lo = (w - w_hi.astype(F32)).astype(BF16)
    acc = jnp.dot(a_hi, w_hi, preferred_element_type=F32)
    acc += jnp.dot(a_lo, w_hi, preferred_element_type=F32)
    acc += jnp.dot(a_hi, w_lo, preferred_element_type=F32)
    o_ref[0] = acc + b_ref[0]


def _adaln(cond, w_ada, b_ada):
    tn = 1024
    nt = (6 * D_MODEL) // tn
    return pl.pallas_call(
        _adaln_kernel,
        grid=(DEPTH, nt),
        in_specs=[
            pl.BlockSpec((COND_ROWS, D_MODEL), lambda l, j: (0, 0)),
            pl.BlockSpec((1, D_MODEL, tn), lambda l, j: (l, 0, j)),
            pl.BlockSpec((1, 1, tn), lambda l, j: (l, 0, j)),
        ],
        out_specs=pl.BlockSpec((1, COND_ROWS, tn), lambda l, j: (l, 0, j)),
        out_shape=jax.ShapeDtypeStruct((DEPTH, COND_ROWS, 6 * D_MODEL), F32),
        compiler_params=_params(2),
        name="adaln",
    )(cond, w_ada, b_ada.reshape(DEPTH, 1, 6 * D_MODEL))


IN_WIDTH = V_START + KV_WIDTH
LOG2E = float(np.log2(np.e))
Q_SCALE = HEAD_DIM ** -0.5 * LOG2E
INPROJ_ROWS = 2048
INPROJ_SUB = 512


def _inproj_kernel(x_ref, sh_ref, sc_ref, g_ref, w_ref, cos_ref, sa_ref, sb_ref,
                   bgu_ref, q_ref, kk_ref, vv_ref):
    first_half = lax.broadcasted_iota(jnp.int32, (1, LANES), 1) < HEAD_DIM
    tm = x_ref.shape[1]
    sub = min(tm, INPROJ_SUB)
    for s0 in range(0, tm, sub):
        rs = slice(s0, s0 + sub)
        h = _norm_mod(x_ref[0, rs, :], g_ref[...], sh_ref[0], sc_ref[0]).astype(BF16)
        p = jnp.dot(h, w_ref[...], preferred_element_type=F32)
        bgu_ref[0, rs, :CONV_WIDTH] = p[:, :CONV_WIDTH].astype(BF16)
        bgu_ref[0, rs, CONV_WIDTH:] = (p[:, CONV_WIDTH:2 * CONV_WIDTH]
                                       * p[:, 2 * CONV_WIDTH:Q_START]).astype(BF16)
        cos = cos_ref[rs, :]
        sa = sa_ref[rs, :]
        sb = sb_ref[rs, :]

        def rope(t):
            return t * cos + pltpu.roll(t, LANES - 16, 1) * sa + pltpu.roll(t, 16, 1) * sb

        for j in range(ATTN_WIDTH // LANES):
            lo = Q_START + j * LANES
            q_ref[0, rs, j * LANES:(j + 1) * LANES] = (rope(p[:, lo:lo + LANES]) * Q_SCALE).astype(BF16)
        for t, ref in ((rope(p[:, K_START:V_START]), kk_ref), (p[:, V_START:], vv_ref)):
            swapped = pltpu.roll(t, HEAD_DIM, 1)
            ref[0, rs, :LANES] = jnp.where(first_half, t, swapped).astype(BF16)
            ref[0, rs, LANES:] = jnp.where(first_half, swapped, t).astype(BF16)


def _inproj(x, shift, scale, g, w_in, layer, cos, sa, sb, tm):
    b, n, _ = x.shape
    nt = n // tm
    tok = lambda w: pl.BlockSpec((1, tm, w), lambda i, bb: (bb, i, 0))
    mod = pl.BlockSpec((1, 1, D_MODEL), lambda i, bb: (bb, 0, 0))
    tab = pl.BlockSpec((tm, LANES), lambda i, bb: (i, 0))
    return pl.pallas_call(
        _inproj_kernel,
        grid=(nt, b),
        in_specs=[tok(D_MODEL), mod, mod, _const_spec((1, D_MODEL)),
                  _layer_spec((D_MODEL, IN_WIDTH), layer), tab, tab, tab],
        out_specs=[tok(D_MODEL), tok(ATTN_WIDTH), tok(KV_REP), tok(KV_REP)],
        out_shape=[jax.ShapeDtypeStruct((b, n, D_MODEL), BF16),
                   jax.ShapeDtypeStruct((b, n, ATTN_WIDTH), BF16),
                   jax.ShapeDtypeStruct((b, n, KV_REP), BF16),
                   jax.ShapeDtypeStruct((b, n, KV_REP), BF16)],
        compiler_params=_params(2),
        name="inproj",
    )(x, shift, scale, g, w_in, cos, sa, sb)


CONV_ROWS = 32
SM_ROWS = 16
SCORE_SLOTS = 2
MIX_ROWS = 1024


def _mixout_kernel(*refs, tile_rows, local):
    if local:
        (sink_ref, bgu_ref, bgu_p_ref, bgu_n_ref, q_ref,
         kk_ref, kk_p_ref, kk_n_ref, vv_ref, vv_p_ref, vv_n_ref, kkc_ref, vvc_ref,
         bias_mid_ref, bias_a_ref, bias_b_ref, cw_ref, o_ref, s_slots, e_slots) = refs
    else:
        (sink_ref, bgu_ref, bgu_p_ref, bgu_n_ref, q_ref,
         kkc_ref, vvc_ref, cw_ref, o_ref, s_slots, e_slots) = refs
    i = pl.program_id(0)
    n_tiles = pl.num_programs(0)
    r = tile_rows
    nb = r // BLOCK

    has_prev = (i > 0).astype(F32)
    has_next = (i < n_tiles - 1).astype(F32)
    cw = cw_ref[...]
    cr = CONV_ROWS
    pk = 2 * SUBLANES
    row = lax.broadcasted_iota(jnp.int32, (cr, CONV_WIDTH), 0)

    def conv_chunk(r0):
        bg = bgu_ref[0, r0:r0 + cr, :CONV_WIDTH].astype(F32)
        u = bgu_ref[0, r0:r0 + cr, CONV_WIDTH:].astype(F32)
        if r0 == 0:
            u_prev = bgu_p_ref[0, SUBLANES - 1:SUBLANES, CONV_WIDTH:].astype(F32) * has_prev
        else:
            u_prev = bgu_ref[0, r0 - pk:r0, CONV_WIDTH:].astype(F32)[pk - 1:pk]
        if r0 + cr == r:
            u_next = bgu_n_ref[0, 0:1, CONV_WIDTH:].astype(F32) * has_next
        else:
            u_next = bgu_ref[0, r0 + cr:r0 + cr + pk, CONV_WIDTH:].astype(F32)[0:1]
        u_up = jnp.where(row == 0, u_prev, pltpu.roll(u, 1, 0))
        u_dn = jnp.where(row == cr - 1, u_next, pltpu.roll(u, cr - 1, 0))
        o_ref[0, r0:r0 + cr, :CONV_WIDTH] = (bg * (u_up * cw[0:1] + u * cw[1:2] + u_dn * cw[2:3])).astype(BF16)

    lane = lax.broadcasted_iota(jnp.int32, (1, LANES), 1)
    first_half = lane < HEAD_DIM
    zero = jnp.zeros((), BF16)
    nt_dims = (((1,), (1,)), ((), ()))

    def split_k(t):
        return jnp.where(first_half, t, zero), jnp.where(first_half, zero, t)

    def split_v(t):
        va, vb = split_k(t)
        first = lax.broadcasted_iota(jnp.int32, t.shape, 1) < HEAD_DIM
        ones_a = jnp.where(first, 1.0, 0.0).astype(BF16)
        ones_b = jnp.where(first, 0.0, 1.0).astype(BF16)
        return jnp.concatenate([va, ones_a], axis=1), jnp.concatenate([vb, ones_b], axis=1)

    n_ctx = kkc_ref.shape[1]
    wc = 2 * n_ctx
    nl = 3 * BLOCK
    groups_per_pair = BLOCK // SM_ROWS
    cols_a = [slice(0, n_ctx)] + ([slice(wc, wc + nl)] if local else [])
    cols_b = [slice(n_ctx, wc)] + ([slice(wc + nl, wc + 2 * nl)] if local else [])

    heads = []
    for h in range(N_KV_HEADS):
        hs = slice(h * LANES, (h + 1) * LANES)
        hd = dict(kc_bd=jnp.concatenate(split_k(kkc_ref[0, :, hs]), axis=0),
                  vc_ext=jnp.concatenate(split_v(vvc_ref[0, :, hs]), axis=0),
                  sinks=[sink_ref[4 * h + i] * LOG2E for i in range(4)])
        if local:
            hd["k"] = split_k(jnp.concatenate([kk_p_ref[0, :, hs], kk_ref[0, :, hs], kk_n_ref[0, :, hs]], axis=0))
            hd["v"] = split_v(jnp.concatenate([vv_p_ref[0, :, hs], vv_ref[0, :, hs], vv_n_ref[0, :, hs]], axis=0))
        heads.append(hd)

    blocks = [(h, j) for h in range(N_KV_HEADS) for j in range(nb)]

    def scores(idx):
        h, j = blocks[idx]
        hd = heads[h]
        rs = slice(j * BLOCK, (j + 1) * BLOCK)
        q2 = jnp.concatenate([q_ref[0, rs, 2 * h * LANES:(2 * h + 1) * LANES],
                              q_ref[0, rs, (2 * h + 1) * LANES:(2 * h + 2) * LANES]], axis=0)
        s_ref = s_slots.at[idx % SCORE_SLOTS]
        s_ref[:, :wc] = lax.dot_general(q2, hd["kc_bd"], nt_dims, preferred_element_type=F32)
        if local:
            ws = slice(j * BLOCK, j * BLOCK + nl)
            kl_bd = jnp.concatenate([hd["k"][0][ws], hd["k"][1][ws]], axis=0)
            if nb == 1:
                bias = bias_a_ref[0] + bias_b_ref[0]
            elif j == 0:
                bias = bias_a_ref[0]
            elif j == nb - 1:
                bias = bias_b_ref[0]
            else:
                bias = bias_mid_ref[0]
            s_ref[:, wc:] = lax.dot_general(q2, kl_bd, nt_dims, preferred_element_type=F32) + bias

    def softmax_pv(idx):
        h, j = blocks[idx]
        hd = heads[h]
        s_ref = s_slots.at[idx % SCORE_SLOTS]
        e_ref = e_slots.at[idx % SCORE_SLOTS]
        sink_terms = []
        for g in range(2 * BLOCK // SM_ROWS):
            gr = slice(g * SM_ROWS, (g + 1) * SM_ROWS)
            pair = g // groups_per_pair
            terms = []
            for cols, sk in ((cols_a, hd["sinks"][2 * pair]), (cols_b, hd["sinks"][2 * pair + 1])):
                ss = [s_ref[gr, c] for c in cols]
                m = jnp.max(ss[0], axis=-1, keepdims=True)
                for t in ss[1:]:
                    m = jnp.maximum(m, jnp.max(t, axis=-1, keepdims=True))
                m = jnp.maximum(m, sk)
                for c, t in zip(cols, ss):
                    e_ref[gr, c] = jnp.exp2(t - m).astype(BF16)
                terms.append(jnp.exp2(sk - m))
            sink_terms.append(jnp.where(first_half, terms[0], terms[1]))
        o_ext = jnp.dot(e_ref[:, :wc], hd["vc_ext"], preferred_element_type=F32)
        if local:
            ws = slice(j * BLOCK, j * BLOCK + nl)
            vl_ext = jnp.concatenate([hd["v"][0][ws], hd["v"][1][ws]], axis=0)
            o_ext = o_ext + jnp.dot(e_ref[:, wc:], vl_ext, preferred_element_type=F32)
        o = o_ext[:, :LANES] / (o_ext[:, LANES:] + jnp.concatenate(sink_terms, axis=0))
        return jnp.concatenate([o[:BLOCK], o[BLOCK:]], axis=1)

    conv_starts = list(range(0, r, cr))
    per_block = -(-len(conv_starts) // len(blocks))
    outs = {}
    scores(0)
    for idx in range(len(blocks)):
        if idx + 1 < len(blocks):
            scores(idx + 1)
        for r0 in conv_starts[idx * per_block:(idx + 1) * per_block]:
            conv_chunk(r0)
        outs[blocks[idx]] = softmax_pv(idx)
    b_cols = [jnp.concatenate([outs[(h, j)] for j in range(nb)], axis=0) for h in range(N_KV_HEADS)]
    o_ref[0, :, CONV_WIDTH:] = jnp.concatenate(b_cols, axis=1).astype(BF16)


def _band_bias():
    rr = np.arange(2 * BLOCK)[:, None] % BLOCK
    kidx = np.arange(6 * BLOCK)[None, :] % (3 * BLOCK)
    band = (kidx >= rr) & (kidx <= rr + 2 * WINDOW)
    mid = band
    first = band & (kidx >= BLOCK)
    last = band & (kidx < 2 * BLOCK)
    return jnp.asarray(np.where(np.stack([mid, first, last]), 0.0, NEG_BIG), dtype=F32)


def _mixout(bgu, q, kk, vv, kkc, vvc, sink, conv_w, tile_rows, local):
    b, n, _ = bgu.shape
    r = tile_rows
    nt = n // r
    nctx = kkc.shape[1]
    tok = lambda w: pl.BlockSpec((1, r, w), lambda i, bb: (bb, i, 0))
    r8 = r // SUBLANES
    halo8_p = pl.BlockSpec((1, SUBLANES, D_MODEL), lambda i, bb: (bb, jnp.maximum(i * r8 - 1, 0), 0))
    halo8_n = pl.BlockSpec((1, SUBLANES, D_MODEL),
                           lambda i, bb: (bb, jnp.minimum((i + 1) * r8, n // SUBLANES - 1), 0))
    ctx = pl.BlockSpec((1, nctx, KV_REP), lambda i, bb: (bb, 0, 0))
    smem = pl.BlockSpec(memory_space=pltpu.SMEM)
    in_specs = [smem, tok(D_MODEL), halo8_p, halo8_n, tok(ATTN_WIDTH)]
    args = [sink, bgu, bgu, bgu, q]
    if local:
        rb = r // BLOCK
        halo_p = pl.BlockSpec((1, BLOCK, KV_REP), lambda i, bb: (bb, jnp.maximum(i * rb - 1, 0), 0))
        halo_n = pl.BlockSpec((1, BLOCK, KV_REP),
                              lambda i, bb: (bb, jnp.minimum((i + 1) * rb, n // BLOCK - 1), 0))
        bias = _band_bias()
        bshape = (1, 2 * BLOCK, 6 * BLOCK)
        in_specs += [tok(KV_REP), halo_p, halo_n, tok(KV_REP), halo_p, halo_n, ctx, ctx,
                     pl.BlockSpec(bshape, lambda i, bb: (0, 0, 0)),
                     pl.BlockSpec(bshape, lambda i, bb: (jnp.where(i == 0, 1, 0), 0, 0)),
                     pl.BlockSpec(bshape, lambda i, bb: (jnp.where(i == nt - 1, 2, 0), 0, 0))]
        args += [kk, kk, kk, vv, vv, vv, kkc, vvc, bias, bias, bias]
    else:
        in_specs += [ctx, ctx]
        args += [kkc, vvc]
    in_specs += [_const_spec((3, CONV_WIDTH))]
    args += [conv_w]
    score_cols = 2 * nctx + (6 * BLOCK if local else 0)
    return pl.pallas_call(
        functools.partial(_mixout_kernel, tile_rows=r, local=local),
        grid=(nt, b),
        in_specs=in_specs,
        out_specs=tok(D_MODEL),
        out_shape=jax.ShapeDtypeStruct((b, n, D_MODEL), BF16),
        scratch_shapes=[pltpu.VMEM((SCORE_SLOTS, 2 * BLOCK, score_cols), F32),
                        pltpu.VMEM((SCORE_SLOTS, 2 * BLOCK, score_cols), BF16)],
        compiler_params=_params(2),
        name="mixout_local" if local else "mixout_ctx",
    )(*args)


def _chan_dft_kernel(x_ref, sh_ref, sc_ref, g_ref, cs_ref, y1_ref, y2_ref):
    h = _norm_mod(x_ref[0], g_ref[...], sh_ref[0], sc_ref[0]).astype(BF16)
    fg = FOURIER_GROUP
    for gi in range(N_FOURIER_GROUPS):
        yc = jnp.dot(h[:, gi * fg:(gi + 1) * fg], cs_ref[...], preferred_element_type=F32)
        y1_ref[0, :, gi * fg:(gi + 1) * fg] = yc[:, :fg].astype(BF16)
        y2_ref[0, :, gi * fg:(gi + 1) * fg] = yc[:, fg:].astype(BF16)


def _chan_dft(x, shift, scale, g, cs, tm):
    b, n, _ = x.shape
    tok = pl.BlockSpec((1, tm, D_MODEL), lambda i, bb: (bb, i, 0))
    mod = pl.BlockSpec((1, 1, D_MODEL), lambda i, bb: (bb, 0, 0))
    return pl.pallas_call(
        _chan_dft_kernel,
        grid=(n // tm, b),
        in_specs=[tok, mod, mod, _const_spec((1, D_MODEL)), _const_spec((FOURIER_GROUP, 2 * FOURIER_GROUP))],
        out_specs=[tok, tok],
        out_shape=[jax.ShapeDtypeStruct((b, n, D_MODEL), BF16)] * 2,
        compiler_params=_params(2),
        name="chan_dft",
    )(x, shift, scale, g, cs)


def _seq_dft_kernel(cn_ref, sn_ref, y1_ref, y2_ref, o_ref):
    z = jnp.dot(cn_ref[...], y1_ref[0], preferred_element_type=F32)
    z = z + jnp.dot(sn_ref[...], y2_ref[0], preferred_element_type=F32)
    o_ref[0] = z.astype(BF16)


def _seq_dft(cn, nsn, y1, y2, tm):
    b, n, _ = y1.shape
    tok = pl.BlockSpec((1, tm, D_MODEL), lambda bb, i: (bb, i, 0))
    mat = pl.BlockSpec((tm, n), lambda bb, i: (i, 0))
    full = pl.BlockSpec((1, n, D_MODEL), lambda bb, i: (bb, 0, 0), pipeline_mode=pl.Buffered(1))
    return pl.pallas_call(
        _seq_dft_kernel,
        grid=(b, n // tm),
        in_specs=[mat, mat, full, full],
        out_specs=tok,
        out_shape=jax.ShapeDtypeStruct((b, n, D_MODEL), BF16),
        compiler_params=_params(2),
        name="seq_dft",
    )(cn, nsn, y1, y2)


def _dft_mats(n, scale):
    j = np.arange(n)
    ang = ((j[:, None] * j[None, :]) % n) * (2.0 * np.pi / n)
    return (np.cos(ang) * scale).astype(np.float32), (np.sin(ang) * scale).astype(np.float32)


FFT_RADIX = 16
FFT_INNER = 256
FFT_N = FFT_RADIX * FFT_INNER
PERM_TILE = 512
PERM_B = PERM_TILE // FFT_RADIX
PERM_STEP = 4 * PERM_TILE


def _chan_dft_perm_kernel(x_ref, sh_ref, sc_ref, g_ref, p_ref, cs_ref, y_ref):
    fg = FOURIER_GROUP
    for t in range(x_ref.shape[1] // PERM_TILE):
        h = _norm_mod(x_ref[0, t * PERM_TILE:(t + 1) * PERM_TILE, :], g_ref[...], sh_ref[0], sc_ref[0]).astype(BF16)
        hp = jnp.dot(p_ref[...], h, preferred_element_type=F32).astype(BF16)
        for gi in range(N_FOURIER_GROUPS):
            gs = slice(gi * fg, (gi + 1) * fg)
            yc = jnp.dot(hp[:, gs], cs_ref[...], preferred_element_type=F32)
            for a in range(FFT_RADIX):
                rs = slice(a * PERM_B, (a + 1) * PERM_B)
                ws = slice(t * PERM_B, (t + 1) * PERM_B)
                y_ref[0, gi, a, 0, ws, :] = yc[rs, :fg].astype(BF16)
                y_ref[0, gi, a, 1, ws, :] = yc[rs, fg:].astype(BF16)


def _chan_dft_perm(x, shift, scale, g, cs):
    b, n, _ = x.shape
    r_out = np.arange(PERM_TILE)
    perm = np.zeros((PERM_TILE, PERM_TILE), np.float32)
    perm[r_out, FFT_RADIX * (r_out % PERM_B) + r_out // PERM_B] = 1.0
    tok = pl.BlockSpec((1, PERM_STEP, D_MODEL), lambda i, bb: (bb, i, 0))
    mod = pl.BlockSpec((1, 1, D_MODEL), lambda i, bb: (bb, 0, 0))
    return pl.pallas_call(
        _chan_dft_perm_kernel,
        grid=(n // PERM_STEP, b),
        in_specs=[tok, mod, mod, _const_spec((1, D_MODEL)), _const_spec((PERM_TILE, PERM_TILE)),
                  _const_spec((FOURIER_GROUP, 2 * FOURIER_GROUP))],
        out_specs=pl.BlockSpec((1, N_FOURIER_GROUPS, FFT_RADIX, 2, PERM_STEP // FFT_RADIX, FOURIER_GROUP),
                               lambda i, bb: (bb, 0, 0, 0, i, 0)),
        out_shape=jax.ShapeDtypeStruct((b, N_FOURIER_GROUPS, FFT_RADIX, 2, n // FFT_RADIX, FOURIER_GROUP), BF16),
        compiler_params=_params(2),
        name="chan_dft_perm",
    )(x, shift, scale, g, jnp.asarray(perm, BF16), cs)


_C1 = float(np.cos(np.pi / 8))
_C2 = float(np.cos(np.pi / 4))
_C3 = float(np.cos(3 * np.pi / 8))


def _dft16_real(ur, ui):
    p = [ur[0]] + [ur[a] + ur[16 - a] for a in range(1, 8)] + [ur[8]]
    q = [p[a] + p[8 - a] for a in range(4)] + [p[4]]
    r = [p[a] - p[8 - a] for a in range(4)]
    s0, s1, s2 = q[0] + q[4], q[1] + q[3], q[2]
    t0, t1 = q[0] - q[4], q[1] - q[3]
    ca = [None] * 9
    ca[0] = s0 + s1 + s2
    ca[4] = s0 - s2
    ca[8] = s0 - s1 + s2
    ca[2] = t0 + _C2 * t1
    ca[6] = t0 - _C2 * t1
    e0, e1 = r[0] + _C2 * r[2], r[0] - _C2 * r[2]
    f, g = _C1 * r[1] + _C3 * r[3], _C3 * r[1] - _C1 * r[3]
    ca[1], ca[7], ca[3], ca[5] = e0 + f, e0 - f, e1 + g, e1 - g
    pp = [None] + [ui[a] - ui[16 - a] for a in range(1, 8)]
    qq = [None] + [pp[a] + pp[8 - a] for a in range(1, 4)] + [pp[4]]
    rr = [None] + [pp[a] - pp[8 - a] for a in range(1, 4)]
    sb = [None] * 8
    e0, e1 = _C2 * qq[2] + qq[4], _C2 * qq[2] - qq[4]
    f, g = _C3 * qq[1] + _C1 * qq[3], _C1 * qq[1] - _C3 * qq[3]
    sb[1], sb[7], sb[3], sb[5] = f + e0, f - e0, g + e1, g - e1
    w = _C2 * (rr[1] + rr[3])
    sb[2], sb[6], sb[4] = w + rr[2], w - rr[2], rr[1] - rr[3]
    out = [None] * 16
    out[0], out[8] = ca[0], ca[8]
    for c in range(1, 8):
        out[c] = ca[c] + sb[c]
        out[16 - c] = ca[c] - sb[c]
    return out


def _seq_fft_kernel(y_ref, l_ref, o_ref, u_even_ref, u_odd_ref):
    t = pl.program_id(0)

    @pl.when(t == 0)
    def _():
        u_odd_ref[...] = jnp.zeros(u_odd_ref.shape, F32)

    def step(cur_ref, prev_ref):
        rt = 2 * SUBLANES
        for i in range(FFT_INNER // rt):
            r0 = i * rt
            for lh in range(FOURIER_GROUP // LANES):
                ls = slice(lh * LANES, (lh + 1) * LANES)
                ur = [prev_ref[a, r0:r0 + rt, ls] for a in range(FFT_RADIX)]
                ui = [prev_ref[a, FFT_INNER + r0:FFT_INNER + r0 + rt, ls] for a in range(FFT_RADIX)]
                out = _dft16_real(ur, ui)
                for c in range(FFT_RADIX):
                    o_ref[0, c * FFT_INNER + r0:c * FFT_INNER + r0 + rt, ls] = out[c].astype(BF16)
        for a in range(FFT_RADIX):
            u = jnp.dot(l_ref[a, :, :FFT_INNER], y_ref[a, 0], preferred_element_type=F32)
            cur_ref[a] = u + jnp.dot(l_ref[a, :, FFT_INNER:], y_ref[a, 1], preferred_element_type=F32)

    pl.when(t % 2 == 0)(lambda: step(u_even_ref, u_odd_ref))
    pl.when(t % 2 == 1)(lambda: step(u_odd_ref, u_even_ref))


def _seq_fft(ycat, lmat):
    b = ycat.shape[0]
    ng = N_FOURIER_GROUPS
    items = b * ng

    def y_map(t):
        tt = jnp.minimum(t, items - 1)
        return (tt // ng, tt % ng, 0, 0, 0, 0)

    def o_map(t):
        tp = jnp.maximum(t - 1, 0)
        return (tp // ng, 0, tp % ng)

    return pl.pallas_call(
        _seq_fft_kernel,
        grid=(items + 1,),
        in_specs=[pl.BlockSpec((None, None, FFT_RADIX, 2, FFT_INNER, FOURIER_GROUP), y_map),
                  _const_spec((FFT_RADIX, 2 * FFT_INNER, 2 * FFT_INNER))],
        out_specs=pl.BlockSpec((1, FFT_N, FOURIER_GROUP), o_map),
        out_shape=jax.ShapeDtypeStruct((b, FFT_N, D_MODEL), BF16),
        scratch_shapes=[pltpu.VMEM((FFT_RADIX, 2 * FFT_INNER, FOURIER_GROUP), F32)] * 2,
        compiler_params=_params(1),
        name="seq_fft",
    )(ycat, lmat)


def _fft_stage_mats():
    a = np.arange(FFT_RADIX)[:, None, None]
    d = np.arange(FFT_INNER)[None, :, None]
    bb = np.arange(FFT_INNER)[None, None, :]
    ang = ((d * (a + FFT_RADIX * bb)) % FFT_N) * (2.0 * np.pi / FFT_N)
    c = np.cos(ang) * (FFT_N ** -0.5)
    s = np.sin(ang) * (FFT_N ** -0.5)
    top = np.concatenate([c, -s], axis=2)
    bot = np.concatenate([-s, -c], axis=2)
    return jnp.asarray(np.concatenate([top, bot], axis=1).astype(np.float32), BF16)


FF_CHUNK = 256
TAIL_ROWS = 1024


def _tail_kernel(*refs, final):
    if final:
        (x_ref, m_ref, g1_ref, wm_ref, sh_ref, sc_ref, gate_ref, g_ref, wg_ref, wu_ref, wd_ref,
         fg_ref, o_ref, acc_ref) = refs
    else:
        (x_ref, m_ref, g1_ref, wm_ref, sh_ref, sc_ref, gate_ref, g_ref, wg_ref, wu_ref, wd_ref,
         o_ref, acc_ref) = refs
    x = x_ref[0] + g1_ref[0] * jnp.dot(m_ref[0], wm_ref[...], preferred_element_type=F32)
    h = _norm_mod(x, g_ref[...], sh_ref[0], sc_ref[0]).astype(BF16)
    for c in range(D_FF // FF_CHUNK):
        cs = slice(c * FF_CHUNK, (c + 1) * FF_CHUNK)
        gt = jnp.dot(h, wg_ref[:, cs], preferred_element_type=F32)
        up = jnp.dot(h, wu_ref[:, cs], preferred_element_type=F32)
        a = (_silu(gt) * up).astype(BF16)
        part = jnp.dot(a, wd_ref[cs, :], preferred_element_type=F32)
        if c == 0:
            acc_ref[...] = part
        else:
            acc_ref[...] += part
    y = x + gate_ref[0] * acc_ref[...]
    if final:
        ms = jnp.mean(y * y, axis=-1, keepdims=True)
        y = (y * lax.rsqrt(ms + EPS)) * fg_ref[...]
    o_ref[0] = y


def _tail(x, m, gate1, wm, wm_layer, shift, scale, gate, g, wg, wu, wd, layer, tm, final_g=None):
    b, n, _ = x.shape
    tok = pl.BlockSpec((1, tm, D_MODEL), lambda i, bb: (bb, i, 0))
    mod = pl.BlockSpec((1, 1, D_MODEL), lambda i, bb: (bb, 0, 0))
    in_specs = [tok, tok, mod, _layer_spec((D_MODEL, D_MODEL), wm_layer), mod, mod, mod, _const_spec((1, D_MODEL)),
                _layer_spec((D_MODEL, D_FF), layer), _layer_spec((D_MODEL, D_FF), layer),
                _layer_spec((D_FF, D_MODEL), layer)]
    args = [x, m, gate1, wm, shift, scale, gate, g, wg, wu, wd]
    if final_g is not None:
        in_specs.append(_const_spec((1, D_MODEL)))
        args.append(final_g)
    return pl.pallas_call(
        functools.partial(_tail_kernel, final=final_g is not None),
        grid=(n // tm, b),
        in_specs=in_specs,
        out_specs=tok,
        out_shape=jax.ShapeDtypeStruct((b, n, D_MODEL), F32),
        scratch_shapes=[pltpu.VMEM((tm, D_MODEL), F32)],
        compiler_params=_params(2),
        name="tail",
    )(*args)


def _rope_lane_tables(n):
    rows = n // GRID_W
    r = np.repeat(np.arange(rows), GRID_W).astype(np.float64)
    col = np.tile(np.arange(GRID_W), rows).astype(np.float64)
    quarter = HEAD_DIM // 4
    inv = ROPE_THETA ** (-np.arange(quarter, dtype=np.float64) / quarter)
    ang_r = r[:, None] * inv
    ang_c = col[:, None] * inv
    cr, sr, cc, sc = np.cos(ang_r), np.sin(ang_r), np.cos(ang_c), np.sin(ang_c)
    z = np.zeros_like(sr)
    cos = np.concatenate([cr, cr, cc, cc], axis=1)
    sa = np.concatenate([-sr, z, -sc, z], axis=1)
    sb = np.concatenate([z, sr, z, sc], axis=1)
    rep = lambda t: jnp.asarray(np.concatenate([t, t], axis=1).astype(np.float32))
    return rep(cos), rep(sa), rep(sb)


def _identity_lane_tables(n):
    one, zero = np.ones((n, LANES), np.float32), np.zeros((n, LANES), np.float32)
    return jnp.asarray(one), jnp.asarray(zero), jnp.asarray(zero)


def kernel(x, c, ctx, c_ctx, w_ada, b_ada, norm1_g, norm2_g, w_in, conv_w, sink,
           w_mix_out, w_fourier_out, w_ffn_gate, w_ffn_up, w_ffn_down, final_g):
    b, n, d = x.shape
    L = ctx.shape[1]
    nc = b * L
    tm = 512

    cond = jnp.concatenate([c, c_ctx[None, :], jnp.zeros((COND_ROWS - b - 1, d), F32)], axis=0)
    mods = _adaln(cond, w_ada, b_ada)

    def split(l, lo, hi):
        m = mods[l, lo:hi]
        return [m[:, None, k * d:(k + 1) * d] for k in range(6)]

    rope_lat = _rope_lane_tables(n)
    rope_ctx = _identity_lane_tables(nc)
    w_in_b, w_mix_b, w_fo_b = w_in.astype(BF16), w_mix_out.astype(BF16), w_fourier_out.astype(BF16)
    wg_b, wu_b, wd_b = w_ffn_gate.astype(BF16), w_ffn_up.astype(BF16), w_ffn_down.astype(BF16)
    cc, sc_ = _dft_mats(FOURIER_GROUP, FOURIER_GROUP ** -0.5)
    cs_chan = jnp.asarray(np.concatenate([cc, sc_], axis=1), BF16)
    seq_mats = {}
    for m in {n, L} - {FFT_N}:
        cm, sm = _dft_mats(m, m ** -0.5)
        seq_mats[m] = (jnp.asarray(cm, BF16), jnp.asarray(-sm, BF16))
    fft_l = _fft_stage_mats() if n == FFT_N else None

    xc = ctx.reshape(1, nc, d)
    per_seq = lambda t: t.reshape(b, L, t.shape[-1])
    for l in range(DEPTH):
        ctx_after = any(j % 2 == 0 for j in range(l + 1, DEPTH))
        ctx_here = (l % 2 == 0) or ctx_after
        sh1, sc1, g1, sh2, sc2, g2 = split(l, 0, b)
        if ctx_here:
            csh1, csc1, cg1, csh2, csc2, cg2 = split(l, b, b + 1)
        n1 = norm1_g[l][None, :]
        n2 = norm2_g[l][None, :]
        if l % 2 == 0:
            e = l // 2
            wm, wm_layer = w_mix_b, e
            bgu_c, q_c, kk_c, vv_c = _inproj(xc, csh1, csc1, n1, w_in_b, e, *rope_ctx, tm=min(nc, INPROJ_ROWS))
            kk_c, vv_c = per_seq(kk_c), per_seq(vv_c)
            bgu, q, kk, vv = _inproj(x, sh1, sc1, n1, w_in_b, e, *rope_lat, tm=min(n, INPROJ_ROWS))
            mix = _mixout(bgu, q, kk, vv, kk_c, vv_c, sink[e], conv_w[e], tile_rows=min(n, MIX_ROWS), local=True)
            if ctx_after:
                mix_c = _mixout(per_seq(bgu_c), per_seq(q_c), None, None, kk_c, vv_c, sink[e], conv_w[e],
                                tile_rows=L, local=False)
        else:
            wm, wm_layer = w_fo_b, l // 2
            if n == FFT_N:
                mix = _seq_fft(_chan_dft_perm(x, sh1, sc1, n1, cs_chan), fft_l)
            else:
                mix = _seq_dft(*seq_mats[n], *_chan_dft(x, sh1, sc1, n1, cs_chan, tm=tm), tm=256)
            if ctx_after:
                y1c, y2c = _chan_dft(xc, csh1, csc1, n1, cs_chan, tm=min(nc, tm))
                mix_c = _seq_dft(*seq_mats[L], per_seq(y1c), per_seq(y2c), tm=L)
        x = _tail(x, mix, g1, wm, wm_layer, sh2, sc2, g2, n2, wg_b, wu_b, wd_b, l, tm=min(n, TAIL_ROWS),
                  final_g=final_g[None, :] if l == DEPTH - 1 else None)
        if ctx_after:
            xc = _tail(xc, mix_c.reshape(1, nc, d), cg1, wm, wm_layer, csh2, csc2, cg2, n2, wg_b, wu_b, wd_b, l,
                       tm=min(nc, TAIL_ROWS))
    return x
```

```python
import functools

import numpy as np
import jax
import jax.numpy as jnp
from jax import lax
from jax.experimental import pallas as pl
from jax.experimental.pallas import tpu as pltpu

F32 = jnp.float32
BF16 = jnp.bfloat16

D_MODEL = 1024
DEPTH = 4
GRID_W = 64
HEAD_DIM = 64
N_Q_HEADS = 8
N_KV_HEADS = 2
ATTN_WIDTH = N_Q_HEADS * HEAD_DIM
KV_WIDTH = N_KV_HEADS * HEAD_DIM
CONV_WIDTH = D_MODEL - ATTN_WIDTH
WINDOW = 128
BLOCK = 128
ROPE_THETA = 10000.0
N_FOURIER_GROUPS = 4
FOURIER_GROUP = D_MODEL // N_FOURIER_GROUPS
D_FF = 2816
EPS = 1e-6
Q_START = 3 * CONV_WIDTH
K_START = Q_START + ATTN_WIDTH
V_START = K_START + KV_WIDTH

LANES = 128
SUBLANES = 8
KV_REP = 2 * KV_WIDTH
NEG_BIG = -1e30
COND_ROWS = 24
VMEM_LIMIT = 56 * 1024 * 1024


def _params(n_axes):
    return pltpu.CompilerParams(dimension_semantics=("arbitrary",) * n_axes,
                                vmem_limit_bytes=VMEM_LIMIT)


def _const_spec(shape):
    nd = len(shape)
    return pl.BlockSpec(shape, lambda *_: (0,) * nd, pipeline_mode=pl.Buffered(1))


def _layer_spec(shape, layer):
    nd = len(shape)
    return pl.BlockSpec((None,) + tuple(shape), lambda *_: (layer,) + (0,) * nd, pipeline_mode=pl.Buffered(1))


def _norm_mod(x, g, shift, scale):
    ms = jnp.mean(x * x, axis=-1, keepdims=True)
    y = x * lax.rsqrt(ms + EPS)
    return (y * g) * (1.0 + scale) + shift


def _silu(x):
    return x * (1.0 / (1.0 + jnp.exp(-x)))


def _adaln_kernel(c_ref, w_ref, b_ref, o_ref):
    a = _silu(c_ref[...])
    w = w_ref[0]
    a_hi = a.astype(BF16)
    a_lo = (a - a_hi.astype(F32)).astype(BF16)
    w_hi = w.astype(BF16)
    w_lo = (w - w_hi.astype(F32)).astype(BF16)
    acc = jnp.dot(a_hi, w_hi, preferred_element_type=F32)
    acc += jnp.dot(a_lo, w_hi, preferred_element_type=F32)
    acc += jnp.dot(a_hi, w_lo, preferred_element_type=F32)
    o_ref[0] = acc + b_ref[0]


def _adaln(cond, w_ada, b_ada):
    tn = 1024
    nt = (6 * D_MODEL) // tn
    return pl.pallas_call(
        _adaln_kernel,
        grid=(DEPTH, nt),
        in_specs=[
            pl.BlockSpec((COND_ROWS, D_MODEL), lambda l, j: (0, 0)),
            pl.BlockSpec((1, D_MODEL, tn), lambda l, j: (l, 0, j)),
            pl.BlockSpec((1, 1, tn), lambda l, j: (l, 0, j)),
        ],
        out_specs=pl.BlockSpec((1, COND_ROWS, tn), lambda l, j: (l, 0, j)),
        out_shape=jax.ShapeDtypeStruct((DEPTH, COND_ROWS, 6 * D_MODEL), F32),
        compiler_params=_params(2),
        name="adaln",
    )(cond, w_ada, b_ada.reshape(DEPTH, 1, 6 * D_MODEL))


IN_WIDTH = V_START + KV_WIDTH
LOG2E = float(np.log2(np.e))
Q_SCALE = HEAD_DIM ** -0.5 * LOG2E
INPROJ_ROWS = 2048
INPROJ_SUB = 512


def _inproj_kernel(x_ref, sh_ref, sc_ref, g_ref, w_ref, cos_ref, sa_ref, sb_ref,
                   bgu_ref, q_ref, kk_ref, vv_ref):
    first_half = lax.broadcasted_iota(jnp.int32, (1, LANES), 1) < HEAD_DIM
    tm = x_ref.shape[1]
    sub = min(tm, INPROJ_SUB)
    for s0 in range(0, tm, sub):
        rs = slice(s0, s0 + sub)
        h = _norm_mod(x_ref[0, rs, :], g_ref[...], sh_ref[0], sc_ref[0]).astype(BF16)
        p = jnp.dot(h, w_ref[...], preferred_element_type=F32)
        bgu_ref[0, rs, :CONV_WIDTH] = p[:, :CONV_WIDTH].astype(BF16)
        bgu_ref[0, rs, CONV_WIDTH:] = (p[:, CONV_WIDTH:2 * CONV_WIDTH]
                                       * p[:, 2 * CONV_WIDTH:Q_START]).astype(BF16)
        cos = cos_ref[rs, :]
        sa = sa_ref[rs, :]
        sb = sb_ref[rs, :]

        def rope(t):
            return t * cos + pltpu.roll(t, LANES - 16, 1) * sa + pltpu.roll(t, 16, 1) * sb

        for j in range(ATTN_WIDTH // LANES):
            lo = Q_START + j * LANES
            q_ref[0, rs, j * LANES:(j + 1) * LANES] = (rope(p[:, lo:lo + LANES]) * Q_SCALE).astype(BF16)
        for t, ref in ((rope(p[:, K_START:V_START]), kk_ref), (p[:, V_START:], vv_ref)):
            swapped = pltpu.roll(t, HEAD_DIM, 1)
            ref[0, rs, :LANES] = jnp.where(first_half, t, swapped).astype(BF16)
            ref[0, rs, LANES:] = jnp.where(first_half, swapped, t).astype(BF16)


def _inproj(x, shift, scale, g, w_in, layer, cos, sa, sb, tm):
    b, n, _ = x.shape
    nt = n // tm
    tok = lambda w: pl.BlockSpec((1, tm, w), lambda i, bb: (bb, i, 0))
    mod = pl.BlockSpec((1, 1, D_MODEL), lambda i, bb: (bb, 0, 0))
    tab = pl.BlockSpec((tm, LANES), lambda i, bb: (i, 0))
    return pl.pallas_call(
        _inproj_kernel,
        grid=(nt, b),
        in_specs=[tok(D_MODEL), mod, mod, _const_spec((1, D_MODEL)),
                  _layer_spec((D_MODEL, IN_WIDTH), layer), tab, tab, tab],
        out_specs=[tok(D_MODEL), tok(ATTN_WIDTH), tok(KV_REP), tok(KV_REP)],
        out_shape=[jax.ShapeDtypeStruct((b, n, D_MODEL), BF16),
                   jax.ShapeDtypeStruct((b, n, ATTN_WIDTH), BF16),
                   jax.ShapeDtypeStruct((b, n, KV_REP), BF16),
                   jax.ShapeDtypeStruct((b, n, KV_REP), BF16)],
        compiler_params=_params(2),
        name="inproj",
    )(x, shift, scale, g, w_in, cos, sa, sb)


CONV_ROWS = 32
SM_ROWS = 16
SCORE_SLOTS = 2
MIX_ROWS = 1024


def _mixout_kernel(*refs, tile_rows, local):
    if local:
        (sink_ref, bgu_ref, bgu_p_ref, bgu_n_ref, q_ref,
         kk_ref, kk_p_ref, kk_n_ref, vv_ref, vv_p_ref, vv_n_ref, kkc_ref, vvc_ref,
         bias_mid_ref, bias_a_ref, bias_b_ref, cw_ref, o_ref, s_slots, e_slots) = refs
    else:
        (sink_ref, bgu_ref, bgu_p_ref, bgu_n_ref, q_ref,
         kkc_ref, vvc_ref, cw_ref, o_ref, s_slots, e_slots) = refs
    i = pl.program_id(0)
    n_tiles = pl.num_programs(0)
    r = tile_rows
    nb = r // BLOCK

    has_prev = (i > 0).astype(F32)
    has_next = (i < n_tiles - 1).astype(F32)
    cw = cw_ref[...]
    cr = CONV_ROWS
    pk = 2 * SUBLANES
    row = lax.broadcasted_iota(jnp.int32, (cr, CONV_WIDTH), 0)

    def conv_chunk(r0):
        bg = bgu_ref[0, r0:r0 + cr, :CONV_WIDTH].astype(F32)
        u = bgu_ref[0, r0:r0 + cr, CONV_WIDTH:].astype(F32)
        if r0 == 0:
            u_prev = bgu_p_ref[0, SUBLANES - 1:SUBLANES, CONV_WIDTH:].astype(F32) * has_prev
        else:
            u_prev = bgu_ref[0, r0 - pk:r0, CONV_WIDTH:].astype(F32)[pk - 1:pk]
        if r0 + cr == r:
            u_next = bgu_n_ref[0, 0:1, CONV_WIDTH:].astype(F32) * has_next
        else:
            u_next = bgu_ref[0, r0 + cr:r0 + cr + pk, CONV_WIDTH:].astype(F32)[0:1]
        u_up = jnp.where(row == 0, u_prev, pltpu.roll(u, 1, 0))
        u_dn = jnp.where(row == cr - 1, u_next, pltpu.roll(u, cr - 1, 0))
        o_ref[0, r0:r0 + cr, :CONV_WIDTH] = (bg * (u_up * cw[0:1] + u * cw[1:2] + u_dn * cw[2:3])).astype(BF16)

    lane = lax.broadcasted_iota(jnp.int32, (1, LANES), 1)
    first_half = lane < HEAD_DIM
    zero = jnp.zeros((), BF16)
    nt_dims = (((1,), (1,)), ((), ()))

    def split_k(t):
        return jnp.where(first_half, t, zero), jnp.where(first_half, zero, t)

    def split_v(t):
        va, vb = split_k(t)
        first = lax.broadcasted_iota(jnp.int32, t.shape, 1) < HEAD_DIM
        ones_a = jnp.where(first, 1.0, 0.0).astype(BF16)
        ones_b = jnp.where(first, 0.0, 1.0).astype(BF16)
        return jnp.concatenate([va, ones_a], axis=1), jnp.concatenate([vb, ones_b], axis=1)

    n_ctx = kkc_ref.shape[1]
    wc = 2 * n_ctx
    nl = 3 * BLOCK
    groups_per_pair = BLOCK // SM_ROWS
    cols_a = [slice(0, n_ctx)] + ([slice(wc, wc + nl)] if local else [])
    cols_b = [slice(n_ctx, wc)] + ([slice(wc + nl, wc + 2 * nl)] if local else [])

    heads = []
    for h in range(N_KV_HEADS):
        hs = slice(h * LANES, (h + 1) * LANES)
        hd = dict(kc_bd=jnp.concatenate(split_k(kkc_ref[0, :, hs]), axis=0),
                  vc_ext=jnp.concatenate(split_v(vvc_ref[0, :, hs]), axis=0),
                  sinks=[sink_ref[4 * h + i] * LOG2E for i in range(4)])
        if local:
            hd["k"] = split_k(jnp.concatenate([kk_p_ref[0, :, hs], kk_ref[0, :, hs], kk_n_ref[0, :, hs]], axis=0))
            hd["v"] = split_v(jnp.concatenate([vv_p_ref[0, :, hs], vv_ref[0, :, hs], vv_n_ref[0, :, hs]], axis=0))
        heads.append(hd)

    blocks = [(h, j) for h in range(N_KV_HEADS) for j in range(nb)]

    def scores(idx):
        h, j = blocks[idx]
        hd = heads[h]
        rs = slice(j * BLOCK, (j + 1) * BLOCK)
        q2 = jnp.concatenate([q_ref[0, rs, 2 * h * LANES:(2 * h + 1) * LANES],
                              q_ref[0, rs, (2 * h + 1) * LANES:(2 * h + 2) * LANES]], axis=0)
        s_ref = s_slots.at[idx % SCORE_SLOTS]
        s_ref[:, :wc] = lax.dot_general(q2, hd["kc_bd"], nt_dims, preferred_element_type=F32)
        if local:
            ws = slice(j * BLOCK, j * BLOCK + nl)
            kl_bd = jnp.concatenate([hd["k"][0][ws], hd["k"][1][ws]], axis=0)
            if nb == 1:
                bias = bias_a_ref[0] + bias_b_ref[0]
            elif j == 0:
                bias = bias_a_ref[0]
            elif j == nb - 1:
                bias = bias_b_ref[0]
            else:
                bias = bias_mid_ref[0]
            s_loc = lax.dot_general(q2, kl_bd, nt_dims, preferred_element_type=F32)
            for c0 in range(0, 2 * nl, BLOCK):
                seg = s_loc[:, c0:c0 + BLOCK]
                if (c0 // BLOCK) % 3 != 1:
                    seg = seg + bias[:, c0:c0 + BLOCK]
                s_ref[:, wc + c0:wc + c0 + BLOCK] = seg

    def softmax_pv(idx):
        h, j = blocks[idx]
        hd = heads[h]
        s_ref = s_slots.at[idx % SCORE_SLOTS]
        e_ref = e_slots.at[idx % SCORE_SLOTS]
        sink_terms = []
        for g in range(2 * BLOCK // SM_ROWS):
            gr = slice(g * SM_ROWS, (g + 1) * SM_ROWS)
            pair = g // groups_per_pair
            terms = []
            for cols, sk in ((cols_a, hd["sinks"][2 * pair]), (cols_b, hd["sinks"][2 * pair + 1])):
                ss = [s_ref[gr, c] for c in cols]
                m = jnp.max(ss[0], axis=-1, keepdims=True)
                for t in ss[1:]:
                    m = jnp.maximum(m, jnp.max(t, axis=-1, keepdims=True))
                m = jnp.maximum(m, sk)
                for c, t in zip(cols, ss):
                    e_ref[gr, c] = jnp.exp2(t - m).astype(BF16)
                terms.append(jnp.exp2(sk - m))
            sink_terms.append(jnp.where(first_half, terms[0], terms[1]))
        o_ext = jnp.dot(e_ref[:, :wc], hd["vc_ext"], preferred_element_type=F32)
        if local:
            ws = slice(j * BLOCK, j * BLOCK + nl)
            vl_ext = jnp.concatenate([hd["v"][0][ws], hd["v"][1][ws]], axis=0)
            o_ext = o_ext + jnp.dot(e_ref[:, wc:], vl_ext, preferred_element_type=F32)
        o = o_ext[:, :LANES] / (o_ext[:, LANES:] + jnp.concatenate(sink_terms, axis=0))
        return jnp.concatenate([o[:BLOCK], o[BLOCK:]], axis=1)

    conv_starts = list(range(0, r, cr))
    per_block = -(-len(conv_starts) // len(blocks))
    outs = {}
    scores(0)
    for idx in range(len(blocks)):
        if idx + 1 < len(blocks):
            scores(idx + 1)
        for r0 in conv_starts[idx * per_block:(idx + 1) * per_block]:
            conv_chunk(r0)
        outs[blocks[idx]] = softmax_pv(idx)
    b_cols = [jnp.concatenate([outs[(h, j)] for j in range(nb)], axis=0) for h in range(N_KV_HEADS)]
    o_ref[0, :, CONV_WIDTH:] = jnp.concatenate(b_cols, axis=1).astype(BF16)


def _band_bias():
    rr = np.arange(2 * BLOCK)[:, None] % BLOCK
    kidx = np.arange(6 * BLOCK)[None, :] % (3 * BLOCK)
    band = (kidx >= rr) & (kidx <= rr + 2 * WINDOW)
    mid = band
    first = band & (kidx >= BLOCK)
    last = band & (kidx < 2 * BLOCK)
    return jnp.asarray(np.where(np.stack([mid, first, last]), 0.0, NEG_BIG), dtype=F32)


def _mixout(bgu, q, kk, vv, kkc, vvc, sink, conv_w, tile_rows, local):
    b, n, _ = bgu.shape
    r = tile_rows
    nt = n // r
    nctx = kkc.shape[1]
    tok = lambda w: pl.BlockSpec((1, r, w), lambda i, bb: (bb, i, 0))
    r8 = r // SUBLANES
    halo8_p = pl.BlockSpec((1, SUBLANES, D_MODEL), lambda i, bb: (bb, jnp.maximum(i * r8 - 1, 0), 0))
    halo8_n = pl.BlockSpec((1, SUBLANES, D_MODEL),
                           lambda i, bb: (bb, jnp.minimum((i + 1) * r8, n // SUBLANES - 1), 0))
    ctx = pl.BlockSpec((1, nctx, KV_REP), lambda i, bb: (bb, 0, 0))
    smem = pl.BlockSpec(memory_space=pltpu.SMEM)
    in_specs = [smem, tok(D_MODEL), halo8_p, halo8_n, tok(ATTN_WIDTH)]
    args = [sink, bgu, bgu, bgu, q]
    if local:
        rb = r // BLOCK
        halo_p = pl.BlockSpec((1, BLOCK, KV_REP), lambda i, bb: (bb, jnp.maximum(i * rb - 1, 0), 0))
        halo_n = pl.BlockSpec((1, BLOCK, KV_REP),
                              lambda i, bb: (bb, jnp.minimum((i + 1) * rb, n // BLOCK - 1), 0))
        bias = _band_bias()
        bshape = (1, 2 * BLOCK, 6 * BLOCK)
        in_specs += [tok(KV_REP), halo_p, halo_n, tok(KV_REP), halo_p, halo_n, ctx, ctx,
                     pl.BlockSpec(bshape, lambda i, bb: (0, 0, 0)),
                     pl.BlockSpec(bshape, lambda i, bb: (jnp.where(i == 0, 1, 0), 0, 0)),
                     pl.BlockSpec(bshape, lambda i, bb: (jnp.where(i == nt - 1, 2, 0), 0, 0))]
        args += [kk, kk, kk, vv, vv, vv, kkc, vvc, bias, bias, bias]
    else:
        in_specs += [ctx, ctx]
        args += [kkc, vvc]
    in_specs += [_const_spec((3, CONV_WIDTH))]
    args += [conv_w]
    score_cols = 2 * nctx + (6 * BLOCK if local else 0)
    return pl.pallas_call(
        functools.partial(_mixout_kernel, tile_rows=r, local=local),
        grid=(nt, b),
        in_specs=in_specs,
        out_specs=tok(D_MODEL),
        out_shape=jax.ShapeDtypeStruct((b, n, D_MODEL), BF16),
        scratch_shapes=[pltpu.VMEM((SCORE_SLOTS, 2 * BLOCK, score_cols), F32),
                        pltpu.VMEM((SCORE_SLOTS, 2 * BLOCK, score_cols), BF16)],
        compiler_params=_params(2),
        name="mixout_local" if local else "mixout_ctx",
    )(*args)


def _chan_dft_kernel(x_ref, sh_ref, sc_ref, g_ref, cs_ref, y1_ref, y2_ref):
    h = _norm_mod(x_ref[0], g_ref[...], sh_ref[0], sc_ref[0]).astype(BF16)
    fg = FOURIER_GROUP
    for gi in range(N_FOURIER_GROUPS):
        yc = jnp.dot(h[:, gi * fg:(gi + 1) * fg], cs_ref[...], preferred_element_type=F32)
        y1_ref[0, :, gi * fg:(gi + 1) * fg] = yc[:, :fg].astype(BF16)
        y2_ref[0, :, gi * fg:(gi + 1) * fg] = yc[:, fg:].astype(BF16)


def _chan_dft(x, shift, scale, g, cs, tm):
    b, n, _ = x.shape
    tok = pl.BlockSpec((1, tm, D_MODEL), lambda i, bb: (bb, i, 0))
    mod = pl.BlockSpec((1, 1, D_MODEL), lambda i, bb: (bb, 0, 0))
    return pl.pallas_call(
        _chan_dft_kernel,
        grid=(n // tm, b),
        in_specs=[tok, mod, mod, _const_spec((1, D_MODEL)), _const_spec((FOURIER_GROUP, 2 * FOURIER_GROUP))],
        out_specs=[tok, tok],
        out_shape=[jax.ShapeDtypeStruct((b, n, D_MODEL), BF16)] * 2,
        compiler_params=_params(2),
        name="chan_dft",
    )(x, shift, scale, g, cs)


def _seq_dft_kernel(cn_ref, sn_ref, y1_ref, y2_ref, o_ref):
    z = jnp.dot(cn_ref[...], y1_ref[0], preferred_element_type=F32)
    z = z + jnp.dot(sn_ref[...], y2_ref[0], preferred_element_type=F32)
    o_ref[0] = z.astype(BF16)


def _seq_dft(cn, nsn, y1, y2, tm):
    b, n, _ = y1.shape
    tok = pl.BlockSpec((1, tm, D_MODEL), lambda bb, i: (bb, i, 0))
    mat = pl.BlockSpec((tm, n), lambda bb, i: (i, 0))
    full = pl.BlockSpec((1, n, D_MODEL), lambda bb, i: (bb, 0, 0), pipeline_mode=pl.Buffered(1))
    return pl.pallas_call(
        _seq_dft_kernel,
        grid=(b, n // tm),
        in_specs=[mat, mat, full, full],
        out_specs=tok,
        out_shape=jax.ShapeDtypeStruct((b, n, D_MODEL), BF16),
        compiler_params=_params(2),
        name="seq_dft",
    )(cn, nsn, y1, y2)


def _dft_mats(n, scale):
    j = np.arange(n)
    ang = ((j[:, None] * j[None, :]) % n) * (2.0 * np.pi / n)
    return (np.cos(ang) * scale).astype(np.float32), (np.sin(ang) * scale).astype(np.float32)


FFT_RADIX = 16
FFT_INNER = 256
FFT_N = FFT_RADIX * FFT_INNER
PERM_TILE = 512
PERM_B = PERM_TILE // FFT_RADIX
PERM_STEP = 4 * PERM_TILE


def _chan_dft_perm_kernel(x_ref, sh_ref, sc_ref, g_ref, p_ref, cs_ref, y_ref):
    fg = FOURIER_GROUP
    for t in range(x_ref.shape[1] // PERM_TILE):
        h = _norm_mod(x_ref[0, t * PERM_TILE:(t + 1) * PERM_TILE, :], g_ref[...], sh_ref[0], sc_ref[0]).astype(BF16)
        hp = jnp.dot(p_ref[...], h, preferred_element_type=F32).astype(BF16)
        for gi in range(N_FOURIER_GROUPS):
            gs = slice(gi * fg, (gi + 1) * fg)
            yc = jnp.dot(hp[:, gs], cs_ref[...], preferred_element_type=F32)
            for a in range(FFT_RADIX):
                rs = slice(a * PERM_B, (a + 1) * PERM_B)
                ws = slice(t * PERM_B, (t + 1) * PERM_B)
                y_ref[0, gi, a, 0, ws, :] = yc[rs, :fg].astype(BF16)
                y_ref[0, gi, a, 1, ws, :] = yc[rs, fg:].astype(BF16)


def _chan_dft_perm(x, shift, scale, g, cs):
    b, n, _ = x.shape
    r_out = np.arange(PERM_TILE)
    perm = np.zeros((PERM_TILE, PERM_TILE), np.float32)
    perm[r_out, FFT_RADIX * (r_out % PERM_B) + r_out // PERM_B] = 1.0
    tok = pl.BlockSpec((1, PERM_STEP, D_MODEL), lambda i, bb: (bb, i, 0))
    mod = pl.BlockSpec((1, 1, D_MODEL), lambda i, bb: (bb, 0, 0))
    return pl.pallas_call(
        _chan_dft_perm_kernel,
        grid=(n // PERM_STEP, b),
        in_specs=[tok, mod, mod, _const_spec((1, D_MODEL)), _const_spec((PERM_TILE, PERM_TILE)),
                  _const_spec((FOURIER_GROUP, 2 * FOURIER_GROUP))],
        out_specs=pl.BlockSpec((1, N_FOURIER_GROUPS, FFT_RADIX, 2, PERM_STEP // FFT_RADIX, FOURIER_GROUP),
                               lambda i, bb: (bb, 0, 0, 0, i, 0)),
        out_shape=jax.ShapeDtypeStruct((b, N_FOURIER_GROUPS, FFT_RADIX, 2, n // FFT_RADIX, FOURIER_GROUP), BF16),
        compiler_params=_params(2),
        name="chan_dft_perm",
    )(x, shift, scale, g, jnp.asarray(perm, BF16), cs)


_C1 = float(np.cos(np.pi / 8))
_C2 = float(np.cos(np.pi / 4))
_C3 = float(np.cos(3 * np.pi / 8))


def _dft16_real(ur, ui):
    p = [ur[0]] + [ur[a] + ur[16 - a] for a in range(1, 8)] + [ur[8]]
    q = [p[a] + p[8 - a] for a in range(4)] + [p[4]]
    r = [p[a] - p[8 - a] for a in range(4)]
    s0, s1, s2 = q[0] + q[4], q[1] + q[3], q[2]
    t0, t1 = q[0] - q[4], q[1] - q[3]
    ca = [None] * 9
    ca[0] = s0 + s1 + s2
    ca[4] = s0 - s2
    ca[8] = s0 - s1 + s2
    ca[2] = t0 + _C2 * t1
    ca[6] = t0 - _C2 * t1
    e0, e1 = r[0] + _C2 * r[2], r[0] - _C2 * r[2]
    f, g = _C1 * r[1] + _C3 * r[3], _C3 * r[1] - _C1 * r[3]
    ca[1], ca[7], ca[3], ca[5] = e0 + f, e0 - f, e1 + g, e1 - g
    pp = [None] + [ui[a] - ui[16 - a] for a in range(1, 8)]
    qq = [None] + [pp[a] + pp[8 - a] for a in range(1, 4)] + [pp[4]]
    rr = [None] + [pp[a] - pp[8 - a] for a in range(1, 4)]
    sb = [None] * 8
    e0, e1 = _C2 * qq[2] + qq[4], _C2 * qq[2] - qq[4]
    f, g = _C3 * qq[1] + _C1 * qq[3], _C1 * qq[1] - _C3 * qq[3]
    sb[1], sb[7], sb[3], sb[5] = f + e0, f - e0, g + e1, g - e1
    w = _C2 * (rr[1] + rr[3])
    sb[2], sb[6], sb[4] = w + rr[2], w - rr[2], rr[1] - rr[3]
    out = [None] * 16
    out[0], out[8] = ca[0], ca[8]
    for c in range(1, 8):
        out[c] = ca[c] + sb[c]
        out[16 - c] = ca[c] - sb[c]
    return out


def _seq_fft_kernel(y_ref, l_ref, o_ref, u_even_ref, u_odd_ref):
    t = pl.program_id(0)

    @pl.when(t == 0)
    def _():
        u_odd_ref[...] = jnp.zeros(u_odd_ref.shape, F32)

    def step(cur_ref, prev_ref):
        rt = 2 * SUBLANES
        for i in range(FFT_INNER // rt):
            r0 = i * rt
            for lh in range(FOURIER_GROUP // LANES):
                ls = slice(lh * LANES, (lh + 1) * LANES)
                ur = [prev_ref[a, r0:r0 + rt, ls] for a in range(FFT_RADIX)]
                ui = [prev_ref[a, FFT_INNER + r0:FFT_INNER + r0 + rt, ls] for a in range(FFT_RADIX)]
                out = _dft16_real(ur, ui)
                for c in range(FFT_RADIX):
                    o_ref[0, c * FFT_INNER + r0:c * FFT_INNER + r0 + rt, ls] = out[c].astype(BF16)
        for a in range(FFT_RADIX):
            u = jnp.dot(l_ref[a, :, :FFT_INNER], y_ref[a, 0], preferred_element_type=F32)
            cur_ref[a] = u + jnp.dot(l_ref[a, :, FFT_INNER:], y_ref[a, 1], preferred_element_type=F32)

    pl.when(t % 2 == 0)(lambda: step(u_even_ref, u_odd_ref))
    pl.when(t % 2 == 1)(lambda: step(u_odd_ref, u_even_ref))


def _seq_fft(ycat, lmat):
    b = ycat.shape[0]
    ng = N_FOURIER_GROUPS
    items = b * ng

    def y_map(t):
        tt = jnp.minimum(t, items - 1)
        return (tt // ng, tt % ng, 0, 0, 0, 0)

    def o_map(t):
        tp = jnp.maximum(t - 1, 0)
        return (tp // ng, 0, tp % ng)

    return pl.pallas_call(
        _seq_fft_kernel,
        grid=(items + 1,),
        in_specs=[pl.BlockSpec((None, None, FFT_RADIX, 2, FFT_INNER, FOURIER_GROUP), y_map),
                  _const_spec((FFT_RADIX, 2 * FFT_INNER, 2 * FFT_INNER))],
        out_specs=pl.BlockSpec((1, FFT_N, FOURIER_GROUP), o_map),
        out_shape=jax.ShapeDtypeStruct((b, FFT_N, D_MODEL), BF16),
        scratch_shapes=[pltpu.VMEM((FFT_RADIX, 2 * FFT_INNER, FOURIER_GROUP), F32)] * 2,
        compiler_params=_params(1),
        name="seq_fft",
    )(ycat, lmat)


def _fft_stage_mats():
    a = np.arange(FFT_RADIX)[:, None, None]
    d = np.arange(FFT_INNER)[None, :, None]
    bb = np.arange(FFT_INNER)[None, None, :]
    ang = ((d * (a + FFT_RADIX * bb)) % FFT_N) * (2.0 * np.pi / FFT_N)
    c = np.cos(ang) * (FFT_N ** -0.5)
    s = np.sin(ang) * (FFT_N ** -0.5)
    top = np.concatenate([c, -s], axis=2)
    bot = np.concatenate([-s, -c], axis=2)
    return jnp.asarray(np.concatenate([top, bot], axis=1).astype(np.float32), BF16)


FF_CHUNK = 256
TAIL_ROWS = 1024


def _tail_kernel(*refs, final):
    if final:
        (x_ref, m_ref, g1_ref, wm_ref, sh_ref, sc_ref, gate_ref, g_ref, wg_ref, wu_ref, wd_ref,
         fg_ref, o_ref, acc_ref) = refs
    else:
        (x_ref, m_ref, g1_ref, wm_ref, sh_ref, sc_ref, gate_ref, g_ref, wg_ref, wu_ref, wd_ref,
         o_ref, acc_ref) = refs
    x = x_ref[0] + g1_ref[0] * jnp.dot(m_ref[0], wm_ref[...], preferred_element_type=F32)
    h = _norm_mod(x, g_ref[...], sh_ref[0], sc_ref[0]).astype(BF16)
    for c in range(D_FF // FF_CHUNK):
        cs = slice(c * FF_CHUNK, (c + 1) * FF_CHUNK)
        gt = jnp.dot(h, wg_ref[:, cs], preferred_element_type=F32)
        up = jnp.dot(h, wu_ref[:, cs], preferred_element_type=F32)
        acc_ref[:, cs] = (_silu(gt) * up).astype(BF16)
    y = x + gate_ref[0] * jnp.dot(acc_ref[...], wd_ref[...], preferred_element_type=F32)
    if final:
        ms = jnp.mean(y * y, axis=-1, keepdims=True)
        y = (y * lax.rsqrt(ms + EPS)) * fg_ref[...]
    o_ref[0] = y


def _tail(x, m, gate1, wm, wm_layer, shift, scale, gate, g, wg, wu, wd, layer, tm, final_g=None):
    b, n, _ = x.shape
    tok = pl.BlockSpec((1, tm, D_MODEL), lambda i, bb: (bb, i, 0))
    mod = pl.BlockSpec((1, 1, D_MODEL), lambda i, bb: (bb, 0, 0))
    in_specs = [tok, tok, mod, _layer_spec((D_MODEL, D_MODEL), wm_layer), mod, mod, mod, _const_spec((1, D_MODEL)),
                _layer_spec((D_MODEL, D_FF), layer), _layer_spec((D_MODEL, D_FF), layer),
                _layer_spec((D_FF, D_MODEL), layer)]
    args = [x, m, gate1, wm, shift, scale, gate, g, wg, wu, wd]
    if final_g is not None:
        in_specs.append(_const_spec((1, D_MODEL)))
        args.append(final_g)
    return pl.pallas_call(
        functools.partial(_tail_kernel, final=final_g is not None),
        grid=(n // tm, b),
        in_specs=in_specs,
        out_specs=tok,
        out_shape=jax.ShapeDtypeStruct((b, n, D_MODEL), F32),
        scratch_shapes=[pltpu.VMEM((tm, D_FF), BF16)],
        compiler_params=_params(2),
        name="tail",
    )(*args)


def _rope_lane_tables(n):
    rows = n // GRID_W
    r = np.repeat(np.arange(rows), GRID_W).astype(np.float64)
    col = np.tile(np.arange(GRID_W), rows).astype(np.float64)
    quarter = HEAD_DIM // 4
    inv = ROPE_THETA ** (-np.arange(quarter, dtype=np.float64) / quarter)
    ang_r = r[:, None] * inv
    ang_c = col[:, None] * inv
    cr, sr, cc, sc = np.cos(ang_r), np.sin(ang_r), np.cos(ang_c), np.sin(ang_c)
    z = np.zeros_like(sr)
    cos = np.concatenate([cr, cr, cc, cc], axis=1)
    sa = np.concatenate([-sr, z, -sc, z], axis=1)
    sb = np.concatenate([z, sr, z, sc], axis=1)
    rep = lambda t: jnp.asarray(np.concatenate([t, t], axis=1).astype(np.float32))
    return rep(cos), rep(sa), rep(sb)


def _identity_lane_tables(n):
    one, zero = np.ones((n, LANES), np.float32), np.zeros((n, LANES), np.float32)
    return jnp.asarray(one), jnp.asarray(zero), jnp.asarray(zero)


def kernel(x, c, ctx, c_ctx, w_ada, b_ada, norm1_g, norm2_g, w_in, conv_w, sink,
           w_mix_out, w_fourier_out, w_ffn_gate, w_ffn_up, w_ffn_down, final_g):
    b, n, d = x.shape
    L = ctx.shape[1]
    nc = b * L
    tm = 512

    cond = jnp.concatenate([c, c_ctx[None, :], jnp.zeros((COND_ROWS - b - 1, d), F32)], axis=0)
    mods = _adaln(cond, w_ada, b_ada)

    def split(l, lo, hi):
        m = mods[l, lo:hi]
        return [m[:, None, k * d:(k + 1) * d] for k in range(6)]

    rope_lat = _rope_lane_tables(n)
    rope_ctx = _identity_lane_tables(nc)
    w_in_b, w_mix_b, w_fo_b = w_in.astype(BF16), w_mix_out.astype(BF16), w_fourier_out.astype(BF16)
    wg_b, wu_b, wd_b = w_ffn_gate.astype(BF16), w_ffn_up.astype(BF16), w_ffn_down.astype(BF16)
    cc, sc_ = _dft_mats(FOURIER_GROUP, FOURIER_GROUP ** -0.5)
    cs_chan = jnp.asarray(np.concatenate([cc, sc_], axis=1), BF16)
    seq_mats = {}
    for m in {n, L} - {FFT_N}:
        cm, sm = _dft_mats(m, m ** -0.5)
        seq_mats[m] = (jnp.asarray(cm, BF16), jnp.asarray(-sm, BF16))
    fft_l = _fft_stage_mats() if n == FFT_N else None

    xc = ctx.reshape(1, nc, d)
    per_seq = lambda t: t.reshape(b, L, t.shape[-1])
    for l in range(DEPTH):
        ctx_after = any(j % 2 == 0 for j in range(l + 1, DEPTH))
        ctx_here = (l % 2 == 0) or ctx_after
        sh1, sc1, g1, sh2, sc2, g2 = split(l, 0, b)
        if ctx_here:
            csh1, csc1, cg1, csh2, csc2, cg2 = split(l, b, b + 1)
        n1 = norm1_g[l][None, :]
        n2 = norm2_g[l][None, :]
        if l % 2 == 0:
            e = l // 2
            wm, wm_layer = w_mix_b, e
            bgu_c, q_c, kk_c, vv_c = _inproj(xc, csh1, csc1, n1, w_in_b, e, *rope_ctx, tm=min(nc, INPROJ_ROWS))
            kk_c, vv_c = per_seq(kk_c), per_seq(vv_c)
            bgu, q, kk, vv = _inproj(x, sh1, sc1, n1, w_in_b, e, *rope_lat, tm=min(n, INPROJ_ROWS))
            mix = _mixout(bgu, q, kk, vv, kk_c, vv_c, sink[e], conv_w[e], tile_rows=min(n, MIX_ROWS), local=True)
            if ctx_after:
                mix_c = _mixout(per_seq(bgu_c), per_seq(q_c), None, None, kk_c, vv_c, sink[e], conv_w[e],
                                tile_rows=L, local=False)
        else:
            wm, wm_layer = w_fo_b, l // 2
            if n == FFT_N:
                mix = _seq_fft(_chan_dft_perm(x, sh1, sc1, n1, cs_chan), fft_l)
            else:
                mix = _seq_dft(*seq_mats[n], *_chan_dft(x, sh1, sc1, n1, cs_chan, tm=tm), tm=256)
            if ctx_after:
                y1c, y2c = _chan_dft(xc, csh1, csc1, n1, cs_chan, tm=min(nc, tm))
                mix_c = _seq_dft(*seq_mats[L], per_seq(y1c), per_seq(y2c), tm=L)
        x = _tail(x, mix, g1, wm, wm_layer, sh2, sc2, g2, n2, wg_b, wu_b, wd_b, l, tm=min(n, TAIL_ROWS),
                  final_g=final_g[None, :] if l == DEPTH - 1 else None)
        if ctx_after:
            xc = _tail(xc, mix_c.reshape(1, nc, d), cg1, wm, wm_layer, csh2, csc2, cg2, n2, wg_b, wu_b, wd_b, l,
                       tm=min(nc, TAIL_ROWS))
    return x
```

```python
import functools

import numpy as np
import jax
import jax.numpy as jnp
from jax import lax
from jax.experimental import pallas as pl
from jax.experimental.pallas import tpu as pltpu

F32 = jnp.float32
BF16 = jnp.bfloat16

D_MODEL = 1024
DEPTH = 4
GRID_W = 64
HEAD_DIM = 64
N_Q_HEADS = 8
N_KV_HEADS = 2
ATTN_WIDTH = N_Q_HEADS * HEAD_DIM
KV_WIDTH = N_KV_HEADS * HEAD_DIM
CONV_WIDTH = D_MODEL - ATTN_WIDTH
WINDOW = 128
BLOCK = 128
ROPE_THETA = 10000.0
N_FOURIER_GROUPS = 4
FOURIER_GROUP = D_MODEL // N_FOURIER_GROUPS
D_FF = 2816
EPS = 1e-6
Q_START = 3 * CONV_WIDTH
K_START = Q_START + ATTN_WIDTH
V_START = K_START + KV_WIDTH

LANES = 128
SUBLANES = 8
KV_REP = 2 * KV_WIDTH
NEG_BIG = -1e30
COND_ROWS = 24
VMEM_LIMIT = 56 * 1024 * 1024


def _params(n_axes):
    return pltpu.CompilerParams(dimension_semantics=("arbitrary",) * n_axes,
                                vmem_limit_bytes=VMEM_LIMIT)


def _const_spec(shape):
    nd = len(shape)
    return pl.BlockSpec(shape, lambda *_: (0,) * nd, pipeline_mode=pl.Buffered(1))


def _layer_spec(shape, layer):
    nd = len(shape)
    return pl.BlockSpec((None,) + tuple(shape), lambda *_: (layer,) + (0,) * nd, pipeline_mode=pl.Buffered(1))


def _norm_mod(x, g, shift, scale):
    ms = jnp.mean(x * x, axis=-1, keepdims=True)
    y = x * lax.rsqrt(ms + EPS)
    return (y * g) * (1.0 + scale) + shift


def _silu(x):
    return x * (1.0 / (1.0 + jnp.exp(-x)))


def _adaln_kernel(c_ref, w_ref, b_ref, o_ref):
    a = _silu(c_ref[...])
    w = w_ref[0]
    a_hi = a.astype(BF16)
    a_lo = (a - a_hi.astype(F32)).astype(BF16)
    w_hi = w.astype(BF16)
    w_lo = (w - w_hi.astype(F32)).astype(BF16)
    acc = jnp.dot(a_hi, w_hi, preferred_element_type=F32)
    acc += jnp.dot(a_lo, w_hi, preferred_element_type=F32)
    acc += jnp.dot(a_hi, w_lo, preferred_element_type=F32)
    o_ref[0] = acc + b_ref[0]


def _adaln(cond, w_ada, b_ada):
    tn = 1024
    nt = (6 * D_MODEL) // tn
    return pl.pallas_call(
        _adaln_kernel,
        grid=(DEPTH, nt),
        in_specs=[
            pl.BlockSpec((COND_ROWS, D_MODEL), lambda l, j: (0, 0)),
            pl.BlockSpec((1, D_MODEL, tn), lambda l, j: (l, 0, j)),
            pl.BlockSpec((1, 1, tn), lambda l, j: (l, 0, j)),
        ],
        out_specs=pl.BlockSpec((1, COND_ROWS, tn), lambda l, j: (l, 0, j)),
        out_shape=jax.ShapeDtypeStruct((DEPTH, COND_ROWS, 6 * D_MODEL), F32),
        compiler_params=_params(2),
        name="adaln",
    )(cond, w_ada, b_ada.reshape(DEPTH, 1, 6 * D_MODEL))


IN_WIDTH = V_START + KV_WIDTH
LOG2E = float(np.log2(np.e))
Q_SCALE = HEAD_DIM ** -0.5 * LOG2E
INPROJ_ROWS = 2048
INPROJ_SUB = 512


def _inproj_kernel(x_ref, sh_ref, sc_ref, g_ref, w_ref, cos_ref, sa_ref, sb_ref,
                   bgu_ref, q_ref, kk_ref, vv_ref):
    first_half = lax.broadcasted_iota(jnp.int32, (1, LANES), 1) < HEAD_DIM
    tm = x_ref.shape[1]
    sub = min(tm, INPROJ_SUB)
    for s0 in range(0, tm, sub):
        rs = slice(s0, s0 + sub)
        h = _norm_mod(x_ref[0, rs, :], g_ref[...], sh_ref[0], sc_ref[0]).astype(BF16)
        p = jnp.dot(h, w_ref[...], preferred_element_type=F32)
        bgu_ref[0, rs, :CONV_WIDTH] = p[:, :CONV_WIDTH].astype(BF16)
        bgu_ref[0, rs, CONV_WIDTH:] = (p[:, CONV_WIDTH:2 * CONV_WIDTH]
                                       * p[:, 2 * CONV_WIDTH:Q_START]).astype(BF16)
        cos = cos_ref[rs, :]
        sa = sa_ref[rs, :]
        sb = sb_ref[rs, :]

        def rope(t):
            return t * cos + pltpu.roll(t, LANES - 16, 1) * sa + pltpu.roll(t, 16, 1) * sb

        for j in range(ATTN_WIDTH // LANES):
            lo = Q_START + j * LANES
            q_ref[0, rs, j * LANES:(j + 1) * LANES] = (rope(p[:, lo:lo + LANES]) * Q_SCALE).astype(BF16)
        for t, ref in ((rope(p[:, K_START:V_START]), kk_ref), (p[:, V_START:], vv_ref)):
            swapped = pltpu.roll(t, HEAD_DIM, 1)
            ref[0, rs, :LANES] = jnp.where(first_half, t, swapped).astype(BF16)
            ref[0, rs, LANES:] = jnp.where(first_half, swapped, t).astype(BF16)


def _inproj(x, shift, scale, g, w_in, layer, cos, sa, sb, tm):
    b, n, _ = x.shape
    nt = n // tm
    tok = lambda w: pl.BlockSpec((1, tm, w), lambda i, bb: (bb, i, 0))
    mod = pl.BlockSpec((1, 1, D_MODEL), lambda i, bb: (bb, 0, 0))
    tab = pl.BlockSpec((tm, LANES), lambda i, bb: (i, 0))
    return pl.pallas_call(
        _inproj_kernel,
        grid=(nt, b),
        in_specs=[tok(D_MODEL), mod, mod, _const_spec((1, D_MODEL)),
                  _layer_spec((D_MODEL, IN_WIDTH), layer), tab, tab, tab],
        out_specs=[tok(D_MODEL), tok(ATTN_WIDTH), tok(KV_REP), tok(KV_REP)],
        out_shape=[jax.ShapeDtypeStruct((b, n, D_MODEL), BF16),
                   jax.ShapeDtypeStruct((b, n, ATTN_WIDTH), BF16),
                   jax.ShapeDtypeStruct((b, n, KV_REP), BF16),
                   jax.ShapeDtypeStruct((b, n, KV_REP), BF16)],
        compiler_params=_params(2),
        name="inproj",
    )(x, shift, scale, g, w_in, cos, sa, sb)


CONV_ROWS = 32
SM_ROWS = 16
SCORE_SLOTS = 2
MIX_ROWS = 1024


def _mixout_kernel(*refs, tile_rows, local):
    if local:
        (sink_ref, bgu_ref, bgu_p_ref, bgu_n_ref, q_ref,
         kk_ref, kk_p_ref, kk_n_ref, vv_ref, vv_p_ref, vv_n_ref, kkc_ref, vvc_ref,
         bias_mid_ref, bias_a_ref, bias_b_ref, cw_ref, o_ref, s_slots, e_slots) = refs
    else:
        (sink_ref, bgu_ref, bgu_p_ref, bgu_n_ref, q_ref,
         kkc_ref, vvc_ref, cw_ref, o_ref, s_slots, e_slots) = refs
    i = pl.program_id(0)
    n_tiles = pl.num_programs(0)
    r = tile_rows
    nb = r // BLOCK

    has_prev = (i > 0).astype(F32)
    has_next = (i < n_tiles - 1).astype(F32)
    cw = cw_ref[...]
    cr = CONV_ROWS
    pk = 2 * SUBLANES
    row = lax.broadcasted_iota(jnp.int32, (cr, CONV_WIDTH), 0)

    def conv_chunk(r0):
        bg = bgu_ref[0, r0:r0 + cr, :CONV_WIDTH].astype(F32)
        u = bgu_ref[0, r0:r0 + cr, CONV_WIDTH:].astype(F32)
        if r0 == 0:
            u_prev = bgu_p_ref[0, SUBLANES - 1:SUBLANES, CONV_WIDTH:].astype(F32) * has_prev
        else:
            u_prev = bgu_ref[0, r0 - pk:r0, CONV_WIDTH:].astype(F32)[pk - 1:pk]
        if r0 + cr == r:
            u_next = bgu_n_ref[0, 0:1, CONV_WIDTH:].astype(F32) * has_next
        else:
            u_next = bgu_ref[0, r0 + cr:r0 + cr + pk, CONV_WIDTH:].astype(F32)[0:1]
        u_up = jnp.where(row == 0, u_prev, pltpu.roll(u, 1, 0))
        u_dn = jnp.where(row == cr - 1, u_next, pltpu.roll(u, cr - 1, 0))
        o_ref[0, r0:r0 + cr, :CONV_WIDTH] = (bg * (u_up * cw[0:1] + u * cw[1:2] + u_dn * cw[2:3])).astype(BF16)

    lane = lax.broadcasted_iota(jnp.int32, (1, LANES), 1)
    first_half = lane < HEAD_DIM
    zero = jnp.zeros((), BF16)
    nt_dims = (((1,), (1,)), ((), ()))

    def split_k(t):
        return jnp.where(first_half, t, zero), jnp.where(first_half, zero, t)

    def split_v(t):
        va, vb = split_k(t)
        first = lax.broadcasted_iota(jnp.int32, t.shape, 1) < HEAD_DIM
        ones_a = jnp.where(first, 1.0, 0.0).astype(BF16)
        ones_b = jnp.where(first, 0.0, 1.0).astype(BF16)
        return jnp.concatenate([va, ones_a], axis=1), jnp.concatenate([vb, ones_b], axis=1)

    n_ctx = kkc_ref.shape[1]
    wc = 2 * n_ctx
    nl = 3 * BLOCK
    groups_per_pair = BLOCK // SM_ROWS
    cols_a = [slice(0, n_ctx)] + ([slice(wc, wc + nl)] if local else [])
    cols_b = [slice(n_ctx, wc)] + ([slice(wc + nl, wc + 2 * nl)] if local else [])

    heads = []
    for h in range(N_KV_HEADS):
        hs = slice(h * LANES, (h + 1) * LANES)
        hd = dict(kc_bd=jnp.concatenate(split_k(kkc_ref[0, :, hs]), axis=0),
                  vc_ext=jnp.concatenate(split_v(vvc_ref[0, :, hs]), axis=0),
                  sinks=[sink_ref[4 * h + i] * LOG2E for i in range(4)])
        if local:
            hd["k"] = split_k(jnp.concatenate([kk_p_ref[0, :, hs], kk_ref[0, :, hs], kk_n_ref[0, :, hs]], axis=0))
            hd["v"] = split_v(jnp.concatenate([vv_p_ref[0, :, hs], vv_ref[0, :, hs], vv_n_ref[0, :, hs]], axis=0))
        heads.append(hd)

    blocks = [(h, j) for h in range(N_KV_HEADS) for j in range(nb)]

    def scores(idx):
        h, j = blocks[idx]
        hd = heads[h]
        rs = slice(j * BLOCK, (j + 1) * BLOCK)
        q2 = jnp.concatenate([q_ref[0, rs, 2 * h * LANES:(2 * h + 1) * LANES],
                              q_ref[0, rs, (2 * h + 1) * LANES:(2 * h + 2) * LANES]], axis=0)
        s_ref = s_slots.at[idx % SCORE_SLOTS]
        s_ref[:, :wc] = lax.dot_general(q2, hd["kc_bd"], nt_dims, preferred_element_type=F32)
        if local:
            ws = slice(j * BLOCK, j * BLOCK + nl)
            kl_bd = jnp.concatenate([hd["k"][0][ws], hd["k"][1][ws]], axis=0)
            if nb == 1:
                bias = bias_a_ref[0] + bias_b_ref[0]
            elif j == 0:
                bias = bias_a_ref[0]
            elif j == nb - 1:
                bias = bias_b_ref[0]
            else:
                bias = bias_mid_ref[0]
            s_loc = lax.dot_general(q2, kl_bd, nt_dims, preferred_element_type=F32)
            for c0 in range(0, 2 * nl, BLOCK):
                seg = s_loc[:, c0:c0 + BLOCK]
                if (c0 // BLOCK) % 3 != 1:
                    seg = seg + bias[:, c0:c0 + BLOCK]
                s_ref[:, wc + c0:wc + c0 + BLOCK] = seg

    def softmax_pv(idx):
        h, j = blocks[idx]
        hd = heads[h]
        s_ref = s_slots.at[idx % SCORE_SLOTS]
        e_ref = e_slots.at[idx % SCORE_SLOTS]
        sink_terms = []
        for g in range(2 * BLOCK // SM_ROWS):
            gr = slice(g * SM_ROWS, (g + 1) * SM_ROWS)
            pair = g // groups_per_pair
            terms = []
            for cols, sk in ((cols_a, hd["sinks"][2 * pair]), (cols_b, hd["sinks"][2 * pair + 1])):
                ss = [s_ref[gr, c] for c in cols]
                m = jnp.max(ss[0], axis=-1, keepdims=True)
                for t in ss[1:]:
                    m = jnp.maximum(m, jnp.max(t, axis=-1, keepdims=True))
                m = jnp.maximum(m, sk)
                for c, t in zip(cols, ss):
                    e_ref[gr, c] = jnp.exp2(t - m).astype(BF16)
                terms.append(jnp.exp2(sk - m))
            sink_terms.append(jnp.where(first_half, terms[0], terms[1]))
        o_ext = jnp.dot(e_ref[:, :wc], hd["vc_ext"], preferred_element_type=F32)
        if local:
            ws = slice(j * BLOCK, j * BLOCK + nl)
            vl_ext = jnp.concatenate([hd["v"][0][ws], hd["v"][1][ws]], axis=0)
            o_ext = o_ext + jnp.dot(e_ref[:, wc:], vl_ext, preferred_element_type=F32)
        o = o_ext[:, :LANES] / (o_ext[:, LANES:] + jnp.concatenate(sink_terms, axis=0))
        return jnp.concatenate([o[:BLOCK], o[BLOCK:]], axis=1)

    conv_starts = list(range(0, r, cr))
    per_block = -(-len(conv_starts) // len(blocks))
    outs = {}
    scores(0)
    for idx in range(len(blocks)):
        if idx + 1 < len(blocks):
            scores(idx + 1)
        for r0 in conv_starts[idx * per_block:(idx + 1) * per_block]:
            conv_chunk(r0)
        outs[blocks[idx]] = softmax_pv(idx)
    b_cols = [jnp.concatenate([outs[(h, j)] for j in range(nb)], axis=0) for h in range(N_KV_HEADS)]
    o_ref[0, :, CONV_WIDTH:] = jnp.concatenate(b_cols, axis=1).astype(BF16)


def _band_bias():
    rr = np.arange(2 * BLOCK)[:, None] % BLOCK
    kidx = np.arange(6 * BLOCK)[None, :] % (3 * BLOCK)
    band = (kidx >= rr) & (kidx <= rr + 2 * WINDOW)
    mid = band
    first = band & (kidx >= BLOCK)
    last = band & (kidx < 2 * BLOCK)
    return jnp.asarray(np.where(np.stack([mid, first, last]), 0.0, NEG_BIG), dtype=F32)


def _mixout(bgu, q, kk, vv, kkc, vvc, sink, conv_w, tile_rows, local):
    b, n, _ = bgu.shape
    r = tile_rows
    nt = n // r
    nctx = kkc.shape[1]
    tok = lambda w: pl.BlockSpec((1, r, w), lambda i, bb: (bb, i, 0))
    r8 = r // SUBLANES
    halo8_p = pl.BlockSpec((1, SUBLANES, D_MODEL), lambda i, bb: (bb, jnp.maximum(i * r8 - 1, 0), 0))
    halo8_n = pl.BlockSpec((1, SUBLANES, D_MODEL),
                           lambda i, bb: (bb, jnp.minimum((i + 1) * r8, n // SUBLANES - 1), 0))
    ctx = pl.BlockSpec((1, nctx, KV_REP), lambda i, bb: (bb, 0, 0))
    smem = pl.BlockSpec(memory_space=pltpu.SMEM)
    in_specs = [smem, tok(D_MODEL), halo8_p, halo8_n, tok(ATTN_WIDTH)]
    args = [sink, bgu, bgu, bgu, q]
    if local:
        rb = r // BLOCK
        halo_p = pl.BlockSpec((1, BLOCK, KV_REP), lambda i, bb: (bb, jnp.maximum(i * rb - 1, 0), 0))
        halo_n = pl.BlockSpec((1, BLOCK, KV_REP),
                              lambda i, bb: (bb, jnp.minimum((i + 1) * rb, n // BLOCK - 1), 0))
        bias = _band_bias()
        bshape = (1, 2 * BLOCK, 6 * BLOCK)
        in_specs += [tok(KV_REP), halo_p, halo_n, tok(KV_REP), halo_p, halo_n, ctx, ctx,
                     pl.BlockSpec(bshape, lambda i, bb: (0, 0, 0)),
                     pl.BlockSpec(bshape, lambda i, bb: (jnp.where(i == 0, 1, 0), 0, 0)),
                     pl.BlockSpec(bshape, lambda i, bb: (jnp.where(i == nt - 1, 2, 0), 0, 0))]
        args += [kk, kk, kk, vv, vv, vv, kkc, vvc, bias, bias, bias]
    else:
        in_specs += [ctx, ctx]
        args += [kkc, vvc]
    in_specs += [_const_spec((3, CONV_WIDTH))]
    args += [conv_w]
    score_cols = 2 * nctx + (6 * BLOCK if local else 0)
    return pl.pallas_call(
        functools.partial(_mixout_kernel, tile_rows=r, local=local),
        grid=(nt, b),
        in_specs=in_specs,
        out_specs=tok(D_MODEL),
        out_shape=jax.ShapeDtypeStruct((b, n, D_MODEL), BF16),
        scratch_shapes=[pltpu.VMEM((SCORE_SLOTS, 2 * BLOCK, score_cols), F32),
                        pltpu.VMEM((SCORE_SLOTS, 2 * BLOCK, score_cols), BF16)],
        compiler_params=_params(2),
        name="mixout_local" if local else "mixout_ctx",
    )(*args)


def _chan_dft_kernel(x_ref, sh_ref, sc_ref, g_ref, cs_ref, y1_ref, y2_ref):
    h = _norm_mod(x_ref[0], g_ref[...], sh_ref[0], sc_ref[0]).astype(BF16)
    fg = FOURIER_GROUP
    for gi in range(N_FOURIER_GROUPS):
        yc = jnp.dot(h[:, gi * fg:(gi + 1) * fg], cs_ref[...], preferred_element_type=F32)
        y1_ref[0, :, gi * fg:(gi + 1) * fg] = yc[:, :fg].astype(BF16)
        y2_ref[0, :, gi * fg:(gi + 1) * fg] = yc[:, fg:].astype(BF16)


def _chan_dft(x, shift, scale, g, cs, tm):
    b, n, _ = x.shape
    tok = pl.BlockSpec((1, tm, D_MODEL), lambda i, bb: (bb, i, 0))
    mod = pl.BlockSpec((1, 1, D_MODEL), lambda i, bb: (bb, 0, 0))
    return pl.pallas_call(
        _chan_dft_kernel,
        grid=(n // tm, b),
        in_specs=[tok, mod, mod, _const_spec((1, D_MODEL)), _const_spec((FOURIER_GROUP, 2 * FOURIER_GROUP))],
        out_specs=[tok, tok],
        out_shape=[jax.ShapeDtypeStruct((b, n, D_MODEL), BF16)] * 2,
        compiler_params=_params(2),
        name="chan_dft",
    )(x, shift, scale, g, cs)


def _seq_dft_kernel(cn_ref, sn_ref, y1_ref, y2_ref, o_ref):
    z = jnp.dot(cn_ref[...], y1_ref[0], preferred_element_type=F32)
    z = z + jnp.dot(sn_ref[...], y2_ref[0], preferred_element_type=F32)
    o_ref[0] = z.astype(BF16)


def _seq_dft(cn, nsn, y1, y2, tm):
    b, n, _ = y1.shape
    tok = pl.BlockSpec((1, tm, D_MODEL), lambda bb, i: (bb, i, 0))
    mat = pl.BlockSpec((tm, n), lambda bb, i: (i, 0))
    full = pl.BlockSpec((1, n, D_MODEL), lambda bb, i: (bb, 0, 0), pipeline_mode=pl.Buffered(1))
    return pl.pallas_call(
        _seq_dft_kernel,
        grid=(b, n // tm),
        in_specs=[mat, mat, full, full],
        out_specs=tok,
        out_shape=jax.ShapeDtypeStruct((b, n, D_MODEL), BF16),
        compiler_params=_params(2),
        name="seq_dft",
    )(cn, nsn, y1, y2)


def _dft_mats(n, scale):
    j = np.arange(n)
    ang = ((j[:, None] * j[None, :]) % n) * (2.0 * np.pi / n)
    return (np.cos(ang) * scale).astype(np.float32), (np.sin(ang) * scale).astype(np.float32)


FFT_RADIX = 16
FFT_INNER = 256
FFT_N = FFT_RADIX * FFT_INNER
PERM_TILE = 256
PERM_B = PERM_TILE // FFT_RADIX
CHAN_TILE = 2 * PERM_TILE
PERM_STEP = 4 * CHAN_TILE


def _chan_dft_perm_kernel(x_ref, sh_ref, sc_ref, g_ref, p_ref, cs_ref, y_ref):
    fg = FOURIER_GROUP
    pieces = CHAN_TILE // PERM_TILE
    for t in range(x_ref.shape[1] // CHAN_TILE):
        h = _norm_mod(x_ref[0, t * CHAN_TILE:(t + 1) * CHAN_TILE, :], g_ref[...], sh_ref[0], sc_ref[0]).astype(BF16)
        hp = jnp.concatenate(
            [jnp.dot(p_ref[...], h[k * PERM_TILE:(k + 1) * PERM_TILE], preferred_element_type=F32).astype(BF16)
             for k in range(pieces)], axis=0)
        for gi in range(N_FOURIER_GROUPS):
            gs = slice(gi * fg, (gi + 1) * fg)
            yc = jnp.dot(hp[:, gs], cs_ref[...], preferred_element_type=F32)
            for k in range(pieces):
                for a in range(FFT_RADIX):
                    rs = slice(k * PERM_TILE + a * PERM_B, k * PERM_TILE + (a + 1) * PERM_B)
                    ws = slice((t * pieces + k) * PERM_B, (t * pieces + k + 1) * PERM_B)
                    y_ref[0, gi, a, 0, ws, :] = yc[rs, :fg].astype(BF16)
                    y_ref[0, gi, a, 1, ws, :] = yc[rs, fg:].astype(BF16)


def _chan_dft_perm(x, shift, scale, g, cs):
    b, n, _ = x.shape
    r_out = np.arange(PERM_TILE)
    perm = np.zeros((PERM_TILE, PERM_TILE), np.float32)
    perm[r_out, FFT_RADIX * (r_out % PERM_B) + r_out // PERM_B] = 1.0
    tok = pl.BlockSpec((1, PERM_STEP, D_MODEL), lambda i, bb: (bb, i, 0))
    mod = pl.BlockSpec((1, 1, D_MODEL), lambda i, bb: (bb, 0, 0))
    return pl.pallas_call(
        _chan_dft_perm_kernel,
        grid=(n // PERM_STEP, b),
        in_specs=[tok, mod, mod, _const_spec((1, D_MODEL)), _const_spec((PERM_TILE, PERM_TILE)),
                  _const_spec((FOURIER_GROUP, 2 * FOURIER_GROUP))],
        out_specs=pl.BlockSpec((1, N_FOURIER_GROUPS, FFT_RADIX, 2, PERM_STEP // FFT_RADIX, FOURIER_GROUP),
                               lambda i, bb: (bb, 0, 0, 0, i, 0)),
        out_shape=jax.ShapeDtypeStruct((b, N_FOURIER_GROUPS, FFT_RADIX, 2, n // FFT_RADIX, FOURIER_GROUP), BF16),
        compiler_params=_params(2),
        name="chan_dft_perm",
    )(x, shift, scale, g, jnp.asarray(perm, BF16), cs)


_C1 = float(np.cos(np.pi / 8))
_C2 = float(np.cos(np.pi / 4))
_C3 = float(np.cos(3 * np.pi / 8))


def _dft16_real(ur, ui):
    p = [ur[0]] + [ur[a] + ur[16 - a] for a in range(1, 8)] + [ur[8]]
    q = [p[a] + p[8 - a] for a in range(4)] + [p[4]]
    r = [p[a] - p[8 - a] for a in range(4)]
    s0, s1, s2 = q[0] + q[4], q[1] + q[3], q[2]
    t0, t1 = q[0] - q[4], q[1] - q[3]
    ca = [None] * 9
    ca[0] = s0 + s1 + s2
    ca[4] = s0 - s2
    ca[8] = s0 - s1 + s2
    ca[2] = t0 + _C2 * t1
    ca[6] = t0 - _C2 * t1
    e0, e1 = r[0] + _C2 * r[2], r[0] - _C2 * r[2]
    f, g = _C1 * r[1] + _C3 * r[3], _C3 * r[1] - _C1 * r[3]
    ca[1], ca[7], ca[3], ca[5] = e0 + f, e0 - f, e1 + g, e1 - g
    pp = [None] + [ui[a] - ui[16 - a] for a in range(1, 8)]
    qq = [None] + [pp[a] + pp[8 - a] for a in range(1, 4)] + [pp[4]]
    rr = [None] + [pp[a] - pp[8 - a] for a in range(1, 4)]
    sb = [None] * 8
    e0, e1 = _C2 * qq[2] + qq[4], _C2 * qq[2] - qq[4]
    f, g = _C3 * qq[1] + _C1 * qq[3], _C1 * qq[1] - _C3 * qq[3]
    sb[1], sb[7], sb[3], sb[5] = f + e0, f - e0, g + e1, g - e1
    w = _C2 * (rr[1] + rr[3])
    sb[2], sb[6], sb[4] = w + rr[2], w - rr[2], rr[1] - rr[3]
    out = [None] * 16
    out[0], out[8] = ca[0], ca[8]
    for c in range(1, 8):
        out[c] = ca[c] + sb[c]
        out[16 - c] = ca[c] - sb[c]
    return out


def _seq_fft_kernel(y_ref, l_ref, o_ref, u_even_ref, u_odd_ref):
    t = pl.program_id(0)

    @pl.when(t == 0)
    def _():
        u_odd_ref[...] = jnp.zeros(u_odd_ref.shape, F32)

    def step(cur_ref, prev_ref):
        rt = 2 * SUBLANES
        for i in range(FFT_INNER // rt):
            r0 = i * rt
            for lh in range(FOURIER_GROUP // LANES):
                ls = slice(lh * LANES, (lh + 1) * LANES)
                ur = [prev_ref[a, r0:r0 + rt, ls] for a in range(FFT_RADIX)]
                ui = [prev_ref[a, FFT_INNER + r0:FFT_INNER + r0 + rt, ls] for a in range(FFT_RADIX)]
                out = _dft16_real(ur, ui)
                for c in range(FFT_RADIX):
                    o_ref[0, c * FFT_INNER + r0:c * FFT_INNER + r0 + rt, ls] = out[c].astype(BF16)
        for a in range(FFT_RADIX):
            u = jnp.dot(l_ref[a, :, :FFT_INNER], y_ref[a, 0], preferred_element_type=F32)
            cur_ref[a] = u + jnp.dot(l_ref[a, :, FFT_INNER:], y_ref[a, 1], preferred_element_type=F32)

    pl.when(t % 2 == 0)(lambda: step(u_even_ref, u_odd_ref))
    pl.when(t % 2 == 1)(lambda: step(u_odd_ref, u_even_ref))


def _seq_fft(ycat, lmat):
    b = ycat.shape[0]
    ng = N_FOURIER_GROUPS
    items = b * ng

    def y_map(t):
        tt = jnp.minimum(t, items - 1)
        return (tt // ng, tt % ng, 0, 0, 0, 0)

    def o_map(t):
        tp = jnp.maximum(t - 1, 0)
        return (tp // ng, 0, tp % ng)

    return pl.pallas_call(
        _seq_fft_kernel,
        grid=(items + 1,),
        in_specs=[pl.BlockSpec((None, None, FFT_RADIX, 2, FFT_INNER, FOURIER_GROUP), y_map),
                  _const_spec((FFT_RADIX, 2 * FFT_INNER, 2 * FFT_INNER))],
        out_specs=pl.BlockSpec((1, FFT_N, FOURIER_GROUP), o_map),
        out_shape=jax.ShapeDtypeStruct((b, FFT_N, D_MODEL), BF16),
        scratch_shapes=[pltpu.VMEM((FFT_RADIX, 2 * FFT_INNER, FOURIER_GROUP), F32)] * 2,
        compiler_params=_params(1),
        name="seq_fft",
    )(ycat, lmat)


def _fft_stage_mats():
    a = np.arange(FFT_RADIX)[:, None, None]
    d = np.arange(FFT_INNER)[None, :, None]
    bb = np.arange(FFT_INNER)[None, None, :]
    ang = ((d * (a + FFT_RADIX * bb)) % FFT_N) * (2.0 * np.pi / FFT_N)
    c = np.cos(ang) * (FFT_N ** -0.5)
    s = np.sin(ang) * (FFT_N ** -0.5)
    top = np.concatenate([c, -s], axis=2)
    bot = np.concatenate([-s, -c], axis=2)
    return jnp.asarray(np.concatenate([top, bot], axis=1).astype(np.float32), BF16)


FF_CHUNK = 256
TAIL_ROWS = 1024


def _tail_kernel(*refs, final):
    if final:
        (x_ref, m_ref, g1_ref, wm_ref, sh_ref, sc_ref, gate_ref, g_ref, wg_ref, wu_ref, wd_ref,
         fg_ref, o_ref, acc_ref) = refs
    else:
        (x_ref, m_ref, g1_ref, wm_ref, sh_ref, sc_ref, gate_ref, g_ref, wg_ref, wu_ref, wd_ref,
         o_ref, acc_ref) = refs
    x = x_ref[0] + g1_ref[0] * jnp.dot(m_ref[0], wm_ref[...], preferred_element_type=F32)
    h = _norm_mod(x, g_ref[...], sh_ref[0], sc_ref[0]).astype(BF16)
    for c in range(D_FF // FF_CHUNK):
        cs = slice(c * FF_CHUNK, (c + 1) * FF_CHUNK)
        gt = jnp.dot(h, wg_ref[:, cs], preferred_element_type=F32)
        up = jnp.dot(h, wu_ref[:, cs], preferred_element_type=F32)
        acc_ref[:, cs] = (_silu(gt) * up).astype(BF16)
    y = x + gate_ref[0] * jnp.dot(acc_ref[...], wd_ref[...], preferred_element_type=F32)
    if final:
        ms = jnp.mean(y * y, axis=-1, keepdims=True)
        y = (y * lax.rsqrt(ms + EPS)) * fg_ref[...]
    o_ref[0] = y


def _tail(x, m, gate1, wm, wm_layer, shift, scale, gate, g, wg, wu, wd, layer, tm, final_g=None):
    b, n, _ = x.shape
    tok = pl.BlockSpec((1, tm, D_MODEL), lambda i, bb: (bb, i, 0))
    mod = pl.BlockSpec((1, 1, D_MODEL), lambda i, bb: (bb, 0, 0))
    in_specs = [tok, tok, mod, _layer_spec((D_MODEL, D_MODEL), wm_layer), mod, mod, mod, _const_spec((1, D_MODEL)),
                _layer_spec((D_MODEL, D_FF), layer), _layer_spec((D_MODEL, D_FF), layer),
                _layer_spec((D_FF, D_MODEL), layer)]
    args = [x, m, gate1, wm, shift, scale, gate, g, wg, wu, wd]
    if final_g is not None:
        in_specs.append(_const_spec((1, D_MODEL)))
        args.append(final_g)
    return pl.pallas_call(
        functools.partial(_tail_kernel, final=final_g is not None),
        grid=(n // tm, b),
        in_specs=in_specs,
        out_specs=tok,
        out_shape=jax.ShapeDtypeStruct((b, n, D_MODEL), F32),
        scratch_shapes=[pltpu.VMEM((tm, D_FF), BF16)],
        compiler_params=_params(2),
        name="tail",
    )(*args)


def _rope_lane_tables(n):
    rows = n // GRID_W
    r = np.repeat(np.arange(rows), GRID_W).astype(np.float64)
    col = np.tile(np.arange(GRID_W), rows).astype(np.float64)
    quarter = HEAD_DIM // 4
    inv = ROPE_THETA ** (-np.arange(quarter, dtype=np.float64) / quarter)
    ang_r = r[:, None] * inv
    ang_c = col[:, None] * inv
    cr, sr, cc, sc = np.cos(ang_r), np.sin(ang_r), np.cos(ang_c), np.sin(ang_c)
    z = np.zeros_like(sr)
    cos = np.concatenate([cr, cr, cc, cc], axis=1)
    sa = np.concatenate([-sr, z, -sc, z], axis=1)
    sb = np.concatenate([z, sr, z, sc], axis=1)
    rep = lambda t: jnp.asarray(np.concatenate([t, t], axis=1).astype(np.float32))
    return rep(cos), rep(sa), rep(sb)


def _identity_lane_tables(n):
    one, zero = np.ones((n, LANES), np.float32), np.zeros((n, LANES), np.float32)
    return jnp.asarray(one), jnp.asarray(zero), jnp.asarray(zero)


def kernel(x, c, ctx, c_ctx, w_ada, b_ada, norm1_g, norm2_g, w_in, conv_w, sink,
           w_mix_out, w_fourier_out, w_ffn_gate, w_ffn_up, w_ffn_down, final_g):
    b, n, d = x.shape
    L = ctx.shape[1]
    nc = b * L
    tm = 512

    cond = jnp.concatenate([c, c_ctx[None, :], jnp.zeros((COND_ROWS - b - 1, d), F32)], axis=0)
    mods = _adaln(cond, w_ada, b_ada)

    def split(l, lo, hi):
        m = mods[l, lo:hi]
        return [m[:, None, k * d:(k + 1) * d] for k in range(6)]

    rope_lat = _rope_lane_tables(n)
    rope_ctx = _identity_lane_tables(nc)
    w_in_b, w_mix_b, w_fo_b = w_in.astype(BF16), w_mix_out.astype(BF16), w_fourier_out.astype(BF16)
    wg_b, wu_b, wd_b = w_ffn_gate.astype(BF16), w_ffn_up.astype(BF16), w_ffn_down.astype(BF16)
    cc, sc_ = _dft_mats(FOURIER_GROUP, FOURIER_GROUP ** -0.5)
    cs_chan = jnp.asarray(np.concatenate([cc, sc_], axis=1), BF16)
    seq_mats = {}
    for m in {n, L} - {FFT_N}:
        cm, sm = _dft_mats(m, m ** -0.5)
        seq_mats[m] = (jnp.asarray(cm, BF16), jnp.asarray(-sm, BF16))
    fft_l = _fft_stage_mats() if n == FFT_N else None

    xc = ctx.reshape(1, nc, d)
    per_seq = lambda t: t.reshape(b, L, t.shape[-1])
    for l in range(DEPTH):
        ctx_after = any(j % 2 == 0 for j in range(l + 1, DEPTH))
        ctx_here = (l % 2 == 0) or ctx_after
        sh1, sc1, g1, sh2, sc2, g2 = split(l, 0, b)
        if ctx_here:
            csh1, csc1, cg1, csh2, csc2, cg2 = split(l, b, b + 1)
        n1 = norm1_g[l][None, :]
        n2 = norm2_g[l][None, :]
        if l % 2 == 0:
            e = l // 2
            wm, wm_layer = w_mix_b, e
            bgu_c, q_c, kk_c, vv_c = _inproj(xc, csh1, csc1, n1, w_in_b, e, *rope_ctx, tm=min(nc, INPROJ_ROWS))
            kk_c, vv_c = per_seq(kk_c), per_seq(vv_c)
            bgu, q, kk, vv = _inproj(x, sh1, sc1, n1, w_in_b, e, *rope_lat, tm=min(n, INPROJ_ROWS))
            mix = _mixout(bgu, q, kk, vv, kk_c, vv_c, sink[e], conv_w[e], tile_rows=min(n, MIX_ROWS), local=True)
            if ctx_after:
                mix_c = _mixout(per_seq(bgu_c), per_seq(q_c), None, None, kk_c, vv_c, sink[e], conv_w[e],
                                tile_rows=L, local=False)
        else:
            wm, wm_layer = w_fo_b, l // 2
            if n == FFT_N:
                mix = _seq_fft(_chan_dft_perm(x, sh1, sc1, n1, cs_chan), fft_l)
            else:
                mix = _seq_dft(*seq_mats[n], *_chan_dft(x, sh1, sc1, n1, cs_chan, tm=tm), tm=256)
            if ctx_after:
                y1c, y2c = _chan_dft(xc, csh1, csc1, n1, cs_chan, tm=min(nc, tm))
                mix_c = _seq_dft(*seq_mats[L], per_seq(y1c), per_seq(y2c), tm=L)
        x = _tail(x, mix, g1, wm, wm_layer, sh2, sc2, g2, n2, wg_b, wu_b, wd_b, l, tm=min(n, TAIL_ROWS),
                  final_g=final_g[None, :] if l == DEPTH - 1 else None)
        if ctx_after:
            xc = _tail(xc, mix_c.reshape(1, nc, d), cg1, wm, wm_layer, csh2, csc2, cg2, n2, wg_b, wu_b, wd_b, l,
                       tm=min(nc, TAIL_ROWS))
    return x
```

```python
import functools

import numpy as np
import jax
import jax.numpy as jnp
from jax import lax
from jax.experimental import pallas as pl
from jax.experimental.pallas import tpu as pltpu

F32 = jnp.float32
BF16 = jnp.bfloat16

D_MODEL = 1024
DEPTH = 4
GRID_W = 64
HEAD_DIM = 64
N_Q_HEADS = 8
N_KV_HEADS = 2
ATTN_WIDTH = N_Q_HEADS * HEAD_DIM
KV_WIDTH = N_KV_HEADS * HEAD_DIM
CONV_WIDTH = D_MODEL - ATTN_WIDTH
WINDOW = 128
BLOCK = 128
ROPE_THETA = 10000.0
N_FOURIER_GROUPS = 4
FOURIER_GROUP = D_MODEL // N_FOURIER_GROUPS
D_FF = 2816
EPS = 1e-6
Q_START = 3 * CONV_WIDTH
K_START = Q_START + ATTN_WIDTH
V_START = K_START + KV_WIDTH

LANES = 128
SUBLANES = 8
KV_REP = 2 * KV_WIDTH
NEG_BIG = -1e30
COND_ROWS = 24
VMEM_LIMIT = 56 * 1024 * 1024


def _params(n_axes):
    return pltpu.CompilerParams(dimension_semantics=("arbitrary",) * n_axes,
                                vmem_limit_bytes=VMEM_LIMIT)


def _const_spec(shape):
    nd = len(shape)
    return pl.BlockSpec(shape, lambda *_: (0,) * nd, pipeline_mode=pl.Buffered(1))


def _layer_spec(shape, layer):
    nd = len(shape)
    return pl.BlockSpec((None,) + tuple(shape), lambda *_: (layer,) + (0,) * nd, pipeline_mode=pl.Buffered(1))


def _norm_mod(x, g, shift, scale):
    ms = jnp.mean(x * x, axis=-1, keepdims=True)
    y = x * lax.rsqrt(ms + EPS)
    return (y * g) * (1.0 + scale) + shift


def _silu(x):
    return x * (1.0 / (1.0 + jnp.exp(-x)))


def _adaln_kernel(c_ref, w_ref, b_ref, o_ref):
    a = _silu(c_ref[...])
    w = w_ref[0]
    a_hi = a.astype(BF16)
    a_lo = (a - a_hi.astype(F32)).astype(BF16)
    w_hi = w.astype(BF16)
    w_lo = (w - w_hi.astype(F32)).astype(BF16)
    acc = jnp.dot(a_hi, w_hi, preferred_element_type=F32)
    acc += jnp.dot(a_lo, w_hi, preferred_element_type=F32)
    acc += jnp.dot(a_hi, w_lo, preferred_element_type=F32)
    o_ref[0] = acc + b_ref[0]


def _adaln(cond, w_ada, b_ada):
    tn = 1024
    nt = (6 * D_MODEL) // tn
    return pl.pallas_call(
        _adaln_kernel,
        grid=(DEPTH, nt),
        in_specs=[
            pl.BlockSpec((COND_ROWS, D_MODEL), lambda l, j: (0, 0)),
            pl.BlockSpec((1, D_MODEL, tn), lambda l, j: (l, 0, j)),
            pl.BlockSpec((1, 1, tn), lambda l, j: (l, 0, j)),
        ],
        out_specs=pl.BlockSpec((1, COND_ROWS, tn), lambda l, j: (l, 0, j)),
        out_shape=jax.ShapeDtypeStruct((DEPTH, COND_ROWS, 6 * D_MODEL), F32),
        compiler_params=_params(2),
        name="adaln",
    )(cond, w_ada, b_ada.reshape(DEPTH, 1, 6 * D_MODEL))


IN_WIDTH = V_START + KV_WIDTH
LOG2E = float(np.log2(np.e))
Q_SCALE = HEAD_DIM ** -0.5 * LOG2E
INPROJ_ROWS = 2048
INPROJ_SUB = 512


def _inproj_kernel(x_ref, sh_ref, sc_ref, g_ref, w_ref, cos_ref, sa_ref, sb_ref,
                   bgu_ref, q_ref, kk_ref, vv_ref):
    first_half = lax.broadcasted_iota(jnp.int32, (1, LANES), 1) < HEAD_DIM
    tm = x_ref.shape[1]
    sub = min(tm, INPROJ_SUB)
    for s0 in range(0, tm, sub):
        rs = slice(s0, s0 + sub)
        h = _norm_mod(x_ref[0, rs, :], g_ref[...], sh_ref[0], sc_ref[0]).astype(BF16)
        p = jnp.dot(h, w_ref[...], preferred_element_type=F32)
        bgu_ref[0, rs, :CONV_WIDTH] = p[:, :CONV_WIDTH].astype(BF16)
        bgu_ref[0, rs, CONV_WIDTH:] = (p[:, CONV_WIDTH:2 * CONV_WIDTH]
                                       * p[:, 2 * CONV_WIDTH:Q_START]).astype(BF16)
        cos = cos_ref[rs, :]
        sa = sa_ref[rs, :]
        sb = sb_ref[rs, :]

        def rope(t):
            return t * cos + pltpu.roll(t, LANES - 16, 1) * sa + pltpu.roll(t, 16, 1) * sb

        for j in range(ATTN_WIDTH // LANES):
            lo = Q_START + j * LANES
            q_ref[0, rs, j * LANES:(j + 1) * LANES] = (rope(p[:, lo:lo + LANES]) * Q_SCALE).astype(BF16)
        for t, ref in ((rope(p[:, K_START:V_START]), kk_ref), (p[:, V_START:], vv_ref)):
            swapped = pltpu.roll(t, HEAD_DIM, 1)
            ref[0, rs, :LANES] = jnp.where(first_half, t, swapped).astype(BF16)
            ref[0, rs, LANES:] = jnp.where(first_half, swapped, t).astype(BF16)


def _inproj(x, shift, scale, g, w_in, layer, cos, sa, sb, tm):
    b, n, _ = x.shape
    nt = n // tm
    tok = lambda w: pl.BlockSpec((1, tm, w), lambda i, bb: (bb, i, 0))
    mod = pl.BlockSpec((1, 1, D_MODEL), lambda i, bb: (bb, 0, 0))
    tab = pl.BlockSpec((tm, LANES), lambda i, bb: (i, 0))
    return pl.pallas_call(
        _inproj_kernel,
        grid=(nt, b),
        in_specs=[tok(D_MODEL), mod, mod, _const_spec((1, D_MODEL)),
                  _layer_spec((D_MODEL, IN_WIDTH), layer), tab, tab, tab],
        out_specs=[tok(D_MODEL), tok(ATTN_WIDTH), tok(KV_REP), tok(KV_REP)],
        out_shape=[jax.ShapeDtypeStruct((b, n, D_MODEL), BF16),
                   jax.ShapeDtypeStruct((b, n, ATTN_WIDTH), BF16),
                   jax.ShapeDtypeStruct((b, n, KV_REP), BF16),
                   jax.ShapeDtypeStruct((b, n, KV_REP), BF16)],
        compiler_params=_params(2),
        name="inproj",
    )(x, shift, scale, g, w_in, cos, sa, sb)


CONV_ROWS = 32
SM_ROWS = 16
SCORE_SLOTS = 2
MIX_ROWS = 2048


def _mixout_kernel(*refs, tile_rows, local):
    if local:
        (sink_ref, bgu_ref, bgu_p_ref, bgu_n_ref, q_ref,
         kk_ref, kk_p_ref, kk_n_ref, vv_ref, vv_p_ref, vv_n_ref, kkc_ref, vvc_ref,
         bias_mid_ref, bias_a_ref, bias_b_ref, cw_ref, o_ref, s_slots, e_slots) = refs
    else:
        (sink_ref, bgu_ref, bgu_p_ref, bgu_n_ref, q_ref,
         kkc_ref, vvc_ref, cw_ref, o_ref, s_slots, e_slots) = refs
    i = pl.program_id(0)
    n_tiles = pl.num_programs(0)
    r = tile_rows
    nb = r // BLOCK

    has_prev = (i > 0).astype(F32)
    has_next = (i < n_tiles - 1).astype(F32)
    cw = cw_ref[...]
    cr = CONV_ROWS
    pk = 2 * SUBLANES
    row = lax.broadcasted_iota(jnp.int32, (cr, CONV_WIDTH), 0)

    def conv_chunk(r0):
        bg = bgu_ref[0, r0:r0 + cr, :CONV_WIDTH].astype(F32)
        u = bgu_ref[0, r0:r0 + cr, CONV_WIDTH:].astype(F32)
        if r0 == 0:
            u_prev = bgu_p_ref[0, SUBLANES - 1:SUBLANES, CONV_WIDTH:].astype(F32) * has_prev
        else:
            u_prev = bgu_ref[0, r0 - pk:r0, CONV_WIDTH:].astype(F32)[pk - 1:pk]
        if r0 + cr == r:
            u_next = bgu_n_ref[0, 0:1, CONV_WIDTH:].astype(F32) * has_next
        else:
            u_next = bgu_ref[0, r0 + cr:r0 + cr + pk, CONV_WIDTH:].astype(F32)[0:1]
        u_up = jnp.where(row == 0, u_prev, pltpu.roll(u, 1, 0))
        u_dn = jnp.where(row == cr - 1, u_next, pltpu.roll(u, cr - 1, 0))
        o_ref[0, r0:r0 + cr, :CONV_WIDTH] = (bg * (u_up * cw[0:1] + u * cw[1:2] + u_dn * cw[2:3])).astype(BF16)

    lane = lax.broadcasted_iota(jnp.int32, (1, LANES), 1)
    first_half = lane < HEAD_DIM
    zero = jnp.zeros((), BF16)
    nt_dims = (((1,), (1,)), ((), ()))

    def split_k(t):
        return jnp.where(first_half, t, zero), jnp.where(first_half, zero, t)

    def split_v(t):
        va, vb = split_k(t)
        first = lax.broadcasted_iota(jnp.int32, t.shape, 1) < HEAD_DIM
        ones_a = jnp.where(first, 1.0, 0.0).astype(BF16)
        ones_b = jnp.where(first, 0.0, 1.0).astype(BF16)
        return jnp.concatenate([va, ones_a], axis=1), jnp.concatenate([vb, ones_b], axis=1)

    n_ctx = kkc_ref.shape[1]
    wc = 2 * n_ctx
    nl = 3 * BLOCK
    groups_per_pair = BLOCK // SM_ROWS
    cols_a = [slice(0, n_ctx)] + ([slice(wc, wc + nl)] if local else [])
    cols_b = [slice(n_ctx, wc)] + ([slice(wc + nl, wc + 2 * nl)] if local else [])

    heads = []
    for h in range(N_KV_HEADS):
        hs = slice(h * LANES, (h + 1) * LANES)
        hd = dict(kc_bd=jnp.concatenate(split_k(kkc_ref[0, :, hs]), axis=0),
                  vc_ext=jnp.concatenate(split_v(vvc_ref[0, :, hs]), axis=0),
                  sinks=[sink_ref[4 * h + i] * LOG2E for i in range(4)])
        if local:
            hd["k"] = split_k(jnp.concatenate([kk_p_ref[0, :, hs], kk_ref[0, :, hs], kk_n_ref[0, :, hs]], axis=0))
            hd["v"] = split_v(jnp.concatenate([vv_p_ref[0, :, hs], vv_ref[0, :, hs], vv_n_ref[0, :, hs]], axis=0))
        heads.append(hd)

    blocks = [(h, j) for h in range(N_KV_HEADS) for j in range(nb)]

    def scores(idx):
        h, j = blocks[idx]
        hd = heads[h]
        rs = slice(j * BLOCK, (j + 1) * BLOCK)
        q2 = jnp.concatenate([q_ref[0, rs, 2 * h * LANES:(2 * h + 1) * LANES],
                              q_ref[0, rs, (2 * h + 1) * LANES:(2 * h + 2) * LANES]], axis=0)
        s_ref = s_slots.at[idx % SCORE_SLOTS]
        s_ref[:, :wc] = lax.dot_general(q2, hd["kc_bd"], nt_dims, preferred_element_type=F32)
        if local:
            ws = slice(j * BLOCK, j * BLOCK + nl)
            kl_bd = jnp.concatenate([hd["k"][0][ws], hd["k"][1][ws]], axis=0)
            if nb == 1:
                bias = bias_a_ref[0] + bias_b_ref[0]
            elif j == 0:
                bias = bias_a_ref[0]
            elif j == nb - 1:
                bias = bias_b_ref[0]
            else:
                bias = bias_mid_ref[0]
            s_loc = lax.dot_general(q2, kl_bd, nt_dims, preferred_element_type=F32)
            for c0 in range(0, 2 * nl, BLOCK):
                seg = s_loc[:, c0:c0 + BLOCK]
                if (c0 // BLOCK) % 3 != 1:
                    seg = seg + bias[:, c0:c0 + BLOCK]
                s_ref[:, wc + c0:wc + c0 + BLOCK] = seg

    def softmax_pv(idx):
        h, j = blocks[idx]
        hd = heads[h]
        s_ref = s_slots.at[idx % SCORE_SLOTS]
        e_ref = e_slots.at[idx % SCORE_SLOTS]
        sink_terms = []
        for g in range(2 * BLOCK // SM_ROWS):
            gr = slice(g * SM_ROWS, (g + 1) * SM_ROWS)
            pair = g // groups_per_pair
            terms = []
            for cols, sk in ((cols_a, hd["sinks"][2 * pair]), (cols_b, hd["sinks"][2 * pair + 1])):
                ss = [s_ref[gr, c] for c in cols]
                m = jnp.max(ss[0], axis=-1, keepdims=True)
                for t in ss[1:]:
                    m = jnp.maximum(m, jnp.max(t, axis=-1, keepdims=True))
                m = jnp.maximum(m, sk)
                for c, t in zip(cols, ss):
                    e_ref[gr, c] = jnp.exp2(t - m).astype(BF16)
                terms.append(jnp.exp2(sk - m))
            sink_terms.append(jnp.where(first_half, terms[0], terms[1]))
        o_ext = jnp.dot(e_ref[:, :wc], hd["vc_ext"], preferred_element_type=F32)
        if local:
            ws = slice(j * BLOCK, j * BLOCK + nl)
            vl_ext = jnp.concatenate([hd["v"][0][ws], hd["v"][1][ws]], axis=0)
            o_ext = o_ext + jnp.dot(e_ref[:, wc:], vl_ext, preferred_element_type=F32)
        o = o_ext[:, :LANES] / (o_ext[:, LANES:] + jnp.concatenate(sink_terms, axis=0))
        return jnp.concatenate([o[:BLOCK], o[BLOCK:]], axis=1)

    conv_starts = list(range(0, r, cr))
    per_block = -(-len(conv_starts) // len(blocks))
    outs = {}
    scores(0)
    for idx in range(len(blocks)):
        if idx + 1 < len(blocks):
            scores(idx + 1)
        for r0 in conv_starts[idx * per_block:(idx + 1) * per_block]:
            conv_chunk(r0)
        outs[blocks[idx]] = softmax_pv(idx)
    b_cols = [jnp.concatenate([outs[(h, j)] for j in range(nb)], axis=0) for h in range(N_KV_HEADS)]
    o_ref[0, :, CONV_WIDTH:] = jnp.concatenate(b_cols, axis=1).astype(BF16)


def _band_bias():
    rr = np.arange(2 * BLOCK)[:, None] % BLOCK
    kidx = np.arange(6 * BLOCK)[None, :] % (3 * BLOCK)
    band = (kidx >= rr) & (kidx <= rr + 2 * WINDOW)
    mid = band
    first = band & (kidx >= BLOCK)
    last = band & (kidx < 2 * BLOCK)
    return jnp.asarray(np.where(np.stack([mid, first, last]), 0.0, NEG_BIG), dtype=F32)


def _mixout(bgu, q, kk, vv, kkc, vvc, sink, conv_w, tile_rows, local):
    b, n, _ = bgu.shape
    r = tile_rows
    nt = n // r
    nctx = kkc.shape[1]
    tok = lambda w: pl.BlockSpec((1, r, w), lambda i, bb: (bb, i, 0))
    r8 = r // SUBLANES
    halo8_p = pl.BlockSpec((1, SUBLANES, D_MODEL), lambda i, bb: (bb, jnp.maximum(i * r8 - 1, 0), 0))
    halo8_n = pl.BlockSpec((1, SUBLANES, D_MODEL),
                           lambda i, bb: (bb, jnp.minimum((i + 1) * r8, n // SUBLANES - 1), 0))
    ctx = pl.BlockSpec((1, nctx, KV_REP), lambda i, bb: (bb, 0, 0))
    smem = pl.BlockSpec(memory_space=pltpu.SMEM)
    in_specs = [smem, tok(D_MODEL), halo8_p, halo8_n, tok(ATTN_WIDTH)]
    args = [sink, bgu, bgu, bgu, q]
    if local:
        rb = r // BLOCK
        halo_p = pl.BlockSpec((1, BLOCK, KV_REP), lambda i, bb: (bb, jnp.maximum(i * rb - 1, 0), 0))
        halo_n = pl.BlockSpec((1, BLOCK, KV_REP),
                              lambda i, bb: (bb, jnp.minimum((i + 1) * rb, n // BLOCK - 1), 0))
        bias = _band_bias()
        bshape = (1, 2 * BLOCK, 6 * BLOCK)
        in_specs += [tok(KV_REP), halo_p, halo_n, tok(KV_REP), halo_p, halo_n, ctx, ctx,
                     pl.BlockSpec(bshape, lambda i, bb: (0, 0, 0)),
                     pl.BlockSpec(bshape, lambda i, bb: (jnp.where(i == 0, 1, 0), 0, 0)),
                     pl.BlockSpec(bshape, lambda i, bb: (jnp.where(i == nt - 1, 2, 0), 0, 0))]
        args += [kk, kk, kk, vv, vv, vv, kkc, vvc, bias, bias, bias]
    else:
        in_specs += [ctx, ctx]
        args += [kkc, vvc]
    in_specs += [_const_spec((3, CONV_WIDTH))]
    args += [conv_w]
    score_cols = 2 * nctx + (6 * BLOCK if local else 0)
    return pl.pallas_call(
        functools.partial(_mixout_kernel, tile_rows=r, local=local),
        grid=(nt, b),
        in_specs=in_specs,
        out_specs=tok(D_MODEL),
        out_shape=jax.ShapeDtypeStruct((b, n, D_MODEL), BF16),
        scratch_shapes=[pltpu.VMEM((SCORE_SLOTS, 2 * BLOCK, score_cols), F32),
                        pltpu.VMEM((SCORE_SLOTS, 2 * BLOCK, score_cols), BF16)],
        compiler_params=_params(2),
        name="mixout_local" if local else "mixout_ctx",
    )(*args)


def _chan_dft_kernel(x_ref, sh_ref, sc_ref, g_ref, cs_ref, y1_ref, y2_ref):
    h = _norm_mod(x_ref[0], g_ref[...], sh_ref[0], sc_ref[0]).astype(BF16)
    fg = FOURIER_GROUP
    for gi in range(N_FOURIER_GROUPS):
        yc = jnp.dot(h[:, gi * fg:(gi + 1) * fg], cs_ref[...], preferred_element_type=F32)
        y1_ref[0, :, gi * fg:(gi + 1) * fg] = yc[:, :fg].astype(BF16)
        y2_ref[0, :, gi * fg:(gi + 1) * fg] = yc[:, fg:].astype(BF16)


def _chan_dft(x, shift, scale, g, cs, tm):
    b, n, _ = x.shape
    tok = pl.BlockSpec((1, tm, D_MODEL), lambda i, bb: (bb, i, 0))
    mod = pl.BlockSpec((1, 1, D_MODEL), lambda i, bb: (bb, 0, 0))
    return pl.pallas_call(
        _chan_dft_kernel,
        grid=(n // tm, b),
        in_specs=[tok, mod, mod, _const_spec((1, D_MODEL)), _const_spec((FOURIER_GROUP, 2 * FOURIER_GROUP))],
        out_specs=[tok, tok],
        out_shape=[jax.ShapeDtypeStruct((b, n, D_MODEL), BF16)] * 2,
        compiler_params=_params(2),
        name="chan_dft",
    )(x, shift, scale, g, cs)


def _seq_dft_kernel(cn_ref, sn_ref, y1_ref, y2_ref, o_ref):
    z = jnp.dot(cn_ref[...], y1_ref[0], preferred_element_type=F32)
    z = z + jnp.dot(sn_ref[...], y2_ref[0], preferred_element_type=F32)
    o_ref[0] = z.astype(BF16)


def _seq_dft(cn, nsn, y1, y2, tm):
    b, n, _ = y1.shape
    tok = pl.BlockSpec((1, tm, D_MODEL), lambda bb, i: (bb, i, 0))
    mat = pl.BlockSpec((tm, n), lambda bb, i: (i, 0))
    full = pl.BlockSpec((1, n, D_MODEL), lambda bb, i: (bb, 0, 0), pipeline_mode=pl.Buffered(1))
    return pl.pallas_call(
        _seq_dft_kernel,
        grid=(b, n // tm),
        in_specs=[mat, mat, full, full],
        out_specs=tok,
        out_shape=jax.ShapeDtypeStruct((b, n, D_MODEL), BF16),
        compiler_params=_params(2),
        name="seq_dft",
    )(cn, nsn, y1, y2)


def _dft_mats(n, scale):
    j = np.arange(n)
    ang = ((j[:, None] * j[None, :]) % n) * (2.0 * np.pi / n)
    return (np.cos(ang) * scale).astype(np.float32), (np.sin(ang) * scale).astype(np.float32)


FFT_RADIX = 16
FFT_INNER = 256
FFT_N = FFT_RADIX * FFT_INNER
PERM_TILE = 256
PERM_B = PERM_TILE // FFT_RADIX
CHAN_TILE = 2 * PERM_TILE
PERM_STEP = 4 * CHAN_TILE


def _chan_dft_perm_kernel(x_ref, sh_ref, sc_ref, g_ref, p_ref, cs_ref, y_ref):
    fg = FOURIER_GROUP
    pieces = CHAN_TILE // PERM_TILE
    for t in range(x_ref.shape[1] // CHAN_TILE):
        h = _norm_mod(x_ref[0, t * CHAN_TILE:(t + 1) * CHAN_TILE, :], g_ref[...], sh_ref[0], sc_ref[0]).astype(BF16)
        hp = jnp.concatenate(
            [jnp.dot(p_ref[...], h[k * PERM_TILE:(k + 1) * PERM_TILE], preferred_element_type=F32).astype(BF16)
             for k in range(pieces)], axis=0)
        for gi in range(N_FOURIER_GROUPS):
            gs = slice(gi * fg, (gi + 1) * fg)
            yc = jnp.dot(hp[:, gs], cs_ref[...], preferred_element_type=F32)
            for k in range(pieces):
                for a in range(FFT_RADIX):
                    rs = slice(k * PERM_TILE + a * PERM_B, k * PERM_TILE + (a + 1) * PERM_B)
                    ws = slice((t * pieces + k) * PERM_B, (t * pieces + k + 1) * PERM_B)
                    y_ref[0, gi, a, 0, ws, :] = yc[rs, :fg].astype(BF16)
                    y_ref[0, gi, a, 1, ws, :] = yc[rs, fg:].astype(BF16)


def _chan_dft_perm(x, shift, scale, g, cs):
    b, n, _ = x.shape
    r_out = np.arange(PERM_TILE)
    perm = np.zeros((PERM_TILE, PERM_TILE), np.float32)
    perm[r_out, FFT_RADIX * (r_out % PERM_B) + r_out // PERM_B] = 1.0
    tok = pl.BlockSpec((1, PERM_STEP, D_MODEL), lambda i, bb: (bb, i, 0))
    mod = pl.BlockSpec((1, 1, D_MODEL), lambda i, bb: (bb, 0, 0))
    return pl.pallas_call(
        _chan_dft_perm_kernel,
        grid=(n // PERM_STEP, b),
        in_specs=[tok, mod, mod, _const_spec((1, D_MODEL)), _const_spec((PERM_TILE, PERM_TILE)),
                  _const_spec((FOURIER_GROUP, 2 * FOURIER_GROUP))],
        out_specs=pl.BlockSpec((1, N_FOURIER_GROUPS, FFT_RADIX, 2, PERM_STEP // FFT_RADIX, FOURIER_GROUP),
                               lambda i, bb: (bb, 0, 0, 0, i, 0)),
        out_shape=jax.ShapeDtypeStruct((b, N_FOURIER_GROUPS, FFT_RADIX, 2, n // FFT_RADIX, FOURIER_GROUP), BF16),
        compiler_params=_params(2),
        name="chan_dft_perm",
    )(x, shift, scale, g, jnp.asarray(perm, BF16), cs)


_C1 = float(np.cos(np.pi / 8))
_C2 = float(np.cos(np.pi / 4))
_C3 = float(np.cos(3 * np.pi / 8))


def _dft16_real(ur, ui):
    p = [ur[0]] + [ur[a] + ur[16 - a] for a in range(1, 8)] + [ur[8]]
    q = [p[a] + p[8 - a] for a in range(4)] + [p[4]]
    r = [p[a] - p[8 - a] for a in range(4)]
    s0, s1, s2 = q[0] + q[4], q[1] + q[3], q[2]
    t0, t1 = q[0] - q[4], q[1] - q[3]
    ca = [None] * 9
    ca[0] = s0 + s1 + s2
    ca[4] = s0 - s2
    ca[8] = s0 - s1 + s2
    ca[2] = t0 + _C2 * t1
    ca[6] = t0 - _C2 * t1
    e0, e1 = r[0] + _C2 * r[2], r[0] - _C2 * r[2]
    f, g = _C1 * r[1] + _C3 * r[3], _C3 * r[1] - _C1 * r[3]
    ca[1], ca[7], ca[3], ca[5] = e0 + f, e0 - f, e1 + g, e1 - g
    pp = [None] + [ui[a] - ui[16 - a] for a in range(1, 8)]
    qq = [None] + [pp[a] + pp[8 - a] for a in range(1, 4)] + [pp[4]]
    rr = [None] + [pp[a] - pp[8 - a] for a in range(1, 4)]
    sb = [None] * 8
    e0, e1 = _C2 * qq[2] + qq[4], _C2 * qq[2] - qq[4]
    f, g = _C3 * qq[1] + _C1 * qq[3], _C1 * qq[1] - _C3 * qq[3]
    sb[1], sb[7], sb[3], sb[5] = f + e0, f - e0, g + e1, g - e1
    w = _C2 * (rr[1] + rr[3])
    sb[2], sb[6], sb[4] = w + rr[2], w - rr[2], rr[1] - rr[3]
    out = [None] * 16
    out[0], out[8] = ca[0], ca[8]
    for c in range(1, 8):
        out[c] = ca[c] + sb[c]
        out[16 - c] = ca[c] - sb[c]
    return out


def _seq_fft_kernel(y_ref, l_ref, o_ref, u_even_ref, u_odd_ref):
    t = pl.program_id(0)

    @pl.when(t == 0)
    def _():
        u_odd_ref[...] = jnp.zeros(u_odd_ref.shape, F32)

    def step(cur_ref, prev_ref):
        rt = 2 * SUBLANES
        for i in range(FFT_INNER // rt):
            r0 = i * rt
            for lh in range(FOURIER_GROUP // LANES):
                ls = slice(lh * LANES, (lh + 1) * LANES)
                ur = [prev_ref[a, r0:r0 + rt, ls] for a in range(FFT_RADIX)]
                ui = [prev_ref[a, FFT_INNER + r0:FFT_INNER + r0 + rt, ls] for a in range(FFT_RADIX)]
                out = _dft16_real(ur, ui)
                for c in range(FFT_RADIX):
                    o_ref[0, c * FFT_INNER + r0:c * FFT_INNER + r0 + rt, ls] = out[c].astype(BF16)
        for a in range(FFT_RADIX):
            u = jnp.dot(l_ref[a, :, :FFT_INNER], y_ref[a, 0], preferred_element_type=F32)
            cur_ref[a] = u + jnp.dot(l_ref[a, :, FFT_INNER:], y_ref[a, 1], preferred_element_type=F32)

    pl.when(t % 2 == 0)(lambda: step(u_even_ref, u_odd_ref))
    pl.when(t % 2 == 1)(lambda: step(u_odd_ref, u_even_ref))


def _seq_fft(ycat, lmat):
    b = ycat.shape[0]
    ng = N_FOURIER_GROUPS
    items = b * ng

    def y_map(t):
        tt = jnp.minimum(t, items - 1)
        return (tt // ng, tt % ng, 0, 0, 0, 0)

    def o_map(t):
        tp = jnp.maximum(t - 1, 0)
        return (tp // ng, 0, tp % ng)

    return pl.pallas_call(
        _seq_fft_kernel,
        grid=(items + 1,),
        in_specs=[pl.BlockSpec((None, None, FFT_RADIX, 2, FFT_INNER, FOURIER_GROUP), y_map),
                  _const_spec((FFT_RADIX, 2 * FFT_INNER, 2 * FFT_INNER))],
        out_specs=pl.BlockSpec((1, FFT_N, FOURIER_GROUP), o_map),
        out_shape=jax.ShapeDtypeStruct((b, FFT_N, D_MODEL), BF16),
        scratch_shapes=[pltpu.VMEM((FFT_RADIX, 2 * FFT_INNER, FOURIER_GROUP), F32)] * 2,
        compiler_params=_params(1),
        name="seq_fft",
    )(ycat, lmat)


def _fft_stage_mats():
    a = np.arange(FFT_RADIX)[:, None, None]
    d = np.arange(FFT_INNER)[None, :, None]
    bb = np.arange(FFT_INNER)[None, None, :]
    ang = ((d * (a + FFT_RADIX * bb)) % FFT_N) * (2.0 * np.pi / FFT_N)
    c = np.cos(ang) * (FFT_N ** -0.5)
    s = np.sin(ang) * (FFT_N ** -0.5)
    top = np.concatenate([c, -s], axis=2)
    bot = np.concatenate([-s, -c], axis=2)
    return jnp.asarray(np.concatenate([top, bot], axis=1).astype(np.float32), BF16)


FF_CHUNK = 256
TAIL_ROWS = 1024


def _tail_kernel(*refs, final):
    if final:
        (x_ref, m_ref, g1_ref, wm_ref, sh_ref, sc_ref, gate_ref, g_ref, wg_ref, wu_ref, wd_ref,
         fg_ref, o_ref, acc_ref) = refs
    else:
        (x_ref, m_ref, g1_ref, wm_ref, sh_ref, sc_ref, gate_ref, g_ref, wg_ref, wu_ref, wd_ref,
         o_ref, acc_ref) = refs
    x = x_ref[0] + g1_ref[0] * jnp.dot(m_ref[0], wm_ref[...], preferred_element_type=F32)
    h = _norm_mod(x, g_ref[...], sh_ref[0], sc_ref[0]).astype(BF16)
    for c in range(D_FF // FF_CHUNK):
        cs = slice(c * FF_CHUNK, (c + 1) * FF_CHUNK)
        gt = jnp.dot(h, wg_ref[:, cs], preferred_element_type=F32)
        up = jnp.dot(h, wu_ref[:, cs], preferred_element_type=F32)
        acc_ref[:, cs] = (_silu(gt) * up).astype(BF16)
    y = x + gate_ref[0] * jnp.dot(acc_ref[...], wd_ref[...], preferred_element_type=F32)
    if final:
        ms = jnp.mean(y * y, axis=-1, keepdims=True)
        y = (y * lax.rsqrt(ms + EPS)) * fg_ref[...]
    o_ref[0] = y


def _tail(x, m, gate1, wm, wm_layer, shift, scale, gate, g, wg, wu, wd, layer, tm, final_g=None):
    b, n, _ = x.shape
    tok = pl.BlockSpec((1, tm, D_MODEL), lambda i, bb: (bb, i, 0))
    mod = pl.BlockSpec((1, 1, D_MODEL), lambda i, bb: (bb, 0, 0))
    in_specs = [tok, tok, mod, _layer_spec((D_MODEL, D_MODEL), wm_layer), mod, mod, mod, _const_spec((1, D_MODEL)),
                _layer_spec((D_MODEL, D_FF), layer), _layer_spec((D_MODEL, D_FF), layer),
                _layer_spec((D_FF, D_MODEL), layer)]
    args = [x, m, gate1, wm, shift, scale, gate, g, wg, wu, wd]
    if final_g is not None:
        in_specs.append(_const_spec((1, D_MODEL)))
        args.append(final_g)
    return pl.pallas_call(
        functools.partial(_tail_kernel, final=final_g is not None),
        grid=(n // tm, b),
        in_specs=in_specs,
        out_specs=tok,
        out_shape=jax.ShapeDtypeStruct((b, n, D_MODEL), F32),
        scratch_shapes=[pltpu.VMEM((tm, D_FF), BF16)],
        compiler_params=_params(2),
        name="tail",
    )(*args)


def _rope_lane_tables(n):
    rows = n // GRID_W
    r = np.repeat(np.arange(rows), GRID_W).astype(np.float64)
    col = np.tile(np.arange(GRID_W), rows).astype(np.float64)
    quarter = HEAD_DIM // 4
    inv = ROPE_THETA ** (-np.arange(quarter, dtype=np.float64) / quarter)
    ang_r = r[:, None] * inv
    ang_c = col[:, None] * inv
    cr, sr, cc, sc = np.cos(ang_r), np.sin(ang_r), np.cos(ang_c), np.sin(ang_c)
    z = np.zeros_like(sr)
    cos = np.concatenate([cr, cr, cc, cc], axis=1)
    sa = np.concatenate([-sr, z, -sc, z], axis=1)
    sb = np.concatenate([z, sr, z, sc], axis=1)
    rep = lambda t: jnp.asarray(np.concatenate([t, t], axis=1).astype(np.float32))
    return rep(cos), rep(sa), rep(sb)


def _identity_lane_tables(n):
    one, zero = np.ones((n, LANES), np.float32), np.zeros((n, LANES), np.float32)
    return jnp.asarray(one), jnp.asarray(zero), jnp.asarray(zero)


def kernel(x, c, ctx, c_ctx, w_ada, b_ada, norm1_g, norm2_g, w_in, conv_w, sink,
           w_mix_out, w_fourier_out, w_ffn_gate, w_ffn_up, w_ffn_down, final_g):
    b, n, d = x.shape
    L = ctx.shape[1]
    nc = b * L
    tm = 512

    cond = jnp.concatenate([c, c_ctx[None, :], jnp.zeros((COND_ROWS - b - 1, d), F32)], axis=0)
    mods = _adaln(cond, w_ada, b_ada)

    def split(l, lo, hi):
        m = mods[l, lo:hi]
        return [m[:, None, k * d:(k + 1) * d] for k in range(6)]

    rope_lat = _rope_lane_tables(n)
    rope_ctx = _identity_lane_tables(nc)
    w_in_b, w_mix_b, w_fo_b = w_in.astype(BF16), w_mix_out.astype(BF16), w_fourier_out.astype(BF16)
    wg_b, wu_b, wd_b = w_ffn_gate.astype(BF16), w_ffn_up.astype(BF16), w_ffn_down.astype(BF16)
    cc, sc_ = _dft_mats(FOURIER_GROUP, FOURIER_GROUP ** -0.5)
    cs_chan = jnp.asarray(np.concatenate([cc, sc_], axis=1), BF16)
    seq_mats = {}
    for m in {n, L} - {FFT_N}:
        cm, sm = _dft_mats(m, m ** -0.5)
        seq_mats[m] = (jnp.asarray(cm, BF16), jnp.asarray(-sm, BF16))
    fft_l = _fft_stage_mats() if n == FFT_N else None

    xc = ctx.reshape(1, nc, d)
    per_seq = lambda t: t.reshape(b, L, t.shape[-1])
    for l in range(DEPTH):
        ctx_after = any(j % 2 == 0 for j in range(l + 1, DEPTH))
        ctx_here = (l % 2 == 0) or ctx_after
        sh1, sc1, g1, sh2, sc2, g2 = split(l, 0, b)
        if ctx_here:
            csh1, csc1, cg1, csh2, csc2, cg2 = split(l, b, b + 1)
        n1 = norm1_g[l][None, :]
        n2 = norm2_g[l][None, :]
        if l % 2 == 0:
            e = l // 2
            wm, wm_layer = w_mix_b, e
            bgu_c, q_c, kk_c, vv_c = _inproj(xc, csh1, csc1, n1, w_in_b, e, *rope_ctx, tm=min(nc, INPROJ_ROWS))
            kk_c, vv_c = per_seq(kk_c), per_seq(vv_c)
            bgu, q, kk, vv = _inproj(x, sh1, sc1, n1, w_in_b, e, *rope_lat, tm=min(n, INPROJ_ROWS))
            mix = _mixout(bgu, q, kk, vv, kk_c, vv_c, sink[e], conv_w[e], tile_rows=min(n, MIX_ROWS), local=True)
            if ctx_after:
                mix_c = _mixout(per_seq(bgu_c), per_seq(q_c), None, None, kk_c, vv_c, sink[e], conv_w[e],
                                tile_rows=L, local=False)
        else:
            wm, wm_layer = w_fo_b, l // 2
            if n == FFT_N:
                mix = _seq_fft(_chan_dft_perm(x, sh1, sc1, n1, cs_chan), fft_l)
            else:
                mix = _seq_dft(*seq_mats[n], *_chan_dft(x, sh1, sc1, n1, cs_chan, tm=tm), tm=256)
            if ctx_after:
                y1c, y2c = _chan_dft(xc, csh1, csc1, n1, cs_chan, tm=min(nc, tm))
                mix_c = _seq_dft(*seq_mats[L], per_seq(y1c), per_seq(y2c), tm=L)
        x = _tail(x, mix, g1, wm, wm_layer, sh2, sc2, g2, n2, wg_b, wu_b, wd_b, l, tm=min(n, TAIL_ROWS),
                  final_g=final_g[None, :] if l == DEPTH - 1 else None)
        if ctx_after:
            xc = _tail(xc, mix_c.reshape(1, nc, d), cg1, wm, wm_layer, csh2, csc2, cg2, n2, wg_b, wu_b, wd_b, l,
                       tm=min(nc, TAIL_ROWS))
    return x
```

```python
import functools

import numpy as np
import jax
import jax.numpy as jnp
from jax import lax
from jax.experimental import pallas as pl
from jax.experimental.pallas import tpu as pltpu

F32 = jnp.float32
BF16 = jnp.bfloat16

D_MODEL = 1024
DEPTH = 4
GRID_W = 64
HEAD_DIM = 64
N_Q_HEADS = 8
N_KV_HEADS = 2
ATTN_WIDTH = N_Q_HEADS * HEAD_DIM
KV_WIDTH = N_KV_HEADS * HEAD_DIM
CONV_WIDTH = D_MODEL - ATTN_WIDTH
WINDOW = 128
BLOCK = 128
ROPE_THETA = 10000.0
N_FOURIER_GROUPS = 4
FOURIER_GROUP = D_MODEL // N_FOURIER_GROUPS
D_FF = 2816
EPS = 1e-6
Q_START = 3 * CONV_WIDTH
K_START = Q_START + ATTN_WIDTH
V_START = K_START + KV_WIDTH

LANES = 128
SUBLANES = 8
KV_REP = 2 * KV_WIDTH
NEG_BIG = -1e30
COND_ROWS = 24
VMEM_LIMIT = 56 * 1024 * 1024
ADALN_COLS = 1024
CHAN_DFT_ROWS = 512
SEQ_DFT_ROWS = 256


def _params(n_axes):
    return pltpu.CompilerParams(dimension_semantics=("arbitrary",) * n_axes,
                                vmem_limit_bytes=VMEM_LIMIT)


def _const_spec(shape):
    nd = len(shape)
    return pl.BlockSpec(shape, lambda *_: (0,) * nd, pipeline_mode=pl.Buffered(1))


def _layer_spec(shape, layer):
    nd = len(shape)
    return pl.BlockSpec((None,) + tuple(shape), lambda *_: (layer,) + (0,) * nd, pipeline_mode=pl.Buffered(1))


def _norm_mod(x, g, shift, scale):
    ms = jnp.mean(x * x, axis=-1, keepdims=True)
    y = x * lax.rsqrt(ms + EPS)
    return (y * g) * (1.0 + scale) + shift


def _silu(x):
    return x * (1.0 / (1.0 + jnp.exp(-x)))


def _adaln_kernel(c_ref, w_ref, b_ref, o_ref):
    a = _silu(c_ref[...])
    w = w_ref[0]
    a_hi = a.astype(BF16)
    a_lo = (a - a_hi.astype(F32)).astype(BF16)
    w_hi = w.astype(BF16)
    w_lo = (w - w_hi.astype(F32)).astype(BF16)
    acc = jnp.dot(a_hi, w_hi, preferred_element_type=F32)
    acc += jnp.dot(a_lo, w_hi, preferred_element_type=F32)
    acc += jnp.dot(a_hi, w_lo, preferred_element_type=F32)
    o_ref[0] = acc + b_ref[0]


def _adaln(cond, w_ada, b_ada):
    tn = ADALN_COLS
    nt = (6 * D_MODEL) // tn
    return pl.pallas_call(
        _adaln_kernel,
        grid=(DEPTH, nt),
        in_specs=[
            pl.BlockSpec((COND_ROWS, D_MODEL), lambda l, j: (0, 0)),
            pl.BlockSpec((1, D_MODEL, tn), lambda l, j: (l, 0, j)),
            pl.BlockSpec((1, 1, tn), lambda l, j: (l, 0, j)),
        ],
        out_specs=pl.BlockSpec((1, COND_ROWS, tn), lambda l, j: (l, 0, j)),
        out_shape=jax.ShapeDtypeStruct((DEPTH, COND_ROWS, 6 * D_MODEL), F32),
        compiler_params=_params(2),
        name="adaln",
    )(cond, w_ada, b_ada.reshape(DEPTH, 1, 6 * D_MODEL))


IN_WIDTH = V_START + KV_WIDTH
LOG2E = float(np.log2(np.e))
Q_SCALE = HEAD_DIM ** -0.5 * LOG2E
INPROJ_ROWS = 2048
INPROJ_SUB = 512


def _inproj_kernel(x_ref, sh_ref, sc_ref, g_ref, w_ref, cos_ref, sa_ref, sb_ref,
                   bgu_ref, q_ref, kk_ref, vv_ref):
    first_half = lax.broadcasted_iota(jnp.int32, (1, LANES), 1) < HEAD_DIM
    tm = x_ref.shape[1]
    sub = min(tm, INPROJ_SUB)
    for s0 in range(0, tm, sub):
        rs = slice(s0, s0 + sub)
        h = _norm_mod(x_ref[0, rs, :], g_ref[...], sh_ref[0], sc_ref[0]).astype(BF16)
        p = jnp.dot(h, w_ref[...], preferred_element_type=F32)
        bgu_ref[0, rs, :CONV_WIDTH] = p[:, :CONV_WIDTH].astype(BF16)
        bgu_ref[0, rs, CONV_WIDTH:] = (p[:, CONV_WIDTH:2 * CONV_WIDTH]
                                       * p[:, 2 * CONV_WIDTH:Q_START]).astype(BF16)
        cos = cos_ref[rs, :]
        sa = sa_ref[rs, :]
        sb = sb_ref[rs, :]

        def rope(t):
            return t * cos + pltpu.roll(t, LANES - 16, 1) * sa + pltpu.roll(t, 16, 1) * sb

        for j in range(ATTN_WIDTH // LANES):
            lo = Q_START + j * LANES
            q_ref[0, rs, j * LANES:(j + 1) * LANES] = (rope(p[:, lo:lo + LANES]) * Q_SCALE).astype(BF16)
        for t, ref in ((rope(p[:, K_START:V_START]), kk_ref), (p[:, V_START:], vv_ref)):
            swapped = pltpu.roll(t, HEAD_DIM, 1)
            ref[0, rs, :LANES] = jnp.where(first_half, t, swapped).astype(BF16)
            ref[0, rs, LANES:] = jnp.where(first_half, swapped, t).astype(BF16)


def _inproj(x, shift, scale, g, w_in, layer, cos, sa, sb, tm):
    b, n, _ = x.shape
    nt = n // tm
    tok = lambda w: pl.BlockSpec((1, tm, w), lambda i, bb: (bb, i, 0))
    mod = pl.BlockSpec((1, 1, D_MODEL), lambda i, bb: (bb, 0, 0))
    tab = pl.BlockSpec((tm, LANES), lambda i, bb: (i, 0))
    return pl.pallas_call(
        _inproj_kernel,
        grid=(nt, b),
        in_specs=[tok(D_MODEL), mod, mod, _const_spec((1, D_MODEL)),
                  _layer_spec((D_MODEL, IN_WIDTH), layer), tab, tab, tab],
        out_specs=[tok(D_MODEL), tok(ATTN_WIDTH), tok(KV_REP), tok(KV_REP)],
        out_shape=[jax.ShapeDtypeStruct((b, n, D_MODEL), BF16),
                   jax.ShapeDtypeStruct((b, n, ATTN_WIDTH), BF16),
                   jax.ShapeDtypeStruct((b, n, KV_REP), BF16),
                   jax.ShapeDtypeStruct((b, n, KV_REP), BF16)],
        compiler_params=_params(2),
        name="inproj",
    )(x, shift, scale, g, w_in, cos, sa, sb)


CONV_ROWS = 32
SM_ROWS = 16
SCORE_SLOTS = 2
MIX_ROWS = 2048


def _mixout_kernel(*refs, tile_rows, local):
    if local:
        (sink_ref, bgu_ref, bgu_p_ref, bgu_n_ref, q_ref,
         kk_ref, kk_p_ref, kk_n_ref, vv_ref, vv_p_ref, vv_n_ref, kkc_ref, vvc_ref,
         bias_mid_ref, bias_a_ref, bias_b_ref, cw_ref, o_ref, s_slots, e_slots) = refs
    else:
        (sink_ref, bgu_ref, bgu_p_ref, bgu_n_ref, q_ref,
         kkc_ref, vvc_ref, cw_ref, o_ref, s_slots, e_slots) = refs
    i = pl.program_id(0)
    n_tiles = pl.num_programs(0)
    r = tile_rows
    nb = r // BLOCK

    has_prev = (i > 0).astype(F32)
    has_next = (i < n_tiles - 1).astype(F32)
    cw = cw_ref[...]
    cr = CONV_ROWS
    pk = 2 * SUBLANES
    row = lax.broadcasted_iota(jnp.int32, (cr, CONV_WIDTH), 0)

    def conv_chunk(r0):
        bg = bgu_ref[0, r0:r0 + cr, :CONV_WIDTH].astype(F32)
        u = bgu_ref[0, r0:r0 + cr, CONV_WIDTH:].astype(F32)
        if r0 == 0:
            u_prev = bgu_p_ref[0, SUBLANES - 1:SUBLANES, CONV_WIDTH:].astype(F32) * has_prev
        else:
            u_prev = bgu_ref[0, r0 - pk:r0, CONV_WIDTH:].astype(F32)[pk - 1:pk]
        if r0 + cr == r:
            u_next = bgu_n_ref[0, 0:1, CONV_WIDTH:].astype(F32) * has_next
        else:
            u_next = bgu_ref[0, r0 + cr:r0 + cr + pk, CONV_WIDTH:].astype(F32)[0:1]
        u_up = jnp.where(row == 0, u_prev, pltpu.roll(u, 1, 0))
        u_dn = jnp.where(row == cr - 1, u_next, pltpu.roll(u, cr - 1, 0))
        o_ref[0, r0:r0 + cr, :CONV_WIDTH] = (bg * (u_up * cw[0:1] + u * cw[1:2] + u_dn * cw[2:3])).astype(BF16)

    lane = lax.broadcasted_iota(jnp.int32, (1, LANES), 1)
    first_half = lane < HEAD_DIM
    zero = jnp.zeros((), BF16)
    nt_dims = (((1,), (1,)), ((), ()))

    def split_k(t):
        return jnp.where(first_half, t, zero), jnp.where(first_half, zero, t)

    def split_v(t):
        va, vb = split_k(t)
        first = lax.broadcasted_iota(jnp.int32, t.shape, 1) < HEAD_DIM
        ones_a = jnp.where(first, 1.0, 0.0).astype(BF16)
        ones_b = jnp.where(first, 0.0, 1.0).astype(BF16)
        return jnp.concatenate([va, ones_a], axis=1), jnp.concatenate([vb, ones_b], axis=1)

    n_ctx = kkc_ref.shape[1]
    wc = 2 * n_ctx
    nl = 3 * BLOCK
    groups_per_pair = BLOCK // SM_ROWS
    cols_a = [slice(0, n_ctx)] + ([slice(wc, wc + nl)] if local else [])
    cols_b = [slice(n_ctx, wc)] + ([slice(wc + nl, wc + 2 * nl)] if local else [])

    heads = []
    for h in range(N_KV_HEADS):
        hs = slice(h * LANES, (h + 1) * LANES)
        hd = dict(kc_bd=jnp.concatenate(split_k(kkc_ref[0, :, hs]), axis=0),
                  vc_ext=jnp.concatenate(split_v(vvc_ref[0, :, hs]), axis=0),
                  sinks=[sink_ref[4 * h + i] * LOG2E for i in range(4)])
        if local:
            hd["k"] = split_k(jnp.concatenate([kk_p_ref[0, :, hs], kk_ref[0, :, hs], kk_n_ref[0, :, hs]], axis=0))
            hd["v"] = split_v(jnp.concatenate([vv_p_ref[0, :, hs], vv_ref[0, :, hs], vv_n_ref[0, :, hs]], axis=0))
        heads.append(hd)

    blocks = [(h, j) for h in range(N_KV_HEADS) for j in range(nb)]

    def scores(idx):
        h, j = blocks[idx]
        hd = heads[h]
        rs = slice(j * BLOCK, (j + 1) * BLOCK)
        q2 = jnp.concatenate([q_ref[0, rs, 2 * h * LANES:(2 * h + 1) * LANES],
                              q_ref[0, rs, (2 * h + 1) * LANES:(2 * h + 2) * LANES]], axis=0)
        s_ref = s_slots.at[idx % SCORE_SLOTS]
        s_ref[:, :wc] = lax.dot_general(q2, hd["kc_bd"], nt_dims, preferred_element_type=F32)
        if local:
            ws = slice(j * BLOCK, j * BLOCK + nl)
            kl_bd = jnp.concatenate([hd["k"][0][ws], hd["k"][1][ws]], axis=0)
            if nb == 1:
                bias = bias_a_ref[0] + bias_b_ref[0]
            elif j == 0:
                bias = bias_a_ref[0]
            elif j == nb - 1:
                bias = bias_b_ref[0]
            else:
                bias = bias_mid_ref[0]
            s_loc = lax.dot_general(q2, kl_bd, nt_dims, preferred_element_type=F32)
            for c0 in range(0, 2 * nl, BLOCK):
                seg = s_loc[:, c0:c0 + BLOCK]
                if (c0 // BLOCK) % 3 != 1:
                    seg = seg + bias[:, c0:c0 + BLOCK]
                s_ref[:, wc + c0:wc + c0 + BLOCK] = seg

    def softmax_pv(idx):
        h, j = blocks[idx]
        hd = heads[h]
        s_ref = s_slots.at[idx % SCORE_SLOTS]
        e_ref = e_slots.at[idx % SCORE_SLOTS]
        sink_terms = []
        for g in range(2 * BLOCK // SM_ROWS):
            gr = slice(g * SM_ROWS, (g + 1) * SM_ROWS)
            pair = g // groups_per_pair
            terms = []
            for cols, sk in ((cols_a, hd["sinks"][2 * pair]), (cols_b, hd["sinks"][2 * pair + 1])):
                ss = [s_ref[gr, c] for c in cols]
                m = jnp.max(ss[0], axis=-1, keepdims=True)
                for t in ss[1:]:
                    m = jnp.maximum(m, jnp.max(t, axis=-1, keepdims=True))
                m = jnp.maximum(m, sk)
                for c, t in zip(cols, ss):
                    e_ref[gr, c] = jnp.exp2(t - m).astype(BF16)
                terms.append(jnp.exp2(sk - m))
            sink_terms.append(jnp.where(first_half, terms[0], terms[1]))
        o_ext = jnp.dot(e_ref[:, :wc], hd["vc_ext"], preferred_element_type=F32)
        if local:
            ws = slice(j * BLOCK, j * BLOCK + nl)
            vl_ext = jnp.concatenate([hd["v"][0][ws], hd["v"][1][ws]], axis=0)
            o_ext = o_ext + jnp.dot(e_ref[:, wc:], vl_ext, preferred_element_type=F32)
        o = o_ext[:, :LANES] / (o_ext[:, LANES:] + jnp.concatenate(sink_terms, axis=0))
        return jnp.concatenate([o[:BLOCK], o[BLOCK:]], axis=1)

    conv_starts = list(range(0, r, cr))
    per_block = -(-len(conv_starts) // len(blocks))
    outs = {}
    scores(0)
    for idx in range(len(blocks)):
        if idx + 1 < len(blocks):
            scores(idx + 1)
        for r0 in conv_starts[idx * per_block:(idx + 1) * per_block]:
            conv_chunk(r0)
        outs[blocks[idx]] = softmax_pv(idx)
    b_cols = [jnp.concatenate([outs[(h, j)] for j in range(nb)], axis=0) for h in range(N_KV_HEADS)]
    o_ref[0, :, CONV_WIDTH:] = jnp.concatenate(b_cols, axis=1).astype(BF16)


def _band_bias():
    rr = np.arange(2 * BLOCK)[:, None] % BLOCK
    kidx = np.arange(6 * BLOCK)[None, :] % (3 * BLOCK)
    band = (kidx >= rr) & (kidx <= rr + 2 * WINDOW)
    mid = band
    first = band & (kidx >= BLOCK)
    last = band & (kidx < 2 * BLOCK)
    return jnp.asarray(np.where(np.stack([mid, first, last]), 0.0, NEG_BIG), dtype=F32)


def _mixout(bgu, q, kk, vv, kkc, vvc, sink, conv_w, tile_rows, local):
    b, n, _ = bgu.shape
    r = tile_rows
    nt = n // r
    nctx = kkc.shape[1]
    tok = lambda w: pl.BlockSpec((1, r, w), lambda i, bb: (bb, i, 0))
    r8 = r // SUBLANES
    halo8_p = pl.BlockSpec((1, SUBLANES, D_MODEL), lambda i, bb: (bb, jnp.maximum(i * r8 - 1, 0), 0))
    halo8_n = pl.BlockSpec((1, SUBLANES, D_MODEL),
                           lambda i, bb: (bb, jnp.minimum((i + 1) * r8, n // SUBLANES - 1), 0))
    ctx = pl.BlockSpec((1, nctx, KV_REP), lambda i, bb: (bb, 0, 0))
    smem = pl.BlockSpec(memory_space=pltpu.SMEM)
    in_specs = [smem, tok(D_MODEL), halo8_p, halo8_n, tok(ATTN_WIDTH)]
    args = [sink, bgu, bgu, bgu, q]
    if local:
        rb = r // BLOCK
        halo_p = pl.BlockSpec((1, BLOCK, KV_REP), lambda i, bb: (bb, jnp.maximum(i * rb - 1, 0), 0))
        halo_n = pl.BlockSpec((1, BLOCK, KV_REP),
                              lambda i, bb: (bb, jnp.minimum((i + 1) * rb, n // BLOCK - 1), 0))
        bias = _band_bias()
        bshape = (1, 2 * BLOCK, 6 * BLOCK)
        in_specs += [tok(KV_REP), halo_p, halo_n, tok(KV_REP), halo_p, halo_n, ctx, ctx,
                     pl.BlockSpec(bshape, lambda i, bb: (0, 0, 0)),
                     pl.BlockSpec(bshape, lambda i, bb: (jnp.where(i == 0, 1, 0), 0, 0)),
                     pl.BlockSpec(bshape, lambda i, bb: (jnp.where(i == nt - 1, 2, 0), 0, 0))]
        args += [kk, kk, kk, vv, vv, vv, kkc, vvc, bias, bias, bias]
    else:
        in_specs += [ctx, ctx]
        args += [kkc, vvc]
    in_specs += [_const_spec((3, CONV_WIDTH))]
    args += [conv_w]
    score_cols = 2 * nctx + (6 * BLOCK if local else 0)
    return pl.pallas_call(
        functools.partial(_mixout_kernel, tile_rows=r, local=local),
        grid=(nt, b),
        in_specs=in_specs,
        out_specs=tok(D_MODEL),
        out_shape=jax.ShapeDtypeStruct((b, n, D_MODEL), BF16),
        scratch_shapes=[pltpu.VMEM((SCORE_SLOTS, 2 * BLOCK, score_cols), F32),
                        pltpu.VMEM((SCORE_SLOTS, 2 * BLOCK, score_cols), BF16)],
        compiler_params=_params(2),
        name="mixout_local" if local else "mixout_ctx",
    )(*args)


def _chan_dft_kernel(x_ref, sh_ref, sc_ref, g_ref, cs_ref, y1_ref, y2_ref):
    h = _norm_mod(x_ref[0], g_ref[...], sh_ref[0], sc_ref[0]).astype(BF16)
    fg = FOURIER_GROUP
    for gi in range(N_FOURIER_GROUPS):
        yc = jnp.dot(h[:, gi * fg:(gi + 1) * fg], cs_ref[...], preferred_element_type=F32)
        y1_ref[0, :, gi * fg:(gi + 1) * fg] = yc[:, :fg].astype(BF16)
        y2_ref[0, :, gi * fg:(gi + 1) * fg] = yc[:, fg:].astype(BF16)


def _chan_dft(x, shift, scale, g, cs, tm):
    b, n, _ = x.shape
    tok = pl.BlockSpec((1, tm, D_MODEL), lambda i, bb: (bb, i, 0))
    mod = pl.BlockSpec((1, 1, D_MODEL), lambda i, bb: (bb, 0, 0))
    return pl.pallas_call(
        _chan_dft_kernel,
        grid=(n // tm, b),
        in_specs=[tok, mod, mod, _const_spec((1, D_MODEL)), _const_spec((FOURIER_GROUP, 2 * FOURIER_GROUP))],
        out_specs=[tok, tok],
        out_shape=[jax.ShapeDtypeStruct((b, n, D_MODEL), BF16)] * 2,
        compiler_params=_params(2),
        name="chan_dft",
    )(x, shift, scale, g, cs)


def _seq_dft_kernel(cn_ref, sn_ref, y1_ref, y2_ref, o_ref):
    z = jnp.dot(cn_ref[...], y1_ref[0], preferred_element_type=F32)
    z = z + jnp.dot(sn_ref[...], y2_ref[0], preferred_element_type=F32)
    o_ref[0] = z.astype(BF16)


def _seq_dft(cn, nsn, y1, y2, tm):
    b, n, _ = y1.shape
    tok = pl.BlockSpec((1, tm, D_MODEL), lambda bb, i: (bb, i, 0))
    mat = pl.BlockSpec((tm, n), lambda bb, i: (i, 0))
    full = pl.BlockSpec((1, n, D_MODEL), lambda bb, i: (bb, 0, 0), pipeline_mode=pl.Buffered(1))
    return pl.pallas_call(
        _seq_dft_kernel,
        grid=(b, n // tm),
        in_specs=[mat, mat, full, full],
        out_specs=tok,
        out_shape=jax.ShapeDtypeStruct((b, n, D_MODEL), BF16),
        compiler_params=_params(2),
        name="seq_dft",
    )(cn, nsn, y1, y2)


def _dft_mats(n, scale):
    j = np.arange(n)
    ang = ((j[:, None] * j[None, :]) % n) * (2.0 * np.pi / n)
    return (np.cos(ang) * scale).astype(np.float32), (np.sin(ang) * scale).astype(np.float32)


FFT_RADIX = 16
FFT_INNER = 256
FFT_N = FFT_RADIX * FFT_INNER
PERM_TILE = 256
PERM_B = PERM_TILE // FFT_RADIX
CHAN_TILE = 2 * PERM_TILE
PERM_STEP = 4 * CHAN_TILE


def _chan_dft_perm_kernel(x_ref, sh_ref, sc_ref, g_ref, p_ref, cs_ref, y_ref):
    fg = FOURIER_GROUP
    pieces = CHAN_TILE // PERM_TILE
    for t in range(x_ref.shape[1] // CHAN_TILE):
        h = _norm_mod(x_ref[0, t * CHAN_TILE:(t + 1) * CHAN_TILE, :], g_ref[...], sh_ref[0], sc_ref[0]).astype(BF16)
        hp = jnp.concatenate(
            [jnp.dot(p_ref[...], h[k * PERM_TILE:(k + 1) * PERM_TILE], preferred_element_type=F32).astype(BF16)
             for k in range(pieces)], axis=0)
        for gi in range(N_FOURIER_GROUPS):
            gs = slice(gi * fg, (gi + 1) * fg)
            yc = jnp.dot(hp[:, gs], cs_ref[...], preferred_element_type=F32)
            for k in range(pieces):
                for a in range(FFT_RADIX):
                    rs = slice(k * PERM_TILE + a * PERM_B, k * PERM_TILE + (a + 1) * PERM_B)
                    ws = slice((t * pieces + k) * PERM_B, (t * pieces + k + 1) * PERM_B)
                    y_ref[0, gi, a, 0, ws, :] = yc[rs, :fg].astype(BF16)
                    y_ref[0, gi, a, 1, ws, :] = yc[rs, fg:].astype(BF16)


def _chan_dft_perm(x, shift, scale, g, cs):
    b, n, _ = x.shape
    r_out = np.arange(PERM_TILE)
    perm = np.zeros((PERM_TILE, PERM_TILE), np.float32)
    perm[r_out, FFT_RADIX * (r_out % PERM_B) + r_out // PERM_B] = 1.0
    tok = pl.BlockSpec((1, PERM_STEP, D_MODEL), lambda i, bb: (bb, i, 0))
    mod = pl.BlockSpec((1, 1, D_MODEL), lambda i, bb: (bb, 0, 0))
    return pl.pallas_call(
        _chan_dft_perm_kernel,
        grid=(n // PERM_STEP, b),
        in_specs=[tok, mod, mod, _const_spec((1, D_MODEL)), _const_spec((PERM_TILE, PERM_TILE)),
                  _const_spec((FOURIER_GROUP, 2 * FOURIER_GROUP))],
        out_specs=pl.BlockSpec((1, N_FOURIER_GROUPS, FFT_RADIX, 2, PERM_STEP // FFT_RADIX, FOURIER_GROUP),
                               lambda i, bb: (bb, 0, 0, 0, i, 0)),
        out_shape=jax.ShapeDtypeStruct((b, N_FOURIER_GROUPS, FFT_RADIX, 2, n // FFT_RADIX, FOURIER_GROUP), BF16),
        compiler_params=_params(2),
        name="chan_dft_perm",
    )(x, shift, scale, g, jnp.asarray(perm, BF16), cs)


_C1 = float(np.cos(np.pi / 8))
_C2 = float(np.cos(np.pi / 4))
_C3 = float(np.cos(3 * np.pi / 8))


def _dft16_real(ur, ui):
    p = [ur[0]] + [ur[a] + ur[16 - a] for a in range(1, 8)] + [ur[8]]
    q = [p[a] + p[8 - a] for a in range(4)] + [p[4]]
    r = [p[a] - p[8 - a] for a in range(4)]
    s0, s1, s2 = q[0] + q[4], q[1] + q[3], q[2]
    t0, t1 = q[0] - q[4], q[1] - q[3]
    ca = [None] * 9
    ca[0] = s0 + s1 + s2
    ca[4] = s0 - s2
    ca[8] = s0 - s1 + s2
    ca[2] = t0 + _C2 * t1
    ca[6] = t0 - _C2 * t1
    e0, e1 = r[0] + _C2 * r[2], r[0] - _C2 * r[2]
    f, g = _C1 * r[1] + _C3 * r[3], _C3 * r[1] - _C1 * r[3]
    ca[1], ca[7], ca[3], ca[5] = e0 + f, e0 - f, e1 + g, e1 - g
    pp = [None] + [ui[a] - ui[16 - a] for a in range(1, 8)]
    qq = [None] + [pp[a] + pp[8 - a] for a in range(1, 4)] + [pp[4]]
    rr = [None] + [pp[a] - pp[8 - a] for a in range(1, 4)]
    sb = [None] * 8
    e0, e1 = _C2 * qq[2] + qq[4], _C2 * qq[2] - qq[4]
    f, g = _C3 * qq[1] + _C1 * qq[3], _C1 * qq[1] - _C3 * qq[3]
    sb[1], sb[7], sb[3], sb[5] = f + e0, f - e0, g + e1, g - e1
    w = _C2 * (rr[1] + rr[3])
    sb[2], sb[6], sb[4] = w + rr[2], w - rr[2], rr[1] - rr[3]
    out = [None] * 16
    out[0], out[8] = ca[0], ca[8]
    for c in range(1, 8):
        out[c] = ca[c] + sb[c]
        out[16 - c] = ca[c] - sb[c]
    return out


def _seq_fft_kernel(y_ref, l_ref, o_ref, u_even_ref, u_odd_ref):
    t = pl.program_id(0)

    @pl.when(t == 0)
    def _():
        u_odd_ref[...] = jnp.zeros(u_odd_ref.shape, F32)

    def step(cur_ref, prev_ref):
        rt = 2 * SUBLANES
        for i in range(FFT_INNER // rt):
            r0 = i * rt
            for lh in range(FOURIER_GROUP // LANES):
                ls = slice(lh * LANES, (lh + 1) * LANES)
                ur = [prev_ref[a, r0:r0 + rt, ls] for a in range(FFT_RADIX)]
                ui = [prev_ref[a, FFT_INNER + r0:FFT_INNER + r0 + rt, ls] for a in range(FFT_RADIX)]
                out = _dft16_real(ur, ui)
                for c in range(FFT_RADIX):
                    o_ref[0, c * FFT_INNER + r0:c * FFT_INNER + r0 + rt, ls] = out[c].astype(BF16)
        for a in range(FFT_RADIX):
            u = jnp.dot(l_ref[a, :, :FFT_INNER], y_ref[a, 0], preferred_element_type=F32)
            cur_ref[a] = u + jnp.dot(l_ref[a, :, FFT_INNER:], y_ref[a, 1], preferred_element_type=F32)

    pl.when(t % 2 == 0)(lambda: step(u_even_ref, u_odd_ref))
    pl.when(t % 2 == 1)(lambda: step(u_odd_ref, u_even_ref))


def _seq_fft(ycat, lmat):
    b = ycat.shape[0]
    ng = N_FOURIER_GROUPS
    items = b * ng

    def y_map(t):
        tt = jnp.minimum(t, items - 1)
        return (tt // ng, tt % ng, 0, 0, 0, 0)

    def o_map(t):
        tp = jnp.maximum(t - 1, 0)
        return (tp // ng, 0, tp % ng)

    return pl.pallas_call(
        _seq_fft_kernel,
        grid=(items + 1,),
        in_specs=[pl.BlockSpec((None, None, FFT_RADIX, 2, FFT_INNER, FOURIER_GROUP), y_map),
                  _const_spec((FFT_RADIX, 2 * FFT_INNER, 2 * FFT_INNER))],
        out_specs=pl.BlockSpec((1, FFT_N, FOURIER_GROUP), o_map),
        out_shape=jax.ShapeDtypeStruct((b, FFT_N, D_MODEL), BF16),
        scratch_shapes=[pltpu.VMEM((FFT_RADIX, 2 * FFT_INNER, FOURIER_GROUP), F32)] * 2,
        compiler_params=_params(1),
        name="seq_fft",
    )(ycat, lmat)


def _fft_stage_mats():
    a = np.arange(FFT_RADIX)[:, None, None]
    d = np.arange(FFT_INNER)[None, :, None]
    bb = np.arange(FFT_INNER)[None, None, :]
    ang = ((d * (a + FFT_RADIX * bb)) % FFT_N) * (2.0 * np.pi / FFT_N)
    c = np.cos(ang) * (FFT_N ** -0.5)
    s = np.sin(ang) * (FFT_N ** -0.5)
    top = np.concatenate([c, -s], axis=2)
    bot = np.concatenate([-s, -c], axis=2)
    return jnp.asarray(np.concatenate([top, bot], axis=1).astype(np.float32), BF16)


FF_CHUNK = 256
TAIL_ROWS = 1024


def _tail_kernel(*refs, final):
    if final:
        (x_ref, m_ref, g1_ref, wm_ref, sh_ref, sc_ref, gate_ref, g_ref, wg_ref, wu_ref, wd_ref,
         fg_ref, o_ref, acc_ref) = refs
    else:
        (x_ref, m_ref, g1_ref, wm_ref, sh_ref, sc_ref, gate_ref, g_ref, wg_ref, wu_ref, wd_ref,
         o_ref, acc_ref) = refs
    x = x_ref[0] + g1_ref[0] * jnp.dot(m_ref[0], wm_ref[...], preferred_element_type=F32)
    h = _norm_mod(x, g_ref[...], sh_ref[0], sc_ref[0]).astype(BF16)
    for c in range(D_FF // FF_CHUNK):
        cs = slice(c * FF_CHUNK, (c + 1) * FF_CHUNK)
        gt = jnp.dot(h, wg_ref[:, cs], preferred_element_type=F32)
        up = jnp.dot(h, wu_ref[:, cs], preferred_element_type=F32)
        acc_ref[:, cs] = (_silu(gt) * up).astype(BF16)
    y = x + gate_ref[0] * jnp.dot(acc_ref[...], wd_ref[...], preferred_element_type=F32)
    if final:
        ms = jnp.mean(y * y, axis=-1, keepdims=True)
        y = (y * lax.rsqrt(ms + EPS)) * fg_ref[...]
    o_ref[0] = y


def _tail(x, m, gate1, wm, wm_layer, shift, scale, gate, g, wg, wu, wd, layer, tm, final_g=None):
    b, n, _ = x.shape
    tok = pl.BlockSpec((1, tm, D_MODEL), lambda i, bb: (bb, i, 0))
    mod = pl.BlockSpec((1, 1, D_MODEL), lambda i, bb: (bb, 0, 0))
    in_specs = [tok, tok, mod, _layer_spec((D_MODEL, D_MODEL), wm_layer), mod, mod, mod, _const_spec((1, D_MODEL)),
                _layer_spec((D_MODEL, D_FF), layer), _layer_spec((D_MODEL, D_FF), layer),
                _layer_spec((D_FF, D_MODEL), layer)]
    args = [x, m, gate1, wm, shift, scale, gate, g, wg, wu, wd]
    if final_g is not None:
        in_specs.append(_const_spec((1, D_MODEL)))
        args.append(final_g)
    return pl.pallas_call(
        functools.partial(_tail_kernel, final=final_g is not None),
        grid=(n // tm, b),
        in_specs=in_specs,
        out_specs=tok,
        out_shape=jax.ShapeDtypeStruct((b, n, D_MODEL), F32),
        scratch_shapes=[pltpu.VMEM((tm, D_FF), BF16)],
        compiler_params=_params(2),
        name="tail",
    )(*args)


def _rope_lane_tables(n):
    rows = n // GRID_W
    r = np.repeat(np.arange(rows), GRID_W).astype(np.float64)
    col = np.tile(np.arange(GRID_W), rows).astype(np.float64)
    quarter = HEAD_DIM // 4
    inv = ROPE_THETA ** (-np.arange(quarter, dtype=np.float64) / quarter)
    ang_r = r[:, None] * inv
    ang_c = col[:, None] * inv
    cr, sr, cc, sc = np.cos(ang_r), np.sin(ang_r), np.cos(ang_c), np.sin(ang_c)
    z = np.zeros_like(sr)
    cos = np.concatenate([cr, cr, cc, cc], axis=1)
    sa = np.concatenate([-sr, z, -sc, z], axis=1)
    sb = np.concatenate([z, sr, z, sc], axis=1)
    rep = lambda t: jnp.asarray(np.concatenate([t, t], axis=1).astype(np.float32))
    return rep(cos), rep(sa), rep(sb)


def _identity_lane_tables(n):
    one, zero = np.ones((n, LANES), np.float32), np.zeros((n, LANES), np.float32)
    return jnp.asarray(one), jnp.asarray(zero), jnp.asarray(zero)


def kernel(x, c, ctx, c_ctx, w_ada, b_ada, norm1_g, norm2_g, w_in, conv_w, sink,
           w_mix_out, w_fourier_out, w_ffn_gate, w_ffn_up, w_ffn_down, final_g):
    b, n, d = x.shape
    L = ctx.shape[1]
    nc = b * L

    cond = jnp.concatenate([c, c_ctx[None, :], jnp.zeros((COND_ROWS - b - 1, d), F32)], axis=0)
    mods = _adaln(cond, w_ada, b_ada)

    def split(l, lo, hi):
        m = mods[l, lo:hi]
        return [m[:, None, k * d:(k + 1) * d] for k in range(6)]

    rope_lat = _rope_lane_tables(n)
    rope_ctx = _identity_lane_tables(nc)
    w_in_b, w_mix_b, w_fo_b = w_in.astype(BF16), w_mix_out.astype(BF16), w_fourier_out.astype(BF16)
    wg_b, wu_b, wd_b = w_ffn_gate.astype(BF16), w_ffn_up.astype(BF16), w_ffn_down.astype(BF16)
    cc, sc_ = _dft_mats(FOURIER_GROUP, FOURIER_GROUP ** -0.5)
    cs_chan = jnp.asarray(np.concatenate([cc, sc_], axis=1), BF16)
    seq_mats = {}
    for m in {n, L} - {FFT_N}:
        cm, sm = _dft_mats(m, m ** -0.5)
        seq_mats[m] = (jnp.asarray(cm, BF16), jnp.asarray(-sm, BF16))
    fft_l = _fft_stage_mats() if n == FFT_N else None

    xc = ctx.reshape(1, nc, d)
    per_seq = lambda t: t.reshape(b, L, t.shape[-1])
    for l in range(DEPTH):
        ctx_after = any(j % 2 == 0 for j in range(l + 1, DEPTH))
        ctx_here = (l % 2 == 0) or ctx_after
        sh1, sc1, g1, sh2, sc2, g2 = split(l, 0, b)
        if ctx_here:
            csh1, csc1, cg1, csh2, csc2, cg2 = split(l, b, b + 1)
        n1 = norm1_g[l][None, :]
        n2 = norm2_g[l][None, :]
        if l % 2 == 0:
            e = l // 2
            wm, wm_layer = w_mix_b, e
            bgu_c, q_c, kk_c, vv_c = _inproj(xc, csh1, csc1, n1, w_in_b, e, *rope_ctx, tm=min(nc, INPROJ_ROWS))
            kk_c, vv_c = per_seq(kk_c), per_seq(vv_c)
            bgu, q, kk, vv = _inproj(x, sh1, sc1, n1, w_in_b, e, *rope_lat, tm=min(n, INPROJ_ROWS))
            mix = _mixout(bgu, q, kk, vv, kk_c, vv_c, sink[e], conv_w[e], tile_rows=min(n, MIX_ROWS), local=True)
            if ctx_after:
                mix_c = _mixout(per_seq(bgu_c), per_seq(q_c), None, None, kk_c, vv_c, sink[e], conv_w[e],
                                tile_rows=L, local=False)
        else:
            wm, wm_layer = w_fo_b, l // 2
            if n == FFT_N:
                mix = _seq_fft(_chan_dft_perm(x, sh1, sc1, n1, cs_chan), fft_l)
            else:
                mix = _seq_dft(*seq_mats[n], *_chan_dft(x, sh1, sc1, n1, cs_chan, tm=min(n, CHAN_DFT_ROWS)),
                               tm=min(n, SEQ_DFT_ROWS))
            if ctx_after:
                y1c, y2c = _chan_dft(xc, csh1, csc1, n1, cs_chan, tm=min(nc, CHAN_DFT_ROWS))
                mix_c = _seq_dft(*seq_mats[L], per_seq(y1c), per_seq(y2c), tm=L)
        x = _tail(x, mix, g1, wm, wm_layer, sh2, sc2, g2, n2, wg_b, wu_b, wd_b, l, tm=min(n, TAIL_ROWS),
                  final_g=final_g[None, :] if l == DEPTH - 1 else None)
        if ctx_after:
            xc = _tail(xc, mix_c.reshape(1, nc, d), cg1, wm, wm_layer, csh2, csc2, cg2, n2, wg_b, wu_b, wd_b, l,
                       tm=min(nc, TAIL_ROWS))
    return x
```

```python
import functools

import numpy as np
import jax
import jax.numpy as jnp
from jax import lax
from jax.experimental import pallas as pl
from jax.experimental.pallas import tpu as pltpu

F32 = jnp.float32
BF16 = jnp.bfloat16

D_MODEL = 1024
DEPTH = 4
GRID_W = 64
HEAD_DIM = 64
N_Q_HEADS = 8
N_KV_HEADS = 2
ATTN_WIDTH = N_Q_HEADS * HEAD_DIM
KV_WIDTH = N_KV_HEADS * HEAD_DIM
CONV_WIDTH = D_MODEL - ATTN_WIDTH
WINDOW = 128
BLOCK = 128
ROPE_THETA = 10000.0
N_FOURIER_GROUPS = 4
FOURIER_GROUP = D_MODEL // N_FOURIER_GROUPS
D_FF = 2816
EPS = 1e-6
Q_START = 3 * CONV_WIDTH
K_START = Q_START + ATTN_WIDTH
V_START = K_START + KV_WIDTH

LANES = 128
SUBLANES = 8
KV_REP = 2 * KV_WIDTH
NEG_BIG = -1e30
COND_ROWS = 24
VMEM_LIMIT = 56 * 1024 * 1024
ADALN_COLS = 1024
CHAN_DFT_ROWS = 512
SEQ_DFT_ROWS = 256


def _params(n_axes):
    return pltpu.CompilerParams(dimension_semantics=("arbitrary",) * n_axes,
                                vmem_limit_bytes=VMEM_LIMIT)


def _const_spec(shape):
    nd = len(shape)
    return pl.BlockSpec(shape, lambda *_: (0,) * nd, pipeline_mode=pl.Buffered(1))


def _layer_spec(shape, layer):
    nd = len(shape)
    return pl.BlockSpec((None,) + tuple(shape), lambda *_: (layer,) + (0,) * nd, pipeline_mode=pl.Buffered(1))


def _norm_mod(x, g, shift, scale):
    ms = jnp.mean(x * x, axis=-1, keepdims=True)
    y = x * lax.rsqrt(ms + EPS)
    return (y * g) * (1.0 + scale) + shift


def _silu(x):
    return x * (1.0 / (1.0 + jnp.exp(-x)))


def _adaln_kernel(c_ref, w_ref, b_ref, o_ref):
    a = _silu(c_ref[...])
    w = w_ref[0]
    a_hi = a.astype(BF16)
    a_lo = (a - a_hi.astype(F32)).astype(BF16)
    w_hi = w.astype(BF16)
    w_lo = (w - w_hi.astype(F32)).astype(BF16)
    acc = jnp.dot(a_hi, w_hi, preferred_element_type=F32)
    acc += jnp.dot(a_lo, w_hi, preferred_element_type=F32)
    acc += jnp.dot(a_hi, w_lo, preferred_element_type=F32)
    o_ref[0] = acc + b_ref[0]


def _adaln(cond, w_ada, b_ada):
    tn = ADALN_COLS
    nt = (6 * D_MODEL) // tn
    return pl.pallas_call(
        _adaln_kernel,
        grid=(DEPTH, nt),
        in_specs=[
            pl.BlockSpec((COND_ROWS, D_MODEL), lambda l, j: (0, 0)),
            pl.BlockSpec((1, D_MODEL, tn), lambda l, j: (l, 0, j)),
            pl.BlockSpec((1, 1, tn), lambda l, j: (l, 0, j)),
        ],
        out_specs=pl.BlockSpec((1, COND_ROWS, tn), lambda l, j: (l, 0, j)),
        out_shape=jax.ShapeDtypeStruct((DEPTH, COND_ROWS, 6 * D_MODEL), F32),
        compiler_params=_params(2),
        name="adaln",
    )(cond, w_ada, b_ada.reshape(DEPTH, 1, 6 * D_MODEL))


IN_WIDTH = V_START + KV_WIDTH
LOG2E = float(np.log2(np.e))
Q_SCALE = HEAD_DIM ** -0.5 * LOG2E
INPROJ_ROWS = 2048
INPROJ_SUB = 512


def _inproj_kernel(x_ref, sh_ref, sc_ref, g_ref, w_ref, cos_ref, sa_ref, sb_ref,
                   bgu_ref, q_ref, kk_ref, vv_ref):
    first_half = lax.broadcasted_iota(jnp.int32, (1, LANES), 1) < HEAD_DIM
    tm = x_ref.shape[1]
    sub = min(tm, INPROJ_SUB)
    for s0 in range(0, tm, sub):
        rs = slice(s0, s0 + sub)
        h = _norm_mod(x_ref[0, rs, :], g_ref[...], sh_ref[0], sc_ref[0]).astype(BF16)
        p = jnp.dot(h, w_ref[...], preferred_element_type=F32)
        bgu_ref[0, rs, :CONV_WIDTH] = p[:, :CONV_WIDTH].astype(BF16)
        bgu_ref[0, rs, CONV_WIDTH:] = (p[:, CONV_WIDTH:2 * CONV_WIDTH]
                                       * p[:, 2 * CONV_WIDTH:Q_START]).astype(BF16)
        cos = cos_ref[rs, :]
        sa = sa_ref[rs, :]
        sb = sb_ref[rs, :]

        def rope(t):
            return t * cos + pltpu.roll(t, LANES - 16, 1) * sa + pltpu.roll(t, 16, 1) * sb

        for j in range(ATTN_WIDTH // LANES):
            lo = Q_START + j * LANES
            q_ref[0, rs, j * LANES:(j + 1) * LANES] = (rope(p[:, lo:lo + LANES]) * Q_SCALE).astype(BF16)
        for t, ref in ((rope(p[:, K_START:V_START]), kk_ref), (p[:, V_START:], vv_ref)):
            swapped = pltpu.roll(t, HEAD_DIM, 1)
            ref[0, rs, :LANES] = jnp.where(first_half, t, swapped).astype(BF16)
            ref[0, rs, LANES:] = jnp.where(first_half, swapped, t).astype(BF16)


def _inproj(x, shift, scale, g, w_in, layer, cos, sa, sb, tm):
    b, n, _ = x.shape
    nt = n // tm
    tok = lambda w: pl.BlockSpec((1, tm, w), lambda i, bb: (bb, i, 0))
    mod = pl.BlockSpec((1, 1, D_MODEL), lambda i, bb: (bb, 0, 0))
    tab = pl.BlockSpec((tm, LANES), lambda i, bb: (i, 0))
    return pl.pallas_call(
        _inproj_kernel,
        grid=(nt, b),
        in_specs=[tok(D_MODEL), mod, mod, _const_spec((1, D_MODEL)),
                  _layer_spec((D_MODEL, IN_WIDTH), layer), tab, tab, tab],
        out_specs=[tok(D_MODEL), tok(ATTN_WIDTH), tok(KV_REP), tok(KV_REP)],
        out_shape=[jax.ShapeDtypeStruct((b, n, D_MODEL), BF16),
                   jax.ShapeDtypeStruct((b, n, ATTN_WIDTH), BF16),
                   jax.ShapeDtypeStruct((b, n, KV_REP), BF16),
                   jax.ShapeDtypeStruct((b, n, KV_REP), BF16)],
        compiler_params=_params(2),
        name="inproj",
    )(x, shift, scale, g, w_in, cos, sa, sb)


CONV_ROWS = 32
SM_ROWS = 16
SCORE_SLOTS = 2
MIX_ROWS = 2048


def _mixout_kernel(*refs, tile_rows, local):
    if local:
        (sink_ref, bgu_ref, bgu_p_ref, bgu_n_ref, q_ref,
         kk_ref, kk_p_ref, kk_n_ref, vv_ref, vv_p_ref, vv_n_ref, kkc_ref, vvc_ref,
         bias_mid_ref, bias_a_ref, bias_b_ref, cw_ref, o_ref, s_slots, e_slots) = refs
    else:
        (sink_ref, bgu_ref, bgu_p_ref, bgu_n_ref, q_ref,
         kkc_ref, vvc_ref, cw_ref, o_ref, s_slots, e_slots) = refs
    i = pl.program_id(0)
    n_tiles = pl.num_programs(0)
    r = tile_rows
    nb = r // BLOCK

    has_prev = (i > 0).astype(F32)
    has_next = (i < n_tiles - 1).astype(F32)
    cw = cw_ref[...]
    cr = CONV_ROWS
    pk = 2 * SUBLANES
    row = lax.broadcasted_iota(jnp.int32, (cr, CONV_WIDTH), 0)

    def conv_chunk(r0):
        bg = bgu_ref[0, r0:r0 + cr, :CONV_WIDTH].astype(F32)
        u = bgu_ref[0, r0:r0 + cr, CONV_WIDTH:].astype(F32)
        if r0 == 0:
            u_prev = bgu_p_ref[0, SUBLANES - 1:SUBLANES, CONV_WIDTH:].astype(F32) * has_prev
        else:
            u_prev = bgu_ref[0, r0 - pk:r0, CONV_WIDTH:].astype(F32)[pk - 1:pk]
        if r0 + cr == r:
            u_next = bgu_n_ref[0, 0:1, CONV_WIDTH:].astype(F32) * has_next
        else:
            u_next = bgu_ref[0, r0 + cr:r0 + cr + pk, CONV_WIDTH:].astype(F32)[0:1]
        u_up = jnp.where(row == 0, u_prev, pltpu.roll(u, 1, 0))
        u_dn = jnp.where(row == cr - 1, u_next, pltpu.roll(u, cr - 1, 0))
        o_ref[0, r0:r0 + cr, :CONV_WIDTH] = (bg * (u_up * cw[0:1] + u * cw[1:2] + u_dn * cw[2:3])).astype(BF16)

    lane = lax.broadcasted_iota(jnp.int32, (1, LANES), 1)
    first_half = lane < HEAD_DIM
    zero = jnp.zeros((), BF16)
    nt_dims = (((1,), (1,)), ((), ()))

    def split_k(t):
        return jnp.where(first_half, t, zero), jnp.where(first_half, zero, t)

    def split_v(t):
        va, vb = split_k(t)
        first = lax.broadcasted_iota(jnp.int32, t.shape, 1) < HEAD_DIM
        ones_a = jnp.where(first, 1.0, 0.0).astype(BF16)
        ones_b = jnp.where(first, 0.0, 1.0).astype(BF16)
        return jnp.concatenate([va, ones_a], axis=1), jnp.concatenate([vb, ones_b], axis=1)

    n_ctx = kkc_ref.shape[1]
    wc = 2 * n_ctx
    nl = 3 * BLOCK
    groups_per_pair = BLOCK // SM_ROWS
    cols_a = [slice(0, n_ctx)] + ([slice(wc, wc + nl)] if local else [])
    cols_b = [slice(n_ctx, wc)] + ([slice(wc + nl, wc + 2 * nl)] if local else [])

    heads = []
    for h in range(N_KV_HEADS):
        hs = slice(h * LANES, (h + 1) * LANES)
        hd = dict(kc_bd=jnp.concatenate(split_k(kkc_ref[0, :, hs]), axis=0),
                  vc_ext=jnp.concatenate(split_v(vvc_ref[0, :, hs]), axis=0),
                  sinks=[sink_ref[4 * h + i] * LOG2E for i in range(4)])
        if local:
            hd["k"] = split_k(jnp.concatenate([kk_p_ref[0, :, hs], kk_ref[0, :, hs], kk_n_ref[0, :, hs]], axis=0))
            hd["v"] = split_v(jnp.concatenate([vv_p_ref[0, :, hs], vv_ref[0, :, hs], vv_n_ref[0, :, hs]], axis=0))
        heads.append(hd)

    blocks = [(h, j) for h in range(N_KV_HEADS) for j in range(nb)]

    def scores(idx):
        h, j = blocks[idx]
        hd = heads[h]
        rs = slice(j * BLOCK, (j + 1) * BLOCK)
        q2 = jnp.concatenate([q_ref[0, rs, 2 * h * LANES:(2 * h + 1) * LANES],
                              q_ref[0, rs, (2 * h + 1) * LANES:(2 * h + 2) * LANES]], axis=0)
        s_ref = s_slots.at[idx % SCORE_SLOTS]
        s_ref[:, :wc] = lax.dot_general(q2, hd["kc_bd"], nt_dims, preferred_element_type=F32)
        if local:
            ws = slice(j * BLOCK, j * BLOCK + nl)
            kl_bd = jnp.concatenate([hd["k"][0][ws], hd["k"][1][ws]], axis=0)
            if nb == 1:
                bias = bias_a_ref[0] + bias_b_ref[0]
            elif j == 0:
                bias = bias_a_ref[0]
            elif j == nb - 1:
                bias = bias_b_ref[0]
            else:
                bias = bias_mid_ref[0]
            s_loc = lax.dot_general(q2, kl_bd, nt_dims, preferred_element_type=F32)
            for c0 in range(0, 2 * nl, BLOCK):
                seg = s_loc[:, c0:c0 + BLOCK]
                if (c0 // BLOCK) % 3 != 1:
                    seg = seg + bias[:, c0:c0 + BLOCK]
                s_ref[:, wc + c0:wc + c0 + BLOCK] = seg

    def softmax_pv(idx):
        h, j = blocks[idx]
        hd = heads[h]
        s_ref = s_slots.at[idx % SCORE_SLOTS]
        e_ref = e_slots.at[idx % SCORE_SLOTS]
        sink_terms = []
        for g in range(2 * BLOCK // SM_ROWS):
            gr = slice(g * SM_ROWS, (g + 1) * SM_ROWS)
            pair = g // groups_per_pair
            terms = []
            for cols, sk in ((cols_a, hd["sinks"][2 * pair]), (cols_b, hd["sinks"][2 * pair + 1])):
                ss = [s_ref[gr, c] for c in cols]
                m = jnp.max(ss[0], axis=-1, keepdims=True)
                for t in ss[1:]:
                    m = jnp.maximum(m, jnp.max(t, axis=-1, keepdims=True))
                m = jnp.maximum(m, sk)
                for c, t in zip(cols, ss):
                    e_ref[gr, c] = jnp.exp2(t - m).astype(BF16)
                terms.append(jnp.exp2(sk - m))
            sink_terms.append(jnp.where(first_half, terms[0], terms[1]))
        o_ext = jnp.dot(e_ref[:, :wc], hd["vc_ext"], preferred_element_type=F32)
        if local:
            ws = slice(j * BLOCK, j * BLOCK + nl)
            vl_ext = jnp.concatenate([hd["v"][0][ws], hd["v"][1][ws]], axis=0)
            o_ext = o_ext + jnp.dot(e_ref[:, wc:], vl_ext, preferred_element_type=F32)
        o = o_ext[:, :LANES] / (o_ext[:, LANES:] + jnp.concatenate(sink_terms, axis=0))
        return jnp.concatenate([o[:BLOCK], o[BLOCK:]], axis=1)

    conv_starts = list(range(0, r, cr))
    per_block = -(-len(conv_starts) // len(blocks))
    outs = {}
    scores(0)
    for idx in range(len(blocks)):
        if idx + 1 < len(blocks):
            scores(idx + 1)
        for r0 in conv_starts[idx * per_block:(idx + 1) * per_block]:
            conv_chunk(r0)
        outs[blocks[idx]] = softmax_pv(idx)
    b_cols = [jnp.concatenate([outs[(h, j)] for j in range(nb)], axis=0) for h in range(N_KV_HEADS)]
    o_ref[0, :, CONV_WIDTH:] = jnp.concatenate(b_cols, axis=1).astype(BF16)


def _band_bias():
    rr = np.arange(2 * BLOCK)[:, None] % BLOCK
    kidx = np.arange(6 * BLOCK)[None, :] % (3 * BLOCK)
    band = (kidx >= rr) & (kidx <= rr + 2 * WINDOW)
    mid = band
    first = band & (kidx >= BLOCK)
    last = band & (kidx < 2 * BLOCK)
    return jnp.asarray(np.where(np.stack([mid, first, last]), 0.0, NEG_BIG), dtype=F32)


def _mixout(bgu, q, kk, vv, kkc, vvc, sink, conv_w, tile_rows, local):
    b, n, _ = bgu.shape
    r = tile_rows
    nt = n // r
    nctx = kkc.shape[1]
    tok = lambda w: pl.BlockSpec((1, r, w), lambda i, bb: (bb, i, 0))
    r8 = r // SUBLANES
    halo8_p = pl.BlockSpec((1, SUBLANES, D_MODEL), lambda i, bb: (bb, jnp.maximum(i * r8 - 1, 0), 0))
    halo8_n = pl.BlockSpec((1, SUBLANES, D_MODEL),
                           lambda i, bb: (bb, jnp.minimum((i + 1) * r8, n // SUBLANES - 1), 0))
    ctx = pl.BlockSpec((1, nctx, KV_REP), lambda i, bb: (bb, 0, 0))
    smem = pl.BlockSpec(memory_space=pltpu.SMEM)
    in_specs = [smem, tok(D_MODEL), halo8_p, halo8_n, tok(ATTN_WIDTH)]
    args = [sink, bgu, bgu, bgu, q]
    if local:
        rb = r // BLOCK
        halo_p = pl.BlockSpec((1, BLOCK, KV_REP), lambda i, bb: (bb, jnp.maximum(i * rb - 1, 0), 0))
        halo_n = pl.BlockSpec((1, BLOCK, KV_REP),
                              lambda i, bb: (bb, jnp.minimum((i + 1) * rb, n // BLOCK - 1), 0))
        bias = _band_bias()
        bshape = (1, 2 * BLOCK, 6 * BLOCK)
        in_specs += [tok(KV_REP), halo_p, halo_n, tok(KV_REP), halo_p, halo_n, ctx, ctx,
                     pl.BlockSpec(bshape, lambda i, bb: (0, 0, 0)),
                     pl.BlockSpec(bshape, lambda i, bb: (jnp.where(i == 0, 1, 0), 0, 0)),
                     pl.BlockSpec(bshape, lambda i, bb: (jnp.where(i == nt - 1, 2, 0), 0, 0))]
        args += [kk, kk, kk, vv, vv, vv, kkc, vvc, bias, bias, bias]
    else:
        in_specs += [ctx, ctx]
        args += [kkc, vvc]
    in_specs += [_const_spec((3, CONV_WIDTH))]
    args += [conv_w]
    score_cols = 2 * nctx + (6 * BLOCK if local else 0)
    return pl.pallas_call(
        functools.partial(_mixout_kernel, tile_rows=r, local=local),
        grid=(nt, b),
        in_specs=in_specs,
        out_specs=tok(D_MODEL),
        out_shape=jax.ShapeDtypeStruct((b, n, D_MODEL), BF16),
        scratch_shapes=[pltpu.VMEM((SCORE_SLOTS, 2 * BLOCK, score_cols), F32),
                        pltpu.VMEM((SCORE_SLOTS, 2 * BLOCK, score_cols), BF16)],
        compiler_params=_params(2),
        name="mixout_local" if local else "mixout_ctx",
    )(*args)


def _chan_dft_kernel(x_ref, sh_ref, sc_ref, g_ref, cs_ref, y1_ref, y2_ref):
    h = _norm_mod(x_ref[0], g_ref[...], sh_ref[0], sc_ref[0]).astype(BF16)
    fg = FOURIER_GROUP
    for gi in range(N_FOURIER_GROUPS):
        yc = jnp.dot(h[:, gi * fg:(gi + 1) * fg], cs_ref[...], preferred_element_type=F32)
        y1_ref[0, :, gi * fg:(gi + 1) * fg] = yc[:, :fg].astype(BF16)
        y2_ref[0, :, gi * fg:(gi + 1) * fg] = yc[:, fg:].astype(BF16)


def _chan_dft(x, shift, scale, g, cs, tm):
    b, n, _ = x.shape
    tok = pl.BlockSpec((1, tm, D_MODEL), lambda i, bb: (bb, i, 0))
    mod = pl.BlockSpec((1, 1, D_MODEL), lambda i, bb: (bb, 0, 0))
    return pl.pallas_call(
        _chan_dft_kernel,
        grid=(n // tm, b),
        in_specs=[tok, mod, mod, _const_spec((1, D_MODEL)), _const_spec((FOURIER_GROUP, 2 * FOURIER_GROUP))],
        out_specs=[tok, tok],
        out_shape=[jax.ShapeDtypeStruct((b, n, D_MODEL), BF16)] * 2,
        compiler_params=_params(2),
        name="chan_dft",
    )(x, shift, scale, g, cs)


def _seq_dft_kernel(cn_ref, sn_ref, y1_ref, y2_ref, o_ref):
    z = jnp.dot(cn_ref[...], y1_ref[0], preferred_element_type=F32)
    z = z + jnp.dot(sn_ref[...], y2_ref[0], preferred_element_type=F32)
    o_ref[0] = z.astype(BF16)


def _seq_dft(cn, nsn, y1, y2, tm):
    b, n, _ = y1.shape
    tok = pl.BlockSpec((1, tm, D_MODEL), lambda bb, i: (bb, i, 0))
    mat = pl.BlockSpec((tm, n), lambda bb, i: (i, 0))
    full = pl.BlockSpec((1, n, D_MODEL), lambda bb, i: (bb, 0, 0), pipeline_mode=pl.Buffered(1))
    return pl.pallas_call(
        _seq_dft_kernel,
        grid=(b, n // tm),
        in_specs=[mat, mat, full, full],
        out_specs=tok,
        out_shape=jax.ShapeDtypeStruct((b, n, D_MODEL), BF16),
        compiler_params=_params(2),
        name="seq_dft",
    )(cn, nsn, y1, y2)


def _dft_mats(n, scale):
    j = np.arange(n)
    ang = ((j[:, None] * j[None, :]) % n) * (2.0 * np.pi / n)
    return (np.cos(ang) * scale).astype(np.float32), (np.sin(ang) * scale).astype(np.float32)


FFT_RADIX = 16
FFT_INNER = 256
FFT_N = FFT_RADIX * FFT_INNER
PERM_TILE = 256
PERM_B = PERM_TILE // FFT_RADIX
CHAN_TILE = 2 * PERM_TILE
PERM_STEP = 4 * CHAN_TILE


def _chan_dft_perm_kernel(x_ref, sh_ref, sc_ref, g_ref, p_ref, cs_ref, y_ref):
    fg = FOURIER_GROUP
    pieces = CHAN_TILE // PERM_TILE
    for t in range(x_ref.shape[1] // CHAN_TILE):
        h = _norm_mod(x_ref[0, t * CHAN_TILE:(t + 1) * CHAN_TILE, :], g_ref[...], sh_ref[0], sc_ref[0]).astype(BF16)
        hp = jnp.concatenate(
            [jnp.dot(p_ref[...], h[k * PERM_TILE:(k + 1) * PERM_TILE], preferred_element_type=F32).astype(BF16)
             for k in range(pieces)], axis=0)
        for gi in range(N_FOURIER_GROUPS):
            gs = slice(gi * fg, (gi + 1) * fg)
            yc = jnp.dot(hp[:, gs], cs_ref[...], preferred_element_type=F32)
            for k in range(pieces):
                for a in range(FFT_RADIX):
                    rs = slice(k * PERM_TILE + a * PERM_B, k * PERM_TILE + (a + 1) * PERM_B)
                    ws = slice((t * pieces + k) * PERM_B, (t * pieces + k + 1) * PERM_B)
                    y_ref[0, gi, a, 0, ws, :] = yc[rs, :fg].astype(BF16)
                    y_ref[0, gi, a, 1, ws, :] = yc[rs, fg:].astype(BF16)


def _chan_dft_perm(x, shift, scale, g, cs):
    b, n, _ = x.shape
    r_out = np.arange(PERM_TILE)
    perm = np.zeros((PERM_TILE, PERM_TILE), np.float32)
    perm[r_out, FFT_RADIX * (r_out % PERM_B) + r_out // PERM_B] = 1.0
    tok = pl.BlockSpec((1, PERM_STEP, D_MODEL), lambda i, bb: (bb, i, 0))
    mod = pl.BlockSpec((1, 1, D_MODEL), lambda i, bb: (bb, 0, 0))
    return pl.pallas_call(
        _chan_dft_perm_kernel,
        grid=(n // PERM_STEP, b),
        in_specs=[tok, mod, mod, _const_spec((1, D_MODEL)), _const_spec((PERM_TILE, PERM_TILE)),
                  _const_spec((FOURIER_GROUP, 2 * FOURIER_GROUP))],
        out_specs=pl.BlockSpec((1, N_FOURIER_GROUPS, FFT_RADIX, 2, PERM_STEP // FFT_RADIX, FOURIER_GROUP),
                               lambda i, bb: (bb, 0, 0, 0, i, 0)),
        out_shape=jax.ShapeDtypeStruct((b, N_FOURIER_GROUPS, FFT_RADIX, 2, n // FFT_RADIX, FOURIER_GROUP), BF16),
        compiler_params=_params(2),
        name="chan_dft_perm",
    )(x, shift, scale, g, jnp.asarray(perm, BF16), cs)


_C1 = float(np.cos(np.pi / 8))
_C2 = float(np.cos(np.pi / 4))
_C3 = float(np.cos(3 * np.pi / 8))


def _dft16_real(ur, ui):
    p = [ur[0]] + [ur[a] + ur[16 - a] for a in range(1, 8)] + [ur[8]]
    q = [p[a] + p[8 - a] for a in range(4)] + [p[4]]
    r = [p[a] - p[8 - a] for a in range(4)]
    s0, s1, s2 = q[0] + q[4], q[1] + q[3], q[2]
    t0, t1 = q[0] - q[4], q[1] - q[3]
    ca = [None] * 9
    ca[0] = s0 + s1 + s2
    ca[4] = s0 - s2
    ca[8] = s0 - s1 + s2
    ca[2] = t0 + _C2 * t1
    ca[6] = t0 - _C2 * t1
    e0, e1 = r[0] + _C2 * r[2], r[0] - _C2 * r[2]
    f, g = _C1 * r[1] + _C3 * r[3], _C3 * r[1] - _C1 * r[3]
    ca[1], ca[7], ca[3], ca[5] = e0 + f, e0 - f, e1 + g, e1 - g
    pp = [None] + [ui[a] - ui[16 - a] for a in range(1, 8)]
    qq = [None] + [pp[a] + pp[8 - a] for a in range(1, 4)] + [pp[4]]
    rr = [None] + [pp[a] - pp[8 - a] for a in range(1, 4)]
    sb = [None] * 8
    e0, e1 = _C2 * qq[2] + qq[4], _C2 * qq[2] - qq[4]
    f, g = _C3 * qq[1] + _C1 * qq[3], _C1 * qq[1] - _C3 * qq[3]
    sb[1], sb[7], sb[3], sb[5] = f + e0, f - e0, g + e1, g - e1
    w = _C2 * (rr[1] + rr[3])
    sb[2], sb[6], sb[4] = w + rr[2], w - rr[2], rr[1] - rr[3]
    out = [None] * 16
    out[0], out[8] = ca[0], ca[8]
    for c in range(1, 8):
        out[c] = ca[c] + sb[c]
        out[16 - c] = ca[c] - sb[c]
    return out


def _seq_fft_kernel(y_ref, l_ref, o_ref, u_even_ref, u_odd_ref):
    t = pl.program_id(0)

    @pl.when(t == 0)
    def _():
        u_odd_ref[...] = jnp.zeros(u_odd_ref.shape, BF16)

    def step(cur_ref, prev_ref):
        rt = 2 * SUBLANES
        for i in range(FFT_INNER // rt):
            r0 = i * rt
            for lh in range(FOURIER_GROUP // LANES):
                ls = slice(lh * LANES, (lh + 1) * LANES)
                ur = [prev_ref[a, r0:r0 + rt, ls].astype(F32) for a in range(FFT_RADIX)]
                ui = [prev_ref[a, FFT_INNER + r0:FFT_INNER + r0 + rt, ls].astype(F32) for a in range(FFT_RADIX)]
                out = _dft16_real(ur, ui)
                for c in range(FFT_RADIX):
                    o_ref[0, c * FFT_INNER + r0:c * FFT_INNER + r0 + rt, ls] = out[c].astype(BF16)
        for a in range(FFT_RADIX):
            u = jnp.dot(l_ref[a, :, :FFT_INNER], y_ref[a, 0], preferred_element_type=F32)
            cur_ref[a] = (u + jnp.dot(l_ref[a, :, FFT_INNER:], y_ref[a, 1], preferred_element_type=F32)).astype(BF16)

    pl.when(t % 2 == 0)(lambda: step(u_even_ref, u_odd_ref))
    pl.when(t % 2 == 1)(lambda: step(u_odd_ref, u_even_ref))


def _seq_fft(ycat, lmat):
    b = ycat.shape[0]
    ng = N_FOURIER_GROUPS
    items = b * ng

    def y_map(t):
        tt = jnp.minimum(t, items - 1)
        return (tt // ng, tt % ng, 0, 0, 0, 0)

    def o_map(t):
        tp = jnp.maximum(t - 1, 0)
        return (tp // ng, 0, tp % ng)

    return pl.pallas_call(
        _seq_fft_kernel,
        grid=(items + 1,),
        in_specs=[pl.BlockSpec((None, None, FFT_RADIX, 2, FFT_INNER, FOURIER_GROUP), y_map),
                  _const_spec((FFT_RADIX, 2 * FFT_INNER, 2 * FFT_INNER))],
        out_specs=pl.BlockSpec((1, FFT_N, FOURIER_GROUP), o_map),
        out_shape=jax.ShapeDtypeStruct((b, FFT_N, D_MODEL), BF16),
        scratch_shapes=[pltpu.VMEM((FFT_RADIX, 2 * FFT_INNER, FOURIER_GROUP), BF16)] * 2,
        compiler_params=_params(1),
        name="seq_fft",
    )(ycat, lmat)


def _fft_stage_mats():
    a = np.arange(FFT_RADIX)[:, None, None]
    d = np.arange(FFT_INNER)[None, :, None]
    bb = np.arange(FFT_INNER)[None, None, :]
    ang = ((d * (a + FFT_RADIX * bb)) % FFT_N) * (2.0 * np.pi / FFT_N)
    c = np.cos(ang) * (FFT_N ** -0.5)
    s = np.sin(ang) * (FFT_N ** -0.5)
    top = np.concatenate([c, -s], axis=2)
    bot = np.concatenate([-s, -c], axis=2)
    return jnp.asarray(np.concatenate([top, bot], axis=1).astype(np.float32), BF16)


FF_CHUNK = 256
TAIL_ROWS = 1024


def _tail_kernel(*refs, final):
    if final:
        (x_ref, m_ref, g1_ref, wm_ref, sh_ref, sc_ref, gate_ref, g_ref, wg_ref, wu_ref, wd_ref,
         fg_ref, o_ref, acc_ref) = refs
    else:
        (x_ref, m_ref, g1_ref, wm_ref, sh_ref, sc_ref, gate_ref, g_ref, wg_ref, wu_ref, wd_ref,
         o_ref, acc_ref) = refs
    x = x_ref[0] + g1_ref[0] * jnp.dot(m_ref[0], wm_ref[...], preferred_element_type=F32)
    h = _norm_mod(x, g_ref[...], sh_ref[0], sc_ref[0]).astype(BF16)
    for c in range(D_FF // FF_CHUNK):
        cs = slice(c * FF_CHUNK, (c + 1) * FF_CHUNK)
        gt = jnp.dot(h, wg_ref[:, cs], preferred_element_type=F32)
        up = jnp.dot(h, wu_ref[:, cs], preferred_element_type=F32)
        acc_ref[:, cs] = (_silu(gt) * up).astype(BF16)
    y = x + gate_ref[0] * jnp.dot(acc_ref[...], wd_ref[...], preferred_element_type=F32)
    if final:
        ms = jnp.mean(y * y, axis=-1, keepdims=True)
        y = (y * lax.rsqrt(ms + EPS)) * fg_ref[...]
    o_ref[0] = y


def _tail(x, m, gate1, wm, wm_layer, shift, scale, gate, g, wg, wu, wd, layer, tm, final_g=None):
    b, n, _ = x.shape
    tok = pl.BlockSpec((1, tm, D_MODEL), lambda i, bb: (bb, i, 0))
    mod = pl.BlockSpec((1, 1, D_MODEL), lambda i, bb: (bb, 0, 0))
    in_specs = [tok, tok, mod, _layer_spec((D_MODEL, D_MODEL), wm_layer), mod, mod, mod, _const_spec((1, D_MODEL)),
                _layer_spec((D_MODEL, D_FF), layer), _layer_spec((D_MODEL, D_FF), layer),
                _layer_spec((D_FF, D_MODEL), layer)]
    args = [x, m, gate1, wm, shift, scale, gate, g, wg, wu, wd]
    if final_g is not None:
        in_specs.append(_const_spec((1, D_MODEL)))
        args.append(final_g)
    return pl.pallas_call(
        functools.partial(_tail_kernel, final=final_g is not None),
        grid=(n // tm, b),
        in_specs=in_specs,
        out_specs=tok,
        out_shape=jax.ShapeDtypeStruct((b, n, D_MODEL), F32),
        scratch_shapes=[pltpu.VMEM((tm, D_FF), BF16)],
        compiler_params=_params(2),
        name="tail",
    )(*args)


def _rope_lane_tables(n):
    rows = n // GRID_W
    r = np.repeat(np.arange(rows), GRID_W).astype(np.float64)
    col = np.tile(np.arange(GRID_W), rows).astype(np.float64)
    quarter = HEAD_DIM // 4
    inv = ROPE_THETA ** (-np.arange(quarter, dtype=np.float64) / quarter)
    ang_r = r[:, None] * inv
    ang_c = col[:, None] * inv
    cr, sr, cc, sc = np.cos(ang_r), np.sin(ang_r), np.cos(ang_c), np.sin(ang_c)
    z = np.zeros_like(sr)
    cos = np.concatenate([cr, cr, cc, cc], axis=1)
    sa = np.concatenate([-sr, z, -sc, z], axis=1)
    sb = np.concatenate([z, sr, z, sc], axis=1)
    rep = lambda t: jnp.asarray(np.concatenate([t, t], axis=1).astype(np.float32))
    return rep(cos), rep(sa), rep(sb)


def _identity_lane_tables(n):
    one, zero = np.ones((n, LANES), np.float32), np.zeros((n, LANES), np.float32)
    return jnp.asarray(one), jnp.asarray(zero), jnp.asarray(zero)


def kernel(x, c, ctx, c_ctx, w_ada, b_ada, norm1_g, norm2_g, w_in, conv_w, sink,
           w_mix_out, w_fourier_out, w_ffn_gate, w_ffn_up, w_ffn_down, final_g):
    b, n, d = x.shape
    L = ctx.shape[1]
    nc = b * L

    cond = jnp.concatenate([c, c_ctx[None, :], jnp.zeros((COND_ROWS - b - 1, d), F32)], axis=0)
    mods = _adaln(cond, w_ada, b_ada)

    def split(l, lo, hi):
        m = mods[l, lo:hi]
        return [m[:, None, k * d:(k + 1) * d] for k in range(6)]

    rope_lat = _rope_lane_tables(n)
    rope_ctx = _identity_lane_tables(nc)
    w_in_b, w_mix_b, w_fo_b = w_in.astype(BF16), w_mix_out.astype(BF16), w_fourier_out.astype(BF16)
    wg_b, wu_b, wd_b = w_ffn_gate.astype(BF16), w_ffn_up.astype(BF16), w_ffn_down.astype(BF16)
    cc, sc_ = _dft_mats(FOURIER_GROUP, FOURIER_GROUP ** -0.5)
    cs_chan = jnp.asarray(np.concatenate([cc, sc_], axis=1), BF16)
    seq_mats = {}
    for m in {n, L} - {FFT_N}:
        cm, sm = _dft_mats(m, m ** -0.5)
        seq_mats[m] = (jnp.asarray(cm, BF16), jnp.asarray(-sm, BF16))
    fft_l = _fft_stage_mats() if n == FFT_N else None

    xc = ctx.reshape(1, nc, d)
    per_seq = lambda t: t.reshape(b, L, t.shape[-1])
    for l in range(DEPTH):
        ctx_after = any(j % 2 == 0 for j in range(l + 1, DEPTH))
        ctx_here = (l % 2 == 0) or ctx_after
        sh1, sc1, g1, sh2, sc2, g2 = split(l, 0, b)
        if ctx_here:
            csh1, csc1, cg1, csh2, csc2, cg2 = split(l, b, b + 1)
        n1 = norm1_g[l][None, :]
        n2 = norm2_g[l][None, :]
        if l % 2 == 0:
            e = l // 2
            wm, wm_layer = w_mix_b, e
            bgu_c, q_c, kk_c, vv_c = _inproj(xc, csh1, csc1, n1, w_in_b, e, *rope_ctx, tm=min(nc, INPROJ_ROWS))
            kk_c, vv_c = per_seq(kk_c), per_seq(vv_c)
            bgu, q, kk, vv = _inproj(x, sh1, sc1, n1, w_in_b, e, *rope_lat, tm=min(n, INPROJ_ROWS))
            mix = _mixout(bgu, q, kk, vv, kk_c, vv_c, sink[e], conv_w[e], tile_rows=min(n, MIX_ROWS), local=True)
            if ctx_after:
                mix_c = _mixout(per_seq(bgu_c), per_seq(q_c), None, None, kk_c, vv_c, sink[e], conv_w[e],
                                tile_rows=L, local=False)
        else:
            wm, wm_layer = w_fo_b, l // 2
            if n == FFT_N:
                mix = _seq_fft(_chan_dft_perm(x, sh1, sc1, n1, cs_chan), fft_l)
            else:
                mix = _seq_dft(*seq_mats[n], *_chan_dft(x, sh1, sc1, n1, cs_chan, tm=min(n, CHAN_DFT_ROWS)),
                               tm=min(n, SEQ_DFT_ROWS))
            if ctx_after:
                y1c, y2c = _chan_dft(xc, csh1, csc1, n1, cs_chan, tm=min(nc, CHAN_DFT_ROWS))
                mix_c = _seq_dft(*seq_mats[L], per_seq(y1c), per_seq(y2c), tm=L)
        x = _tail(x, mix, g1, wm, wm_layer, sh2, sc2, g2, n2, wg_b, wu_b, wd_b, l, tm=min(n, TAIL_ROWS),
                  final_g=final_g[None, :] if l == DEPTH - 1 else None)
        if ctx_after:
            xc = _tail(xc, mix_c.reshape(1, nc, d), cg1, wm, wm_layer, csh2, csc2, cg2, n2, wg_b, wu_b, wd_b, l,
                       tm=min(nc, TAIL_ROWS))
    return x
```
